```python
import math
import jax, jax.numpy as jnp
from jax import lax
import numpy as np

D_MODEL = 1024
BATCH = 2
SEQ = 16384
DEPTH = 2
DEC_BATCH = 16
DEC_SEQ = 64
PAST_LEN = 2048

CHUNK = 64
Q_BLOCK = 128
HEAD_DIM = 64
N_HEADS = D_MODEL // HEAD_DIM
N_MIXERS = 4
HEADS_PER_MIXER = N_HEADS // N_MIXERS
GROUP_W = HEADS_PER_MIXER * HEAD_DIM
ROT_FRACTION = 4
ROPE_THETA = 500000.0
IDX_HEADS = 8
IDX_DIM = 64
TOPK_MAX = 256
BAND_CHUNKS = 8
REL_CLIP = 128
DIFF_DIM = HEAD_DIM // 2
MEM_TOKENS = 256
X_HEADS = 4
X_HEAD_DIM = 128
X_WIDTH = X_HEADS * X_HEAD_DIM
D_FF = 2816
CONV_W = 3
EPS = 1e-6
COL_SIZES = (GROUP_W, GROUP_W, GROUP_W, IDX_HEADS * IDX_DIM, IDX_DIM, IDX_HEADS,
             GROUP_W, GROUP_W, GROUP_W, HEADS_PER_MIXER,
             GROUP_W, GROUP_W, GROUP_W,
             GROUP_W, GROUP_W, GROUP_W)
IN_COLS = sum(COL_SIZES)

kernel_name = 'hybrid_streaming_encoder_step'


def rmsnorm(x, g):
    xf = x.astype(jnp.float32)
    y = xf * lax.rsqrt(jnp.mean(xf * xf, axis=-1, keepdims=True) + EPS)
    return y.astype(x.dtype) * g


def rope_partial(x, pos, rot_dim):
    half = rot_dim // 2
    inv = ROPE_THETA ** (-jnp.arange(half, dtype=jnp.float32) / half)
    ang = pos.astype(jnp.float32)[:, None] * inv[None, :]
    bshape = (pos.shape[0],) + (1,) * (x.ndim - 3) + (half,)
    cos = jnp.cos(ang).reshape(bshape)
    sin = jnp.sin(ang).reshape(bshape)
    xr = x[..., :rot_dim].astype(jnp.float32)
    x1, x2 = xr[..., :half], xr[..., half:]
    rot = jnp.concatenate([x1 * cos - x2 * sin, x2 * cos + x1 * sin], axis=-1)
    return jnp.concatenate([rot.astype(x.dtype), x[..., rot_dim:]], axis=-1)


def to_blocks(x):
    b, s = x.shape[:2]
    return jnp.moveaxis(x.reshape((b, s // Q_BLOCK, Q_BLOCK) + x.shape[2:]), 1, 0)


def from_blocks(y):
    y = jnp.moveaxis(y, 0, 1)
    return y.reshape((y.shape[0], y.shape[1] * y.shape[2]) + y.shape[3:])


def pos_blocks(s):
    return jnp.arange(s, dtype=jnp.int32).reshape(s // Q_BLOCK, Q_BLOCK)


def masked_softmax(s, mask):
    return jax.nn.softmax(jnp.where(mask, s, -jnp.inf), axis=-1)


def project_mixers(h, w_in_l, b_forget_l, pos):
    b, t, _ = h.shape
    points = [int(p) for p in np.cumsum(COL_SIZES)[:-1]]
    (aq, ak, av, aqi, aki, aw, bq, bk, bv, bf, cq, ck, cv, dq, dk, dv) = jnp.split(h @ w_in_l, points, axis=-1)
    heads = lambda z: z.reshape(b, t, HEADS_PER_MIXER, HEAD_DIM)
    pair = lambda z: z.reshape(b, t, HEADS_PER_MIXER, 2, DIFF_DIM)
    rot = HEAD_DIM // ROT_FRACTION
    irot = IDX_DIM // ROT_FRACTION
    drot = DIFF_DIM // ROT_FRACTION
    a_q = rope_partial(heads(aq), pos, rot)
    a_k = rope_partial(heads(ak), pos, rot)
    a_qi = rope_partial(aqi.reshape(b, t, IDX_HEADS, IDX_DIM), pos, irot)
    a_ki = rope_partial(aki[:, :, None, :], pos, irot)[:, :, 0]
    a_w = aw * IDX_HEADS ** -0.5
    b_logf = jax.nn.log_sigmoid((bf + b_forget_l).astype(jnp.float32))
    d_q = rope_partial(pair(dq), pos, drot).reshape(b, t, HEADS_PER_MIXER, HEAD_DIM)
    d_k = rope_partial(pair(dk), pos, drot).reshape(b, t, HEADS_PER_MIXER, HEAD_DIM)
    return (a_q, a_k, heads(av), a_qi, a_ki, a_w,
            heads(bq), heads(bk), heads(bv), b_logf,
            heads(cq), heads(ck), heads(cv),
            d_q, d_k, heads(dv))


def dsa_attend(q, qi, wi, k, v, ki, adm, ksel):
    score = jax.nn.relu(jnp.einsum('bqgd,bsd->bqgs', qi, ki).astype(jnp.float32) * IDX_DIM ** -0.5)
    index = jnp.einsum('bqgs,bqg->bqs', score, wi.astype(jnp.float32))
    index = jnp.where(adm[None], index, -jnp.inf)
    _, sel = lax.top_k(index, ksel)
    valid = jnp.take_along_axis(jnp.broadcast_to(adm[None], index.shape), sel, axis=-1)
    gather = jax.vmap(lambda rows, ids: rows[ids])
    kg = gather(k, sel)
    vg = gather(v, sel)
    s = jnp.einsum('bqhd,bqkhd->bhqk', q, kg).astype(jnp.float32) * HEAD_DIM ** -0.5
    p = masked_softmax(s, valid[:, None])
    return jnp.einsum('bhqk,bqkhd->bqhd', p.astype(vg.dtype), vg)


def dsa_prompt(q, k, v, qi, ki, wi):
    s = q.shape[1]
    ksel = min(TOPK_MAX, s // 4)
    kchunk = jnp.arange(s, dtype=jnp.int32) // CHUNK

    def block(args):
        qb, qib, wib, qpos = args
        adm = kchunk[None, :] <= (qpos // CHUNK)[:, None]
        return dsa_attend(qb, qib, wib, k, v, ki, adm, ksel)

    return from_blocks(lax.map(block, (to_blocks(q), to_blocks(qi), to_blocks(wi), pos_blocks(s))))


def fox_attend(q, cq, k, v, ck, mask):
    s = jnp.einsum('bqhd,bshd->bhqs', q, k).astype(jnp.float32) * HEAD_DIM ** -0.5
    s = s + jnp.moveaxis(cq, 2, 1)[..., None] - jnp.moveaxis(ck, 2, 1)[:, :, None, :]
    p = masked_softmax(s, mask[None, None])
    return jnp.einsum('bhqs,bshd->bqhd', p.astype(v.dtype), v)


def fox_prompt(q, k, v, logf):
    s = q.shape[1]
    c = jnp.cumsum(logf, axis=1)
    kpos = jnp.arange(s, dtype=jnp.int32)

    def block(args):
        qb, cqb, qpos = args
        return fox_attend(qb, cqb, k, v, c, kpos[None, :] <= qpos[:, None])

    return from_blocks(lax.map(block, (to_blocks(q), to_blocks(c), pos_blocks(s))))


def band_attend(q, k, v, bias, valid):
    s = jnp.einsum('...qhd,...khd->...hqk', q, k).astype(jnp.float32) * HEAD_DIM ** -0.5 + bias
    p = masked_softmax(s, valid)
    return jnp.einsum('...hqk,...khd->...qhd', p.astype(v.dtype), v)


def band_prompt(q, k, v, rel_bias_l):
    b, s, h, d = q.shape
    nc = s // CHUNK
    nband = BAND_CHUNKS + 1
    pad = jnp.zeros((b, BAND_CHUNKS * CHUNK, h, d), k.dtype)
    kp = jnp.concatenate([pad, k], axis=1).reshape(b, nc + BAND_CHUNKS, CHUNK, h, d)
    vp = jnp.concatenate([pad, v], axis=1).reshape(b, nc + BAND_CHUNKS, CHUNK, h, d)
    kb = jnp.stack([kp[:, j:j + nc] for j in range(nband)], axis=2).reshape(b, nc, nband * CHUNK, h, d)
    vb = jnp.stack([vp[:, j:j + nc] for j in range(nband)], axis=2).reshape(b, nc, nband * CHUNK, h, d)
    band_chunk = jnp.arange(nc)[:, None] + jnp.arange(nband)[None, :] - BAND_CHUNKS
    valid = jnp.repeat(band_chunk >= 0, CHUNK, axis=1)
    qi = jnp.arange(CHUNK)
    kj = jnp.arange(nband * CHUNK)
    rel = BAND_CHUNKS * CHUNK + qi[:, None] - kj[None, :]
    bias = rel_bias_l[:, jnp.clip(rel, -REL_CLIP, REL_CLIP) + REL_CLIP]
    o = band_attend(q.reshape(b, nc, CHUNK, h, d), kb, vb, bias, valid[None, :, None, None, :])
    return o.reshape(b, s, h, d)


def band_sample(q, k_new, v_new, ck, cv, rel_bias_l, past):
    t = q.shape[1]
    c_len = ck.shape[1]
    k = jnp.concatenate([ck, k_new], axis=1)
    v = jnp.concatenate([cv, v_new], axis=1)
    kpos = jnp.concatenate([past - c_len + jnp.arange(c_len), past + jnp.arange(t)])
    qpos = past + jnp.arange(t)
    rel = jnp.clip(qpos[:, None] - kpos[None, :], -REL_CLIP, REL_CLIP) + REL_CLIP
    bias = rel_bias_l[:, rel]
    qc, kc = qpos // CHUNK, kpos // CHUNK
    valid = (kc[None, :] <= qc[:, None]) & (kc[None, :] >= qc[:, None] - BAND_CHUNKS)
    return band_attend(q, k, v, bias, valid[None, None])


def diff_attend(q, k, v, mask, lam):
    sc = DIFF_DIM ** -0.5
    s1 = jnp.einsum('bqhd,bshd->bhqs', q[..., :DIFF_DIM], k[..., :DIFF_DIM]).astype(jnp.float32) * sc
    s2 = jnp.einsum('bqhd,bshd->bhqs', q[..., DIFF_DIM:], k[..., DIFF_DIM:]).astype(jnp.float32) * sc
    p = masked_softmax(s1, mask[None, None]) - lam * masked_softmax(s2, mask[None, None])
    return jnp.einsum('bhqs,bshd->bqhd', p.astype(v.dtype), v)


def diff_prompt(q, k, v, lam):
    s = q.shape[1]
    kchunk = jnp.arange(s, dtype=jnp.int32) // CHUNK

    def block(args):
        qb, qpos = args
        return diff_attend(qb, k, v, kchunk[None, :] <= (qpos // CHUNK)[:, None], lam)

    return from_blocks(lax.map(block, (to_blocks(q), pos_blocks(s))))


def diff_out(o, g_sub_l, lam_init):
    return rmsnorm(o, g_sub_l) * (1.0 - lam_init)


def merge_heads(oa, ob, oc, od, w_out_l):
    b, t = oa.shape[:2]
    return jnp.concatenate([oa, ob, oc, od], axis=2).reshape(b, t, N_HEADS * HEAD_DIM) @ w_out_l


def memory_kv(mem, g_mem_l, w_xk_l, w_xv_l):
    b, m, _ = mem.shape
    mn = rmsnorm(mem, g_mem_l)
    return ((mn @ w_xk_l).reshape(b, m, X_HEADS, X_HEAD_DIM),
            (mn @ w_xv_l).reshape(b, m, X_HEADS, X_HEAD_DIM))


def cross_attend(h, mk, mv, w_xq_l, w_xo_l):
    b, t, _ = h.shape
    q = (h @ w_xq_l).reshape(b, t, X_HEADS, X_HEAD_DIM)
    s = jnp.einsum('bthd,bmhd->bhtm', q, mk).astype(jnp.float32) * X_HEAD_DIM ** -0.5
    p = jax.nn.softmax(s, axis=-1)
    o = jnp.einsum('bhtm,bmhd->bthd', p.astype(mv.dtype), mv).reshape(b, t, X_WIDTH)
    return o @ w_xo_l


def conv_ffn(h, prev, w_up_l, conv_w_l, conv_b_l, w_down_l):
    g, u = jnp.split(h @ w_up_l, 2, axis=-1)
    t = g.shape[1]
    gp = jnp.concatenate([prev.astype(g.dtype), g], axis=1)
    acc = conv_b_l
    for j in range(CONV_W):
        acc = acc + gp[:, j:j + t] * conv_w_l[j]
    y = jax.nn.silu(acc) * u
    return y @ w_down_l, gp[:, gp.shape[1] - (CONV_W - 1):]


def setup_inputs(seed: int = 0) -> dict:
    key = jax.random.key(seed)
    ks = iter(jax.random.split(key, 48))
    nrm = lambda shape, scale: jax.random.normal(next(ks), shape, jnp.float32) * scale
    gain = lambda shape: 1.0 + nrm(shape, 0.01)
    c_len = min(BAND_CHUNKS * CHUNK, PAST_LEN)
    H, Dh = HEADS_PER_MIXER, HEAD_DIM
    return {
        'x_prompt': nrm((BATCH, SEQ, D_MODEL), 1.0),
        'x_sample': nrm((DEC_BATCH, DEC_SEQ, D_MODEL), 1.0),
        'mem_prompt': nrm((BATCH, MEM_TOKENS, D_MODEL), 1.0),
        'cache_a_k': nrm((DEPTH, DEC_BATCH, PAST_LEN, H, Dh), 1.0),
        'cache_a_v': nrm((DEPTH, DEC_BATCH, PAST_LEN, H, Dh), 1.0),
        'cache_a_kidx': nrm((DEPTH, DEC_BATCH, PAST_LEN, IDX_DIM), 1.0),
        'cache_b_k': nrm((DEPTH, DEC_BATCH, PAST_LEN, H, Dh), 1.0),
        'cache_b_v': nrm((DEPTH, DEC_BATCH, PAST_LEN, H, Dh), 1.0),
        'cache_b_logf': jax.nn.log_sigmoid(nrm((DEPTH, DEC_BATCH, PAST_LEN, H), 1.0) + 3.0),
        'cache_c_k': nrm((DEPTH, DEC_BATCH, c_len, H, Dh), 1.0),
        'cache_c_v': nrm((DEPTH, DEC_BATCH, c_len, H, Dh), 1.0),
        'cache_d_k': nrm((DEPTH, DEC_BATCH, PAST_LEN, H, Dh), 1.0),
        'cache_d_v': nrm((DEPTH, DEC_BATCH, PAST_LEN, H, Dh), 1.0),
        'cache_mem_k': nrm((DEPTH, DEC_BATCH, MEM_TOKENS, X_HEADS, X_HEAD_DIM), 1.0),
        'cache_mem_v': nrm((DEPTH, DEC_BATCH, MEM_TOKENS, X_HEADS, X_HEAD_DIM), 1.0),
        'state_ffn_conv': nrm((DEPTH, DEC_BATCH, CONV_W - 1, D_FF), 1.0),
        'g_mix': gain((DEPTH, D_MODEL)),
        'w_in': nrm((DEPTH, D_MODEL, IN_COLS), D_MODEL ** -0.5),
        'b_forget': 3.0 + nrm((DEPTH, H), 0.1),
        'rel_bias': nrm((DEPTH, H, 2 * REL_CLIP + 1), 0.5),
        'lam_q1': nrm((DEPTH, DIFF_DIM), 0.1),
        'lam_k1': nrm((DEPTH, DIFF_DIM), 0.1),
        'lam_q2': nrm((DEPTH, DIFF_DIM), 0.1),
        'lam_k2': nrm((DEPTH, DIFF_DIM), 0.1),
        'g_sub': gain((DEPTH, HEAD_DIM)),
        'w_out': nrm((DEPTH, N_HEADS * HEAD_DIM, D_MODEL), (N_HEADS * HEAD_DIM) ** -0.5),
        'g_cross': gain((DEPTH, D_MODEL)),
        'g_mem': gain((DEPTH, D_MODEL)),
        'w_xq': nrm((DEPTH, D_MODEL, X_WIDTH), D_MODEL ** -0.5),
        'w_xk': nrm((DEPTH, D_MODEL, X_WIDTH), D_MODEL ** -0.5),
        'w_xv': nrm((DEPTH, D_MODEL, X_WIDTH), D_MODEL ** -0.5),
        'w_xo': nrm((DEPTH, X_WIDTH, D_MODEL), X_WIDTH ** -0.5),
        'g_ffn': gain((DEPTH, D_MODEL)),
        'w_up': nrm((DEPTH, D_MODEL, 2 * D_FF), D_MODEL ** -0.5),
        'conv_w': nrm((DEPTH, CONV_W, D_FF), CONV_W ** -0.5),
        'conv_b': nrm((DEPTH, D_FF), 0.01),
        'w_down': nrm((DEPTH, D_FF, D_MODEL), D_FF ** -0.5),
        'g_final': gain((D_MODEL,)),
    }


def reference(x_prompt, x_sample, mem_prompt, cache_a_k, cache_a_v, cache_a_kidx, cache_b_k, cache_b_v,
              cache_b_logf, cache_c_k, cache_c_v, cache_d_k, cache_d_v, cache_mem_k, cache_mem_v,
              state_ffn_conv, g_mix, w_in, b_forget, rel_bias, lam_q1, lam_k1, lam_q2, lam_k2, g_sub, w_out,
              g_cross, g_mem, w_xq, w_xk, w_xv, w_xo, g_ffn, w_up, conv_w, conv_b, w_down, g_final):
    s_len = x_prompt.shape[1]
    t_len = x_sample.shape[1]
    past = cache_a_k.shape[2]
    pos_p = jnp.arange(s_len, dtype=jnp.int32)
    pos_s = past + jnp.arange(t_len, dtype=jnp.int32)
    kpos_s = jnp.arange(past + t_len, dtype=jnp.int32)
    chunk_mask_s = (kpos_s // CHUNK)[None, :] <= (pos_s // CHUNK)[:, None]
    causal_mask_s = kpos_s[None, :] <= pos_s[:, None]
    ksel_s = min(TOPK_MAX, (past + t_len) // 4)
    c_keep = min(BAND_CHUNKS * CHUNK, s_len)
    cat = lambda a, b: jnp.concatenate([a, b], axis=1)
    xp, xs = x_prompt, x_sample
    prompt_states, sample_states = [], []
    for l in range(DEPTH):
        lam_init = 0.8 - 0.6 * math.exp(-0.3 * l)
        lam = (jnp.exp(jnp.sum(lam_q1[l].astype(jnp.float32) * lam_k1[l].astype(jnp.float32)))
               - jnp.exp(jnp.sum(lam_q2[l].astype(jnp.float32) * lam_k2[l].astype(jnp.float32))) + lam_init)

        h = rmsnorm(xp, g_mix[l])
        (aq, ak, av, aqi, aki, aw, bq, bk, bv, blf, cq, ck, cv, dq, dk, dv) = project_mixers(h, w_in[l], b_forget[l], pos_p)
        oa = dsa_prompt(aq, ak, av, aqi, aki, aw)
        ob = fox_prompt(bq, bk, bv, blf)
        oc = band_prompt(cq, ck, cv, rel_bias[l])
        od = diff_out(diff_prompt(dq, dk, dv, lam), g_sub[l], lam_init)
        xp = xp + merge_heads(oa, ob, oc, od, w_out[l])
        mk, mv = memory_kv(mem_prompt, g_mem[l], w_xk[l], w_xv[l])
        xp = xp + cross_attend(rmsnorm(xp, g_cross[l]), mk, mv, w_xq[l], w_xo[l])
        hf = rmsnorm(xp, g_ffn[l])
        f, conv_p = conv_ffn(hf, jnp.zeros((hf.shape[0], CONV_W - 1, D_FF), hf.dtype),
                             w_up[l], conv_w[l], conv_b[l], w_down[l])
        xp = xp + f
        prompt_states.append((ak, av, aki, bk, bv, blf, ck[:, s_len - c_keep:], cv[:, s_len - c_keep:],
                              dk, dv, conv_p, mk, mv))

        h = rmsnorm(xs, g_mix[l])
        (aq, ak, av, aqi, aki, aw, bq, bk, bv, blf, cq, ck, cv, dq, dk, dv) = project_mixers(h, w_in[l], b_forget[l], pos_s)
        oa = dsa_attend(aq, aqi, aw, cat(cache_a_k[l], ak), cat(cache_a_v[l], av), cat(cache_a_kidx[l], aki),
                        chunk_mask_s, ksel_s)
        c_all = jnp.cumsum(cat(cache_b_logf[l].astype(jnp.float32), blf), axis=1)
        ob = fox_attend(bq, c_all[:, past:], cat(cache_b_k[l], bk), cat(cache_b_v[l], bv), c_all, causal_mask_s)
        oc = band_sample(cq, ck, cv, cache_c_k[l], cache_c_v[l], rel_bias[l], past)
        od = diff_out(diff_attend(dq, cat(cache_d_k[l], dk), cat(cache_d_v[l], dv), chunk_mask_s, lam),
                      g_sub[l], lam_init)
        xs = xs + merge_heads(oa, ob, oc, od, w_out[l])
        xs = xs + cross_attend(rmsnorm(xs, g_cross[l]), cache_mem_k[l], cache_mem_v[l], w_xq[l], w_xo[l])
        f, conv_s = conv_ffn(rmsnorm(xs, g_ffn[l]), state_ffn_conv[l], w_up[l], conv_w[l], conv_b[l], w_down[l])
        xs = xs + f
        sample_states.append((ak, av, aki, bk, bv, blf, ck, cv, dk, dv, conv_s))

    y_prompt = rmsnorm(xp, g_final)
    y_sample = rmsnorm(xs, g_final)
    (p_a_k, p_a_v, p_a_kidx, p_b_k, p_b_v, p_b_logf, p_c_k, p_c_v, p_d_k, p_d_v, p_conv, p_mem_k,
     p_mem_v) = [jnp.stack(z, axis=0) for z in zip(*prompt_states)]
    (s_a_k, s_a_v, s_a_kidx, s_b_k, s_b_v, s_b_logf, s_c_k, s_c_v, s_d_k, s_d_v,
     s_conv) = [jnp.stack(z, axis=0) for z in zip(*sample_states)]
    return (y_prompt, y_sample,
            p_a_k, p_a_v, p_a_kidx, p_b_k, p_b_v, p_b_logf, p_c_k, p_c_v, p_d_k, p_d_v, p_conv, p_mem_k, p_mem_v,
            s_a_k, s_a_v, s_a_kidx, s_b_k, s_b_v, s_b_logf, s_c_k, s_c_v, s_d_k, s_d_v, s_conv)
```

```python
import functools
import math

import numpy as np
import jax
import jax.numpy as jnp
from jax import lax
from jax.experimental import pallas as pl
from jax.experimental.pallas import tpu as pltpu

F32 = jnp.float32
BF16 = jnp.bfloat16
I32 = jnp.int32

CHUNK = 64
HEAD_DIM = 64
HEADS = 4
GROUP_W = HEADS * HEAD_DIM
ROT_FRACTION = 4
ROPE_THETA = 500000.0
IDX_HEADS = 8
IDX_DIM = 64
TOPK_MAX = 256
BAND_CHUNKS = 8
BAND_KEYS = (BAND_CHUNKS + 1) * CHUNK
BAND_PREV = BAND_CHUNKS * CHUNK
REL_CLIP = 128
DIFF_DIM = HEAD_DIM // 2
X_HEADS = 4
X_HEAD_DIM = 128
CONV_W = 3
EPS = 1e-6

LANES = 128
VMEM_LIMIT_BYTES = 56 * 1024 * 1024

NEG = -1e30
INT_MIN = -2147483648
INT_MAX = 2147483647

MISC_W = LANES
MISC_AW = IDX_DIM
MISC_BF = IDX_DIM + IDX_HEADS
PROJ_COLS = 3 * GROUP_W + IDX_HEADS * IDX_DIM + MISC_W + 9 * GROUP_W


def _cparams(n_axes):
    return pltpu.CompilerParams(dimension_semantics=("arbitrary",) * n_axes,
                                vmem_limit_bytes=VMEM_LIMIT_BYTES)


def _rms(x, g):
    return (x * lax.rsqrt(jnp.mean(x * x, axis=-1, keepdims=True) + EPS)) * g


def _dot(a, b):
    return jnp.dot(a, b, preferred_element_type=F32)


def _dot_nt(a, b):
    return lax.dot_general(a, b, (((1,), (1,)), ((), ())), preferred_element_type=F32)


def _rope(z, tab, half):
    w = z.shape[1]
    reps = w // LANES
    rep = lambda t: t if reps == 1 else jnp.concatenate([t] * reps, axis=1)
    c = rep(tab[:, 0:LANES])
    s1 = rep(tab[:, LANES:2 * LANES])
    s2 = rep(tab[:, 2 * LANES:3 * LANES])
    return z * c + pltpu.roll(z, w - half, 1) * s1 + pltpu.roll(z, half, 1) * s2


def _proj_kernel(x_ref, g_ref, w_ref, bf_ref, t64_ref, t32_ref,
                 aq_o, ak_o, av_o, aqi_o, aki_o, bq_o, bk_o, bv_o, cq_o, ck_o, cv_o, dq_o, dk_o, dv_o,
                 akf_o, avf_o, misc_o, bkf_o, bvf_o, ckf_o, cvf_o, dkf_o, dvf_o):
    hb = _rms(x_ref[...], g_ref[...]).astype(BF16)
    t64 = t64_ref[...]
    t32 = t32_ref[...]
    h64 = HEAD_DIM // ROT_FRACTION // 2
    h32 = DIFF_DIM // ROT_FRACTION // 2
    qscale = HEAD_DIM ** -0.5
    col = [0]

    def mm(width):
        c0 = col[0]
        col[0] = c0 + width
        return _dot(hb, w_ref[:, c0:c0 + width])

    z = _rope(mm(GROUP_W), t64, h64)
    aq_o[...] = (z * qscale).astype(BF16)
    z = _rope(mm(GROUP_W), t64, h64)
    akf_o[...] = z
    ak_o[...] = z.astype(BF16)
    z = mm(GROUP_W)
    avf_o[...] = z
    av_o[...] = z.astype(BF16)
    z = _rope(mm(IDX_HEADS * IDX_DIM), t64, h64)
    aqi_o[...] = (z * (IDX_DIM ** -0.5)).astype(BF16)
    z = mm(MISC_W)
    r = _rope(z, t64, h64)
    lane = lax.broadcasted_iota(I32, z.shape, 1)
    zf = z + bf_ref[...]
    logsig = jnp.minimum(zf, 0.0) - jnp.log1p(jnp.exp(-jnp.abs(zf)))
    misc_o[...] = jnp.where(lane < MISC_AW, r, jnp.where(lane < MISC_BF, z * (IDX_HEADS ** -0.5), logsig))
    aki_o[...] = jnp.where(lane < MISC_AW, r, 0.0).astype(BF16)
    z = mm(GROUP_W)
    bq_o[...] = (z * qscale).astype(BF16)
    z = mm(GROUP_W)
    bkf_o[...] = z
    bk_o[...] = z.astype(BF16)
    z = mm(GROUP_W)
    bvf_o[...] = z
    bv_o[...] = z.astype(BF16)
    z = mm(GROUP_W)
    cq_o[...] = (z * qscale).astype(BF16)
    z = mm(GROUP_W)
    ckf_o[...] = z
    ck_o[...] = z.astype(BF16)
    z = mm(GROUP_W)
    cvf_o[...] = z
    cv_o[...] = z.astype(BF16)
    z = _rope(mm(GROUP_W), t32, h32)
    dq_o[...] = (z * (DIFF_DIM ** -0.5)).astype(BF16)
    z = _rope(mm(GROUP_W), t32, h32)
    dkf_o[...] = z
    dk_o[...] = z.astype(BF16)
    z = mm(GROUP_W)
    dvf_o[...] = z
    dv_o[...] = z.astype(BF16)


def _rope_table(pos, half, period):
    inv = ROPE_THETA ** (-jnp.arange(half, dtype=F32) / half)
    ang = pos.astype(F32)[:, None] * inv[None, :]
    cos, sin = jnp.cos(ang), jnp.sin(ang)
    t = pos.shape[0]
    rest = period - 2 * half
    zh = jnp.zeros((t, half), F32)
    zr = jnp.zeros((t, rest), F32)
    c = jnp.concatenate([cos, cos, jnp.ones((t, rest), F32)], axis=1)
    s1 = jnp.concatenate([-sin, zh, zr], axis=1)
    s2 = jnp.concatenate([zh, sin, zr], axis=1)
    rep = lambda a: jnp.tile(a, (1, LANES // period))
    return jnp.concatenate([rep(c), rep(s1), rep(s2)], axis=1)


def _pack_w_in(w_in_l, b_forget_l):
    sizes = (GROUP_W, GROUP_W, GROUP_W, IDX_HEADS * IDX_DIM, IDX_DIM, IDX_HEADS,
             GROUP_W, GROUP_W, GROUP_W, HEADS, GROUP_W, GROUP_W, GROUP_W, GROUP_W, GROUP_W, GROUP_W)
    pts = [int(p) for p in np.cumsum(sizes)[:-1]]
    (aq, ak, av, aqi, aki, aw, bq, bk, bv, bf, cq, ck, cv, dq, dk, dv) = jnp.split(w_in_l, pts, axis=1)
    d = w_in_l.shape[0]
    pad = jnp.zeros((d, MISC_W - IDX_DIM - IDX_HEADS - HEADS), w_in_l.dtype)
    misc = jnp.concatenate([aki, aw, bf, pad], axis=1)
    w = jnp.concatenate([aq, ak, av, aqi, misc, bq, bk, bv, cq, ck, cv, dq, dk, dv], axis=1).astype(BF16)
    bfp = jnp.zeros((1, MISC_W), F32).at[0, MISC_BF:MISC_BF + HEADS].set(b_forget_l.astype(F32))
    return w, bfp


def _proj(x, g, w, bfp, t64, t32, c_keep):
    b, t, d = x.shape
    tm = min(512, t)
    nt = t // tm
    nkeep = c_keep // tm
    tok = lambda width: pl.BlockSpec((None, tm, width), lambda bi, i: (bi, i, 0))
    ctok = pl.BlockSpec((None, tm, GROUP_W), lambda bi, i: (bi, jnp.maximum(i - (nt - nkeep), 0), 0))
    const = lambda shape: pl.BlockSpec(shape, lambda bi, i: (0, 0))
    tab = pl.BlockSpec((tm, 3 * LANES), lambda bi, i: (i, 0))
    sds = lambda width, dt, rows=t: jax.ShapeDtypeStruct((b, rows, width), dt)
    bf_names = ["aq", "ak", "av", "aqi", "aki", "bq", "bk", "bv", "cq", "ck", "cv", "dq", "dk", "dv"]
    bf_w = [GROUP_W, GROUP_W, GROUP_W, IDX_HEADS * IDX_DIM, MISC_W] + [GROUP_W] * 9
    f_names = ["akf", "avf", "misc", "bkf", "bvf", "ckf", "cvf", "dkf", "dvf"]
    f_w = [GROUP_W, GROUP_W, MISC_W] + [GROUP_W] * 6
    out_shape = [sds(wd, BF16) for wd in bf_w]
    out_specs = [tok(wd) for wd in bf_w]
    for n, wd in zip(f_names, f_w):
        if n in ("ckf", "cvf"):
            out_shape.append(sds(wd, F32, c_keep))
            out_specs.append(ctok)
        else:
            out_shape.append(sds(wd, F32))
            out_specs.append(tok(wd))
    outs = pl.pallas_call(
        _proj_kernel,
        grid=(b, nt),
        in_specs=[tok(d), const((1, d)), const((d, PROJ_COLS)), const((1, MISC_W)), tab, tab],
        out_specs=out_specs,
        out_shape=out_shape,
        compiler_params=_cparams(2),
        name="proj",
    )(x, g.reshape(1, d), w, bfp, t64, t32)
    return dict(zip(bf_names + f_names, outs))


def _cumsum_kernel(x_ref, o_ref, *, nb):
    x = x_ref[...]
    lane = lax.broadcasted_iota(I32, x.shape, 1)
    d = 1
    while d < LANES:
        x = x + jnp.where(lane >= d, pltpu.roll(x, d, 1), 0.0)
        d *= 2
    row = lax.broadcasted_iota(I32, x.shape, 0) % nb
    tot = jnp.broadcast_to(x[:, LANES - 1:LANES], x.shape)
    exc = jnp.where(row >= 1, pltpu.roll(tot, 1, 0), 0.0)
    d = 1
    while d < nb:
        exc = exc + jnp.where(row >= d, pltpu.roll(exc, d, 0), 0.0)
        d *= 2
    o_ref[...] = x + exc


def _cumsum_t(logf_t):
    b, h, l = logf_t.shape
    nb = l // LANES
    x = logf_t.reshape(b, h * nb, LANES)
    spec = pl.BlockSpec((None, h * nb, LANES), lambda bi: (bi, 0, 0))
    out = pl.pallas_call(
        functools.partial(_cumsum_kernel, nb=nb),
        grid=(b,),
        in_specs=[spec],
        out_specs=spec,
        out_shape=jax.ShapeDtypeStruct(x.shape, F32),
        compiler_params=_cparams(1),
        name="cumsum",
    )(x)
    return out.reshape(b, h, l)


def _online_update(h, s, v, m_scr, l_scr, acc_scr):
    m_prev = m_scr[h]
    m_new = jnp.maximum(m_prev, jnp.max(s, axis=1, keepdims=True))
    p = jnp.exp(s - m_new)
    alpha = jnp.exp(m_prev - m_new)
    l_scr[h] = alpha * l_scr[h] + jnp.sum(p, axis=1, keepdims=True)
    acc_scr[h] = alpha * acc_scr[h] + _dot(p.astype(BF16), v)
    m_scr[h] = m_new


def _init_online(m_scr, l_scr, acc_scr):
    m_scr[...] = jnp.full(m_scr.shape, NEG, F32)
    l_scr[...] = jnp.zeros(l_scr.shape, F32)
    acc_scr[...] = jnp.zeros(acc_scr.shape, F32)


def _attn_scratch(n_state, tq):
    return [pltpu.VMEM((n_state, tq, 1), F32), pltpu.VMEM((n_state, tq, 1), F32),
            pltpu.VMEM((n_state, tq, HEAD_DIM), F32)]


def _dsa_kernel(qi_ref, w_ref, q_ref, ki_ref, k_ref, v_ref, o_ref, key_scr, m_scr, l_scr, acc_scr,
                *, tq, tk, q_off, l_valid, ksel, pos_bits):
    i = pl.program_id(1)
    q0 = q_off + i * tq
    qchunk = (q0 + lax.broadcasted_iota(I32, (tq, 1), 0)) // CHUNK
    n_end = jnp.minimum(((q0 + tq - 1) // CHUNK + 1) * CHUNK, l_valid)
    nblk = (n_end + tk - 1) // tk
    kcol = lax.broadcasted_iota(I32, (1, tk), 1)

    qi = qi_ref[...]
    w = w_ref[...]

    def idx_body(j, carry):
        start = pl.multiple_of(j * tk, tk)
        ki = ki_ref[pl.ds(start, tk), 0:IDX_DIM]
        idx = jnp.zeros((tq, tk), F32)
        for g in range(IDX_HEADS):
            s = _dot_nt(qi[:, g * IDX_DIM:(g + 1) * IDX_DIM], ki)
            idx = idx + jnp.maximum(s, 0.0) * w[:, MISC_AW + g:MISC_AW + g + 1]
        bits = lax.bitcast_convert_type(idx, I32)
        key = jnp.where(bits < 0, bits ^ INT_MAX, bits)
        key = jnp.where(idx == 0.0, 0, key)
        kpos = start + kcol
        kchunk = jnp.where(kpos < l_valid, kpos // CHUNK, INT_MAX)
        key_scr[:, pl.ds(start, tk)] = jnp.where(kchunk <= qchunk, key, INT_MIN)
        return carry

    lax.fori_loop(0, nblk, idx_body, 0)

    def count(indicator):
        def body(j, acc):
            start = pl.multiple_of(j * tk, tk)
            ind = indicator(key_scr[:, pl.ds(start, tk)], start + kcol)
            part = ind[:, 0:LANES]
            for c in range(1, tk // LANES):
                part = part + ind[:, c * LANES:(c + 1) * LANES]
            return acc + part
        acc = lax.fori_loop(0, nblk, body, jnp.zeros((tq, LANES), F32))
        return jnp.sum(acc, axis=1, keepdims=True)

    def bit_body(t, ans):
        cand_u = ans | lax.shift_left(jnp.int32(1), 31 - t)
        cand_s = cand_u ^ INT_MIN
        cnt = count(lambda kt, kp: jnp.where(kt >= cand_s, 1.0, 0.0))
        return jnp.where(cnt >= ksel, cand_u, ans)

    ans = lax.fori_loop(0, 32, bit_body, jnp.zeros((tq, 1), I32))
    tau = jnp.maximum(ans ^ INT_MIN, INT_MIN + 1)
    c_ge = count(lambda kt, kp: jnp.where(kt >= tau, 1.0, 0.0))
    c_gt = count(lambda kt, kp: jnp.where(kt > tau, 1.0, 0.0))
    need = ksel - c_gt
    any_tie = jnp.max(jnp.where(c_ge > ksel, 1.0, 0.0)) > 0.0

    def pos_body(t, p):
        cand = p | lax.shift_left(jnp.int32(1), pos_bits - 1 - t)
        cnt = count(lambda kt, kp: jnp.where(kt == tau, jnp.where(kp < cand, 1.0, 0.0), 0.0))
        return jnp.where(cnt < need, cand, p)

    p_lim = lax.fori_loop(0, jnp.where(any_tie, pos_bits, 0), pos_body, jnp.zeros((tq, 1), I32))
    p_lim = jnp.where(any_tie, p_lim, INT_MAX)

    def bias_body(j, carry):
        start = pl.multiple_of(j * tk, tk)
        kt = key_scr[:, pl.ds(start, tk)]
        kpos = start + kcol
        bias = jnp.where(kt > tau, 0.0, jnp.where(kt == tau, jnp.where(kpos <= p_lim, 0.0, NEG), NEG))
        key_scr[:, pl.ds(start, tk)] = lax.bitcast_convert_type(bias.astype(F32), I32)
        return carry

    lax.fori_loop(0, nblk, bias_body, 0)

    _init_online(m_scr, l_scr, acc_scr)

    def att_body(j, carry):
        start = pl.multiple_of(j * tk, tk)
        bias = lax.bitcast_convert_type(key_scr[:, pl.ds(start, tk)], F32)
        for h in range(HEADS):
            hs = slice(h * HEAD_DIM, (h + 1) * HEAD_DIM)
            s = _dot_nt(q_ref[:, hs], k_ref[pl.ds(start, tk), hs]) + bias
            _online_update(h, s, v_ref[pl.ds(start, tk), hs], m_scr, l_scr, acc_scr)
        return carry

    lax.fori_loop(0, nblk, att_body, 0)
    o_ref[...] = jnp.concatenate([acc_scr[h] / l_scr[h] for h in range(HEADS)], axis=1).astype(BF16)


def _dsa(qi, misc, q, ki, k, v, *, q_off, l_valid, tq):
    b, t, _ = q.shape
    lp = k.shape[1]
    tk = 512
    assert lp % tk == 0 and t % tq == 0
    ksel = min(TOPK_MAX, l_valid // 4)
    qtok = lambda width: pl.BlockSpec((None, tq, width), lambda bi, i: (bi, i, 0))
    res = lambda width: pl.BlockSpec((None, lp, width), lambda bi, i: (bi, 0, 0), pipeline_mode=pl.Buffered(1))
    kern = functools.partial(_dsa_kernel, tq=tq, tk=tk, q_off=q_off, l_valid=l_valid, ksel=float(ksel),
                             pos_bits=int(lp - 1).bit_length())
    return pl.pallas_call(
        kern,
        grid=(b, t // tq),
        in_specs=[qtok(IDX_HEADS * IDX_DIM), qtok(MISC_W), qtok(GROUP_W), res(MISC_W), res(GROUP_W), res(GROUP_W)],
        out_specs=qtok(GROUP_W),
        out_shape=jax.ShapeDtypeStruct((b, t, GROUP_W), BF16),
        scratch_shapes=[pltpu.VMEM((tq, lp), I32)] + _attn_scratch(HEADS, tq),
        compiler_params=_cparams(2),
        name="dsa",
    )(qi, misc, q, ki, k, v)


def _fox_kernel(q_ref, cq_ref, k_ref, v_ref, ck_ref, o_ref, m_scr, l_scr, acc_scr, *, tq, tk, q_off, l_valid):
    i = pl.program_id(1)
    q0 = q_off + i * tq
    qpos = q0 + lax.broadcasted_iota(I32, (tq, 1), 0)
    nblk = (jnp.minimum(q0 + tq, l_valid) + tk - 1) // tk
    nfull = jnp.minimum((q0 + 1) // tk, l_valid // tk)
    kcol = lax.broadcasted_iota(I32, (1, tk), 1)
    cq = cq_ref[...]
    _init_online(m_scr, l_scr, acc_scr)

    def body(j, carry, masked):
        start = pl.multiple_of(j * tk, tk)
        if masked:
            kpos = start + kcol
            valid = kpos <= qpos
        for h in range(HEADS):
            hs = slice(h * HEAD_DIM, (h + 1) * HEAD_DIM)
            s = _dot_nt(q_ref[:, hs], k_ref[pl.ds(start, tk), hs])
            s = s + cq[:, h:h + 1] - ck_ref[h:h + 1, pl.ds(start, tk)]
            if masked:
                s = jnp.where(valid, s, NEG)
            _online_update(h, s, v_ref[pl.ds(start, tk), hs], m_scr, l_scr, acc_scr)
        return carry

    lax.fori_loop(0, nfull, functools.partial(body, masked=False), 0)
    lax.fori_loop(nfull, nblk, functools.partial(body, masked=True), 0)
    o_ref[...] = jnp.concatenate([acc_scr[h] / l_scr[h] for h in range(HEADS)], axis=1).astype(BF16)


def _fox(q, cq, k, v, ck_t, *, q_off, l_valid, tq):
    b, t, _ = q.shape
    lp = k.shape[1]
    tk = 512
    assert lp % tk == 0 and t % tq == 0
    qtok = lambda width: pl.BlockSpec((None, tq, width), lambda bi, i: (bi, i, 0))
    res = lambda rows, width: pl.BlockSpec((None, rows, width), lambda bi, i: (bi, 0, 0),
                                           pipeline_mode=pl.Buffered(1))
    kern = functools.partial(_fox_kernel, tq=tq, tk=tk, q_off=q_off, l_valid=l_valid)
    return pl.pallas_call(
        kern,
        grid=(b, t // tq),
        in_specs=[qtok(GROUP_W), qtok(HEADS), res(lp, GROUP_W), res(lp, GROUP_W), res(HEADS, lp)],
        out_specs=qtok(GROUP_W),
        out_shape=jax.ShapeDtypeStruct((b, t, GROUP_W), BF16),
        scratch_shapes=_attn_scratch(HEADS, tq),
        compiler_params=_cparams(2),
        name="fox",
    )(q, cq, k, v, ck_t)


def _diff_kernel(q_ref, k_ref, v_ref, lam_ref, gsub_ref, o_ref, m_scr, l_scr, acc_scr,
                 *, tq, tk, q_off, l_valid, out_scale):
    i = pl.program_id(1)
    q0 = q_off + i * tq
    qchunk = (q0 + lax.broadcasted_iota(I32, (tq, 1), 0)) // CHUNK
    n_end = jnp.minimum(((q0 + tq - 1) // CHUNK + 1) * CHUNK, l_valid)
    nblk = (n_end + tk - 1) // tk
    nfull = jnp.minimum(((q0 // CHUNK + 1) * CHUNK) // tk, l_valid // tk)
    kcol = lax.broadcasted_iota(I32, (1, tk), 1)
    _init_online(m_scr, l_scr, acc_scr)

    def body(j, carry, masked):
        start = pl.multiple_of(j * tk, tk)
        if masked:
            kpos = start + kcol
            valid = jnp.where(kpos < l_valid, kpos // CHUNK, INT_MAX) <= qchunk
        for h in range(HEADS):
            v = v_ref[pl.ds(start, tk), h * HEAD_DIM:(h + 1) * HEAD_DIM]
            for half in range(2):
                ds_ = slice(h * HEAD_DIM + half * DIFF_DIM, h * HEAD_DIM + (half + 1) * DIFF_DIM)
                s = _dot_nt(q_ref[:, ds_], k_ref[pl.ds(start, tk), ds_])
                if masked:
                    s = jnp.where(valid, s, NEG)
                _online_update(2 * h + half, s, v, m_scr, l_scr, acc_scr)
        return carry

    lax.fori_loop(0, nfull, functools.partial(body, masked=False), 0)
    lax.fori_loop(nfull, nblk, functools.partial(body, masked=True), 0)

    lv = lam_ref[...]
    lam = (jnp.exp(jnp.sum(lv[0:1] * lv[1:2], axis=1, keepdims=True))
           - jnp.exp(jnp.sum(lv[2:3] * lv[3:4], axis=1, keepdims=True)) + (1.0 - out_scale))
    outs = []
    for h in range(HEADS):
        o = acc_scr[2 * h] / l_scr[2 * h] - lam * (acc_scr[2 * h + 1] / l_scr[2 * h + 1])
        outs.append(_rms(o, gsub_ref[...]) * out_scale)
    o_ref[...] = jnp.concatenate(outs, axis=1).astype(BF16)


def _diff(q, k, v, lam_vecs, g_sub, *, q_off, l_valid, tq, lam_init):
    b, t, _ = q.shape
    lp = k.shape[1]
    tk = 512
    assert lp % tk == 0 and t % tq == 0
    qtok = lambda width: pl.BlockSpec((None, tq, width), lambda bi, i: (bi, i, 0))
    res = lambda width: pl.BlockSpec((None, lp, width), lambda bi, i: (bi, 0, 0), pipeline_mode=pl.Buffered(1))
    const = lambda shape: pl.BlockSpec(shape, lambda bi, i: (0, 0))
    kern = functools.partial(_diff_kernel, tq=tq, tk=tk, q_off=q_off, l_valid=l_valid, out_scale=1.0 - lam_init)
    return pl.pallas_call(
        kern,
        grid=(b, t // tq),
        in_specs=[qtok(GROUP_W), res(GROUP_W), res(GROUP_W), const((4, DIFF_DIM)), const((1, HEAD_DIM))],
        out_specs=qtok(GROUP_W),
        out_shape=jax.ShapeDtypeStruct((b, t, GROUP_W), BF16),
        scratch_shapes=_attn_scratch(2 * HEADS, tq),
        compiler_params=_cparams(2),
        name="diff",
    )(q, k, v, lam_vecs, g_sub.reshape(1, HEAD_DIM))


def _relbias_kernel(rb_ref, o_ref):
    qi = lax.broadcasted_iota(I32, (CHUNK, BAND_KEYS), 0)
    kj = lax.broadcasted_iota(I32, (CHUNK, BAND_KEYS), 1)
    idx = jnp.clip(BAND_PREV + qi - kj, -REL_CLIP, REL_CLIP) + REL_CLIP

    def body(r, accs):
        return tuple(jnp.where(idx == r, rb_ref[h, r], accs[h]) for h in range(HEADS))

    accs = lax.fori_loop(0, 2 * REL_CLIP + 1, body, tuple(jnp.zeros((CHUNK, BAND_KEYS), F32) for _ in range(HEADS)))
    for h in range(HEADS):
        o_ref[h] = accs[h]


def _relbias(rel_bias_l):
    return pl.pallas_call(
        _relbias_kernel,
        in_specs=[pl.BlockSpec(memory_space=pltpu.SMEM)],
        out_specs=pl.BlockSpec(memory_space=pltpu.VMEM),
        out_shape=jax.ShapeDtypeStruct((HEADS, CHUNK, BAND_KEYS), F32),
        name="relbias",
    )(rel_bias_l.astype(F32))


def _band_kernel(q_ref, kp_ref, kc_ref, vp_ref, vc_ref, bias_ref, o_ref, k_scr, v_scr, *, tq, first_has_no_prev):
    i = pl.program_id(1)
    k_scr[0:BAND_PREV, :] = kp_ref[...]
    k_scr[BAND_PREV:BAND_PREV + tq, :] = kc_ref[...]
    v_scr[0:BAND_PREV, :] = vp_ref[...]
    v_scr[BAND_PREV:BAND_PREV + tq, :] = vc_ref[...]
    kcol = lax.broadcasted_iota(I32, (1, BAND_KEYS), 1)
    first = jnp.where(i == 0, BAND_PREV, 0) if first_has_no_prev else 0
    for c in range(tq // CHUNK):
        r0 = c * CHUNK
        valid = (r0 + kcol) >= first
        outs = []
        for h in range(HEADS):
            hs = slice(h * HEAD_DIM, (h + 1) * HEAD_DIM)
            s = _dot_nt(q_ref[r0:r0 + CHUNK, hs], k_scr[r0:r0 + BAND_KEYS, hs]) + bias_ref[h]
            s = jnp.where(valid, s, NEG)
            e = jnp.exp(s - jnp.max(s, axis=1, keepdims=True))
            o = _dot(e.astype(BF16), v_scr[r0:r0 + BAND_KEYS, hs])
            outs.append(o / jnp.sum(e, axis=1, keepdims=True))
        o_ref[r0:r0 + CHUNK, :] = jnp.concatenate(outs, axis=1).astype(BF16)


def _band(q, k_prev, k_cur, v_prev, v_cur, bias, *, tq, prev_from_cur):
    b, t, _ = q.shape
    assert t % tq == 0 and (not prev_from_cur or tq == BAND_PREV)
    qtok = pl.BlockSpec((None, tq, GROUP_W), lambda bi, i: (bi, i, 0))
    if prev_from_cur:
        prev = pl.BlockSpec((None, BAND_PREV, GROUP_W), lambda bi, i: (bi, jnp.maximum(i - 1, 0), 0))
    else:
        prev = pl.BlockSpec((None, BAND_PREV, GROUP_W), lambda bi, i: (bi, 0, 0))
    bspec = pl.BlockSpec((HEADS, CHUNK, BAND_KEYS), lambda bi, i: (0, 0, 0))
    kern = functools.partial(_band_kernel, tq=tq, first_has_no_prev=prev_from_cur)
    return pl.pallas_call(
        kern,
        grid=(b, t // tq),
        in_specs=[qtok, prev, qtok, prev, qtok, bspec],
        out_specs=qtok,
        out_shape=jax.ShapeDtypeStruct((b, t, GROUP_W), BF16),
        scratch_shapes=[pltpu.VMEM((BAND_PREV + tq, GROUP_W), BF16), pltpu.VMEM((BAND_PREV + tq, GROUP_W), BF16)],
        compiler_params=_cparams(2),
        name="band",
    )(q, k_prev, k_cur, v_prev, v_cur, bias)


def _memkv_kernel(m_ref, g_ref, wk_ref, wv_ref, kf_o, vf_o, kb_o, vb_o):
    hb = _rms(m_ref[...], g_ref[...]).astype(BF16)
    k = _dot(hb, wk_ref[...])
    v = _dot(hb, wv_ref[...])
    kf_o[...] = k
    vf_o[...] = v
    kb_o[...] = k.astype(BF16)
    vb_o[...] = v.astype(BF16)


def _memkv(mem, g, wk, wv):
    b, m, d = mem.shape
    xw = wk.shape[1]
    tok = lambda width: pl.BlockSpec((None, m, width), lambda bi: (bi, 0, 0))
    const = lambda shape: pl.BlockSpec(shape, lambda bi: (0, 0))
    return pl.pallas_call(
        _memkv_kernel,
        grid=(b,),
        in_specs=[tok(d), const((1, d)), const((d, xw)), const((d, xw))],
        out_specs=[tok(xw)] * 4,
        out_shape=[jax.ShapeDtypeStruct((b, m, xw), F32)] * 2 + [jax.ShapeDtypeStruct((b, m, xw), BF16)] * 2,
        compiler_params=_cparams(1),
        name="memkv",
    )(mem, g.reshape(1, d), wk, wv)


def _mix_cross_kernel(x_ref, oa_ref, ob_ref, oc_ref, od_ref, wout_ref, g_ref, wq_ref, mk_ref, mv_ref, wo_ref, o_ref):
    x = x_ref[...]
    for n, oref in enumerate((oa_ref, ob_ref, oc_ref, od_ref)):
        x = x + _dot(oref[...], wout_ref[n * GROUP_W:(n + 1) * GROUP_W, :])
    hb = _rms(x, g_ref[...]).astype(BF16)
    q = _dot(hb, wq_ref[...])
    outs = []
    for h in range(X_HEADS):
        hs = slice(h * X_HEAD_DIM, (h + 1) * X_HEAD_DIM)
        s = _dot_nt(q[:, hs].astype(BF16), mk_ref[:, hs]) * (X_HEAD_DIM ** -0.5)
        e = jnp.exp(s - jnp.max(s, axis=1, keepdims=True))
        p = e / jnp.sum(e, axis=1, keepdims=True)
        outs.append(_dot(p.astype(BF16), mv_ref[:, hs]))
    o = jnp.concatenate(outs, axis=1).astype(BF16)
    o_ref[...] = x + _dot(o, wo_ref[...])


def _mix_cross(x, oa, ob, oc, od, w_out, g, wq, mk, mv, wo):
    b, t, d = x.shape
    tm = min(512, t)
    m, xw = mk.shape[1], mk.shape[2]
    tok = lambda width: pl.BlockSpec((None, tm, width), lambda bi, i: (bi, i, 0))
    const = lambda shape: pl.BlockSpec(shape, lambda bi, i: (0, 0))
    mem = pl.BlockSpec((None, m, xw), lambda bi, i: (bi, 0, 0))
    return pl.pallas_call(
        _mix_cross_kernel,
        grid=(b, t // tm),
        in_specs=[tok(d)] + [tok(GROUP_W)] * 4 + [const(w_out.shape), const((1, d)), const(wq.shape), mem, mem,
                                                   const(wo.shape)],
        out_specs=tok(d),
        out_shape=jax.ShapeDtypeStruct((b, t, d), F32),
        compiler_params=_cparams(2),
        name="mix_cross",
    )(x, oa, ob, oc, od, w_out, g.reshape(1, d), wq, mk, mv, wo)


def _ffn_kernel(x_ref, g_ref, prev_ref, wup_ref, cw_ref, cb_ref, wdn_ref, o_ref, conv_ref, halo_scr, acc_scr,
                *, tm, d_ff, fc):
    i = pl.program_id(1)

    @pl.when(i == 0)
    def _():
        halo_scr[...] = prev_ref[...]

    x = x_ref[...]
    hb = _rms(x, g_ref[...]).astype(BF16)
    row = lax.broadcasted_iota(I32, (tm, fc), 0)
    for c in range(d_ff // fc):
        cs = slice(c * fc, (c + 1) * fc)
        g = _dot(hb, wup_ref[:, cs])
        u = _dot(hb, wup_ref[:, d_ff + c * fc:d_ff + (c + 1) * fc])
        h0 = halo_scr[0:1, cs]
        h1 = halo_scr[1:2, cs]
        gm1 = jnp.where(row == 0, h1, pltpu.roll(g, 1, 0))
        gm2 = jnp.where(row == 0, h0, jnp.where(row == 1, h1, pltpu.roll(g, 2, 0)))
        a = cb_ref[:, cs] + gm2 * cw_ref[0:1, cs]
        a = a + gm1 * cw_ref[1:2, cs]
        a = a + g * cw_ref[2:3, cs]
        y = (a * (1.0 / (1.0 + jnp.exp(-a)))) * u
        f = _dot(y.astype(BF16), wdn_ref[cs, :])
        if c == 0:
            acc_scr[...] = f
        else:
            acc_scr[...] += f
        tail = g[tm - (CONV_W - 1):tm, :]
        halo_scr[:, cs] = tail
        conv_ref[:, cs] = tail
    o_ref[...] = x + acc_scr[...]


def _ffn(x, g, prev, w_up, conv_w, conv_b, w_down):
    b, t, d = x.shape
    d_ff = w_down.shape[0]
    tm = min(512, t)
    fc = 256
    assert d_ff % fc == 0 and t % tm == 0 and tm >= CONV_W - 1
    tok = pl.BlockSpec((None, tm, d), lambda bi, i: (bi, i, 0))
    const = lambda shape: pl.BlockSpec(shape, lambda bi, i: (0, 0))
    st = pl.BlockSpec((None, CONV_W - 1, d_ff), lambda bi, i: (bi, 0, 0))
    kern = functools.partial(_ffn_kernel, tm=tm, d_ff=d_ff, fc=fc)
    return pl.pallas_call(
        kern,
        grid=(b, t // tm),
        in_specs=[tok, const((1, d)), st, const(w_up.shape), const(conv_w.shape), const((1, d_ff)),
                  const(w_down.shape)],
        out_specs=[tok, st],
        out_shape=[jax.ShapeDtypeStruct((b, t, d), F32), jax.ShapeDtypeStruct((b, CONV_W - 1, d_ff), F32)],
        scratch_shapes=[pltpu.VMEM((CONV_W - 1, d_ff), F32), pltpu.VMEM((tm, d), F32)],
        compiler_params=_cparams(2),
        name="ffn",
    )(x, g.reshape(1, d), prev, w_up, conv_w, conv_b.reshape(1, d_ff), w_down)


def _rmsnorm_kernel(x_ref, g_ref, o_ref):
    o_ref[...] = _rms(x_ref[...], g_ref[...])


def _rmsnorm(x, g):
    b, t, d = x.shape
    tm = min(1024, t)
    tok = pl.BlockSpec((None, tm, d), lambda bi, i: (bi, i, 0))
    return pl.pallas_call(
        _rmsnorm_kernel,
        grid=(b, t // tm),
        in_specs=[tok, pl.BlockSpec((1, d), lambda bi, i: (0, 0))],
        out_specs=tok,
        out_shape=jax.ShapeDtypeStruct((b, t, d), F32),
        compiler_params=_cparams(2),
        name="final_norm",
    )(x, g.reshape(1, d))


def _pad_rows(a, rows):
    return a if a.shape[1] == rows else jnp.pad(a, ((0, 0), (0, rows - a.shape[1]), (0, 0)))


def _round_up(n, m):
    return (n + m - 1) // m * m


def _mixers(pr, keys, *, q_off, l_valid, tq_a, tq_bd, band_args, band_tq, prev_from_cur, lam_vecs, g_sub,
            lam_init, logf_all):
    lp = keys["ak"].shape[1]
    oa = _dsa(pr["aqi"], pr["misc"], pr["aq"], keys["aki"], keys["ak"], keys["av"],
              q_off=q_off, l_valid=l_valid, tq=tq_a)
    lc = _round_up(logf_all.shape[1], 1024)
    c_t = _cumsum_t(jnp.swapaxes(_pad_rows(logf_all, lc), 1, 2))
    t = pr["bq"].shape[1]
    cq = jnp.swapaxes(c_t[:, :, q_off:q_off + t], 1, 2)
    ck_t = c_t[:, :, :lp] if lc >= lp else jnp.pad(c_t, ((0, 0), (0, 0), (0, lp - lc)))
    ob = _fox(pr["bq"], cq, keys["bk"], keys["bv"], ck_t, q_off=q_off, l_valid=l_valid, tq=tq_bd)
    oc = _band(pr["cq"], *band_args, tq=band_tq, prev_from_cur=prev_from_cur)
    od = _diff(pr["dq"], keys["dk"], keys["dv"], lam_vecs, g_sub, q_off=q_off, l_valid=l_valid, tq=tq_bd,
               lam_init=lam_init)
    return oa, ob, oc, od


def kernel(x_prompt, x_sample, mem_prompt, cache_a_k, cache_a_v, cache_a_kidx, cache_b_k, cache_b_v,
           cache_b_logf, cache_c_k, cache_c_v, cache_d_k, cache_d_v, cache_mem_k, cache_mem_v,
           state_ffn_conv, g_mix, w_in, b_forget, rel_bias, lam_q1, lam_k1, lam_q2, lam_k2, g_sub, w_out,
           g_cross, g_mem, w_xq, w_xk, w_xv, w_xo, g_ffn, w_up, conv_w, conv_b, w_down, g_final):
    depth = w_in.shape[0]
    bp, s_len, d_model = x_prompt.shape
    bs, t_len, _ = x_sample.shape
    past = cache_a_k.shape[2]
    d_ff = w_down.shape[1]
    c_keep = min(BAND_PREV, s_len)
    l_s = past + t_len
    lp_s = _round_up(l_s, 512)

    pos_p = jnp.arange(s_len, dtype=jnp.int32)
    pos_s = jnp.tile(past + jnp.arange(t_len, dtype=jnp.int32), bs)
    h64 = HEAD_DIM // ROT_FRACTION // 2
    h32 = DIFF_DIM // ROT_FRACTION // 2
    t64_p, t32_p = _rope_table(pos_p, h64, HEAD_DIM), _rope_table(pos_p, h32, DIFF_DIM)
    t64_s, t32_s = _rope_table(pos_s, h64, HEAD_DIM), _rope_table(pos_s, h32, DIFF_DIM)

    flat = lambda a: a.reshape(a.shape[0], a.shape[1], -1)
    heads = lambda a, hd=HEAD_DIM: a.reshape(a.shape[0], a.shape[1], a.shape[2] // hd, hd)

    xp, xs = x_prompt, x_sample
    p_states, s_states = [], []
    for l in range(depth):
        lam_init = 0.8 - 0.6 * math.exp(-0.3 * l)
        w_pack, bfp = _pack_w_in(w_in[l], b_forget[l])
        lam_vecs = jnp.stack([lam_q1[l], lam_k1[l], lam_q2[l], lam_k2[l]]).astype(F32)
        bias = _relbias(rel_bias[l])
        w_out_b, wq_b, wo_b = w_out[l].astype(BF16), w_xq[l].astype(BF16), w_xo[l].astype(BF16)
        w_up_b, w_dn_b = w_up[l].astype(BF16), w_down[l].astype(BF16)

        pr = _proj(xp, g_mix[l], w_pack, bfp, t64_p, t32_p, c_keep)
        logf = pr["misc"][:, :, MISC_BF:MISC_BF + HEADS]
        oa, ob, oc, od = _mixers(
            pr, pr, q_off=0, l_valid=s_len, tq_a=128, tq_bd=256,
            band_args=(pr["ck"], pr["ck"], pr["cv"], pr["cv"], bias), band_tq=BAND_PREV, prev_from_cur=True,
            lam_vecs=lam_vecs, g_sub=g_sub[l], lam_init=lam_init, logf_all=logf)
        mkf, mvf, mkb, mvb = _memkv(mem_prompt, g_mem[l], w_xk[l].astype(BF16), w_xv[l].astype(BF16))
        xp = _mix_cross(xp, oa, ob, oc, od, w_out_b, g_cross[l], wq_b, mkb, mvb, wo_b)
        xp, conv_p = _ffn(xp, g_ffn[l], jnp.zeros((bp, CONV_W - 1, d_ff), F32), w_up_b, conv_w[l], conv_b[l], w_dn_b)
        p_states.append((heads(pr["akf"]), heads(pr["avf"]), pr["misc"][:, :, :IDX_DIM],
                         heads(pr["bkf"]), heads(pr["bvf"]), logf,
                         heads(pr["ckf"]), heads(pr["cvf"]), heads(pr["dkf"]), heads(pr["dvf"]),
                         conv_p, heads(mkf, X_HEAD_DIM), heads(mvf, X_HEAD_DIM)))

        prs = _proj(xs.reshape(1, bs * t_len, d_model), g_mix[l], w_pack, bfp, t64_s, t32_s, bs * t_len)
        prs = {n: a.reshape(bs, t_len, a.shape[-1]) for n, a in prs.items()}
        cat = lambda cache, new: _pad_rows(jnp.concatenate([flat(cache).astype(BF16), new], axis=1), lp_s)
        kidx_new = prs["aki"]
        kidx_cache = jnp.pad(cache_a_kidx[l].astype(BF16), ((0, 0), (0, 0), (0, MISC_W - IDX_DIM)))
        keys = {"aki": _pad_rows(jnp.concatenate([kidx_cache, kidx_new], axis=1), lp_s),
                "ak": cat(cache_a_k[l], prs["ak"]), "av": cat(cache_a_v[l], prs["av"]),
                "bk": cat(cache_b_k[l], prs["bk"]), "bv": cat(cache_b_v[l], prs["bv"]),
                "dk": cat(cache_d_k[l], prs["dk"]), "dv": cat(cache_d_v[l], prs["dv"])}
        logf_s = prs["misc"][:, :, MISC_BF:MISC_BF + HEADS]
        logf_all = jnp.concatenate([cache_b_logf[l].astype(F32), logf_s], axis=1)
        oa, ob, oc, od = _mixers(
            prs, keys, q_off=past, l_valid=l_s, tq_a=t_len, tq_bd=t_len,
            band_args=(flat(cache_c_k[l]).astype(BF16), prs["ck"], flat(cache_c_v[l]).astype(BF16), prs["cv"], bias),
            band_tq=t_len, prev_from_cur=False,
            lam_vecs=lam_vecs, g_sub=g_sub[l], lam_init=lam_init, logf_all=logf_all)
        xs = _mix_cross(xs, oa, ob, oc, od, w_out_b, g_cross[l], wq_b,
                        flat(cache_mem_k[l]).astype(BF16), flat(cache_mem_v[l]).astype(BF16), wo_b)
        xs, conv_s = _ffn(xs, g_ffn[l], state_ffn_conv[l], w_up_b, conv_w[l], conv_b[l], w_dn_b)
        s_states.append((heads(prs["akf"]), heads(prs["avf"]), prs["misc"][:, :, :IDX_DIM],
                         heads(prs["bkf"]), heads(prs["bvf"]), logf_s,
                         heads(prs["ckf"]), heads(prs["cvf"]), heads(prs["dkf"]), heads(prs["dvf"]), conv_s))

    y_prompt = _rmsnorm(xp, g_final)
    y_sample = _rmsnorm(xs, g_final)
    p_out = [jnp.stack(z, axis=0) for z in zip(*p_states)]
    s_out = [jnp.stack(z, axis=0) for z in zip(*s_states)]
    return (y_prompt, y_sample, *p_out, *s_out)
```

```python
import functools
import math

import numpy as np
import jax
import jax.numpy as jnp
from jax import lax
from jax.experimental import pallas as pl
from jax.experimental.pallas import tpu as pltpu

F32 = jnp.float32
BF16 = jnp.bfloat16
I32 = jnp.int32

CHUNK = 64
HEAD_DIM = 64
HEADS = 4
GROUP_W = HEADS * HEAD_DIM
ROT_FRACTION = 4
ROPE_THETA = 500000.0
IDX_HEADS = 8
IDX_DIM = 64
TOPK_MAX = 256
BAND_CHUNKS = 8
BAND_KEYS = (BAND_CHUNKS + 1) * CHUNK
BAND_PREV = BAND_CHUNKS * CHUNK
REL_CLIP = 128
DIFF_DIM = HEAD_DIM // 2
X_HEADS = 4
X_HEAD_DIM = 128
CONV_W = 3
EPS = 1e-6

LANES = 128
VMEM_LIMIT_BYTES = 56 * 1024 * 1024

NEG = -1e30
INT_MIN = -2147483648
INT_MAX = 2147483647

MISC_W = LANES
MISC_AW = IDX_DIM
MISC_BF = IDX_DIM + IDX_HEADS
PROJ_COLS = 3 * GROUP_W + IDX_HEADS * IDX_DIM + MISC_W + 9 * GROUP_W


def _cparams(n_axes):
    return pltpu.CompilerParams(dimension_semantics=("arbitrary",) * n_axes,
                                vmem_limit_bytes=VMEM_LIMIT_BYTES)


def _rms(x, g):
    return (x * lax.rsqrt(jnp.mean(x * x, axis=-1, keepdims=True) + EPS)) * g


def _dot(a, b):
    return jnp.dot(a, b, preferred_element_type=F32)


def _dot_nt(a, b):
    return lax.dot_general(a, b, (((1,), (1,)), ((), ())), preferred_element_type=F32)


def _rope(z, tab, half):
    w = z.shape[1]
    reps = w // LANES
    rep = lambda t: t if reps == 1 else jnp.concatenate([t] * reps, axis=1)
    c = rep(tab[:, 0:LANES])
    s1 = rep(tab[:, LANES:2 * LANES])
    s2 = rep(tab[:, 2 * LANES:3 * LANES])
    return z * c + pltpu.roll(z, w - half, 1) * s1 + pltpu.roll(z, half, 1) * s2


def _slab_masked(z, widths):
    lane = lax.broadcasted_iota(I32, (z.shape[0], LANES), 1)
    outs = []
    for j in range(GROUP_W // widths):
        lo = (j * widths) % LANES
        zs = z[:, (j * widths) // LANES * LANES:((j * widths) // LANES + 1) * LANES]
        outs.append(jnp.where(lane < lo, 0.0, jnp.where(lane < lo + widths, zs, 0.0)))
    return jnp.concatenate(outs, axis=1)


def _value_slabs(z):
    one = jnp.where(lax.broadcasted_iota(I32, (z.shape[0], HEAD_DIM), 1) == 0, 1.0, 0.0)
    return jnp.concatenate([t for h in range(HEADS) for t in (z[:, h * HEAD_DIM:(h + 1) * HEAD_DIM], one)], axis=1)


def _proj_kernel(x_ref, g_ref, w_ref, bf_ref, t64_ref, t32_ref,
                 aq_o, ak_o, av_o, aqi_o, aki_o, bq_o, bk_o, bv_o, cq_o, ck_o, cv_o, dq_o, dk_o, dv_o,
                 akf_o, avf_o, misc_o, bkf_o, bvf_o, ckf_o, cvf_o, dkf_o, dvf_o):
    hb = _rms(x_ref[...], g_ref[...]).astype(BF16)
    t64 = t64_ref[...]
    t32 = t32_ref[...]
    h64 = HEAD_DIM // ROT_FRACTION // 2
    h32 = DIFF_DIM // ROT_FRACTION // 2
    qscale = HEAD_DIM ** -0.5
    col = [0]

    def mm(width):
        c0 = col[0]
        col[0] = c0 + width
        return _dot(hb, w_ref[:, c0:c0 + width])

    z = _rope(mm(GROUP_W), t64, h64)
    aq_o[...] = _slab_masked(z * qscale, HEAD_DIM).astype(BF16)
    z = _rope(mm(GROUP_W), t64, h64)
    akf_o[...] = z
    ak_o[...] = z.T.astype(BF16)
    z = mm(GROUP_W)
    avf_o[...] = z
    av_o[...] = _value_slabs(z).astype(BF16)
    z = _rope(mm(IDX_HEADS * IDX_DIM), t64, h64)
    aqi_o[...] = (z * (IDX_DIM ** -0.5)).astype(BF16)
    z = mm(MISC_W)
    r = _rope(z, t64, h64)
    lane = lax.broadcasted_iota(I32, z.shape, 1)
    zf = z + bf_ref[...]
    logsig = jnp.minimum(zf, 0.0) - jnp.log1p(jnp.exp(-jnp.abs(zf)))
    misc_o[...] = jnp.where(lane < MISC_AW, r, jnp.where(lane < MISC_BF, z * (IDX_HEADS ** -0.5), logsig))
    aki_o[...] = jnp.where(lane < MISC_AW, r, 0.0).T.astype(BF16)
    z = mm(GROUP_W)
    bq_o[...] = _slab_masked(z * qscale, HEAD_DIM).astype(BF16)
    z = mm(GROUP_W)
    bkf_o[...] = z
    bk_o[...] = z.T.astype(BF16)
    z = mm(GROUP_W)
    bvf_o[...] = z
    bv_o[...] = _value_slabs(z).astype(BF16)
    z = mm(GROUP_W)
    cq_o[...] = (z * qscale).astype(BF16)
    z = mm(GROUP_W)
    ckf_o[...] = z
    ck_o[...] = z.astype(BF16)
    z = mm(GROUP_W)
    cvf_o[...] = z
    cv_o[...] = z.astype(BF16)
    z = _rope(mm(GROUP_W), t32, h32)
    dq_o[...] = _slab_masked(z * (DIFF_DIM ** -0.5), DIFF_DIM).astype(BF16)
    z = _rope(mm(GROUP_W), t32, h32)
    dkf_o[...] = z
    dk_o[...] = z.T.astype(BF16)
    z = mm(GROUP_W)
    dvf_o[...] = z
    dv_o[...] = _value_slabs(z).astype(BF16)


def _rope_table(pos, half, period):
    inv = ROPE_THETA ** (-jnp.arange(half, dtype=F32) / half)
    ang = pos.astype(F32)[:, None] * inv[None, :]
    cos, sin = jnp.cos(ang), jnp.sin(ang)
    t = pos.shape[0]
    rest = period - 2 * half
    zh = jnp.zeros((t, half), F32)
    zr = jnp.zeros((t, rest), F32)
    c = jnp.concatenate([cos, cos, jnp.ones((t, rest), F32)], axis=1)
    s1 = jnp.concatenate([-sin, zh, zr], axis=1)
    s2 = jnp.concatenate([zh, sin, zr], axis=1)
    rep = lambda a: jnp.tile(a, (1, LANES // period))
    return jnp.concatenate([rep(c), rep(s1), rep(s2)], axis=1)


def _pack_w_in(w_in_l, b_forget_l):
    sizes = (GROUP_W, GROUP_W, GROUP_W, IDX_HEADS * IDX_DIM, IDX_DIM, IDX_HEADS,
             GROUP_W, GROUP_W, GROUP_W, HEADS, GROUP_W, GROUP_W, GROUP_W, GROUP_W, GROUP_W, GROUP_W)
    pts = [int(p) for p in np.cumsum(sizes)[:-1]]
    (aq, ak, av, aqi, aki, aw, bq, bk, bv, bf, cq, ck, cv, dq, dk, dv) = jnp.split(w_in_l, pts, axis=1)
    d = w_in_l.shape[0]
    pad = jnp.zeros((d, MISC_W - IDX_DIM - IDX_HEADS - HEADS), w_in_l.dtype)
    misc = jnp.concatenate([aki, aw, bf, pad], axis=1)
    w = jnp.concatenate([aq, ak, av, aqi, misc, bq, bk, bv, cq, ck, cv, dq, dk, dv], axis=1).astype(BF16)
    bfp = jnp.zeros((1, MISC_W), F32).at[0, MISC_BF:MISC_BF + HEADS].set(b_forget_l.astype(F32))
    return w, bfp


def _proj(x, g, w, bfp, t64, t32, c_keep):
    b, t, d = x.shape
    tm = min(512, t)
    nt = t // tm
    nkeep = c_keep // tm
    tok = lambda width: pl.BlockSpec((None, tm, width), lambda bi, i: (bi, i, 0))
    ctok = pl.BlockSpec((None, tm, GROUP_W), lambda bi, i: (bi, jnp.maximum(i - (nt - nkeep), 0), 0))
    const = lambda shape: pl.BlockSpec(shape, lambda bi, i: (0, 0))
    tab = pl.BlockSpec((tm, 3 * LANES), lambda bi, i: (i, 0))
    sds = lambda width, dt, rows=t: jax.ShapeDtypeStruct((b, rows, width), dt)
    bf_names = ["aq", "ak", "av", "aqi", "aki", "bq", "bk", "bv", "cq", "ck", "cv", "dq", "dk", "dv"]
    qx_w, vx_w = HEADS * LANES, HEADS * LANES
    bf_w = [qx_w, GROUP_W, vx_w, IDX_HEADS * IDX_DIM, MISC_W, qx_w, GROUP_W, vx_w, GROUP_W, GROUP_W, GROUP_W,
            2 * qx_w, GROUP_W, vx_w]
    f_names = ["akf", "avf", "misc", "bkf", "bvf", "ckf", "cvf", "dkf", "dvf"]
    f_w = [GROUP_W, GROUP_W, MISC_W] + [GROUP_W] * 6
    out_shape = [sds(wd, BF16) for wd in bf_w]
    out_specs = [tok(wd) for wd in bf_w]
    for n, wd in zip(bf_names, bf_w):
        if n in ("ak", "aki", "bk", "dk"):
            out_shape[bf_names.index(n)] = jax.ShapeDtypeStruct((b, wd, t), BF16)
            out_specs[bf_names.index(n)] = pl.BlockSpec((None, wd, tm), lambda bi, i: (bi, 0, i))
    for n, wd in zip(f_names, f_w):
        if n in ("ckf", "cvf"):
            out_shape.append(sds(wd, F32, c_keep))
            out_specs.append(ctok)
        else:
            out_shape.append(sds(wd, F32))
            out_specs.append(tok(wd))
    outs = pl.pallas_call(
        _proj_kernel,
        grid=(b, nt),
        in_specs=[tok(d), const((1, d)), const((d, PROJ_COLS)), const((1, MISC_W)), tab, tab],
        out_specs=out_specs,
        out_shape=out_shape,
        compiler_params=_cparams(2),
        name="proj",
    )(x, g.reshape(1, d), w, bfp, t64, t32)
    return dict(zip(bf_names + f_names, outs))


def _cumsum_kernel(x_ref, o_ref, *, nb):
    x = x_ref[...]
    lane = lax.broadcasted_iota(I32, x.shape, 1)
    d = 1
    while d < LANES:
        x = x + jnp.where(lane >= d, pltpu.roll(x, d, 1), 0.0)
        d *= 2
    row = lax.broadcasted_iota(I32, x.shape, 0) % nb
    tot = jnp.broadcast_to(x[:, LANES - 1:LANES], x.shape)
    exc = jnp.where(row >= 1, pltpu.roll(tot, 1, 0), 0.0)
    d = 1
    while d < nb:
        exc = exc + jnp.where(row >= d, pltpu.roll(exc, d, 0), 0.0)
        d *= 2
    o_ref[...] = x + exc


def _cumsum_t(logf_t):
    b, h, l = logf_t.shape
    nb = l // LANES
    x = logf_t.reshape(b, h * nb, LANES)
    spec = pl.BlockSpec((None, h * nb, LANES), lambda bi: (bi, 0, 0))
    out = pl.pallas_call(
        functools.partial(_cumsum_kernel, nb=nb),
        grid=(b,),
        in_specs=[spec],
        out_specs=spec,
        out_shape=jax.ShapeDtypeStruct(x.shape, F32),
        compiler_params=_cparams(1),
        name="cumsum",
    )(x)
    return out.reshape(b, h, l)


def _lane_tile(a, width):
    return a if width == LANES else jnp.concatenate([a] * (width // LANES), axis=1)


def _online_update(j, s, vx, m_scr, acc_scr, shift=None):
    m_prev = m_scr[j]
    rmax = jnp.max(s, axis=1, keepdims=True)
    if shift is not None:
        rmax = rmax + shift
    m_new = jnp.maximum(m_prev, rmax)
    sub = m_new if shift is None else m_new - shift
    p = jnp.exp(s - _lane_tile(sub, s.shape[1]))
    acc_scr[j] = jnp.exp(m_prev - m_new) * acc_scr[j] + _dot(p.astype(BF16), vx)
    m_scr[j] = m_new


def _head_scores(q_ref, kt_ref, start, tk):
    return [_dot(q_ref[:, h * LANES:(h + 1) * LANES], kt_ref[h // 2 * LANES:(h // 2 + 1) * LANES, pl.ds(start, tk)])
            for h in range(HEADS)]


def _init_online(m_scr, acc_scr):
    m_scr[...] = jnp.full(m_scr.shape, NEG, F32)
    acc_scr[...] = jnp.zeros(acc_scr.shape, F32)


def _normalized(acc):
    return acc[:, 0:HEAD_DIM] / acc[:, HEAD_DIM:HEAD_DIM + 1]


def _attn_scratch(n_state, tq):
    return [pltpu.VMEM((n_state, tq, LANES), F32), pltpu.VMEM((n_state, tq, LANES), F32)]


def _dsa_kernel(qi_ref, w_ref, q_ref, ki_ref, k_ref, v_ref, o_ref, key_scr, m_scr, acc_scr,
                *, tq, tk, q_off, l_valid, ksel, pos_bits):
    i = pl.program_id(1)
    q0 = q_off + i * tq
    qchunk = (q0 + lax.broadcasted_iota(I32, (tq, 1), 0)) // CHUNK
    n_end = jnp.minimum(((q0 + tq - 1) // CHUNK + 1) * CHUNK, l_valid)
    nblk = (n_end + tk - 1) // tk
    kcol = lax.broadcasted_iota(I32, (1, tk), 1)

    qi = qi_ref[...]
    w = w_ref[...]
    qis = [qi[:, g * IDX_DIM:(g + 1) * IDX_DIM] for g in range(IDX_HEADS)]
    ws = [jnp.broadcast_to(w[:, MISC_AW + g:MISC_AW + g + 1], (tq, LANES)) for g in range(IDX_HEADS)]

    def idx_body(j, carry):
        start = pl.multiple_of(j * tk, tk)
        ki_t = ki_ref[0:IDX_DIM, pl.ds(start, tk)]
        idx = jnp.zeros((tq, tk), F32)
        for g in range(IDX_HEADS):
            idx = idx + jnp.maximum(_dot(qis[g], ki_t), 0.0) * _lane_tile(ws[g], tk)
        bits = lax.bitcast_convert_type(idx, I32)
        key = jnp.where(bits < 0, bits ^ INT_MAX, bits)
        key = jnp.where(idx == 0.0, 0, key)
        kpos = start + kcol
        kchunk = jnp.where(kpos < l_valid, kpos // CHUNK, INT_MAX)
        key_scr[:, pl.ds(start, tk)] = jnp.where(kchunk <= qchunk, key, INT_MIN)
        return carry

    lax.fori_loop(0, nblk, idx_body, 0)

    rc = min(tq, 128)

    lane_col = lax.broadcasted_iota(I32, (1, LANES), 1)

    def count(indicator, *row_args):
        outs = []
        for r0 in range(0, tq, rc):
            args = [jnp.broadcast_to(a[r0:r0 + rc], (rc, LANES)) for a in row_args]

            def body(j, acc, r0=r0, args=args):
                for c in range(tk // LANES):
                    start = pl.multiple_of(j * tk + c * LANES, LANES)
                    acc = acc + indicator(key_scr[r0:r0 + rc, pl.ds(start, LANES)], start + lane_col, *args)
                return acc

            acc = lax.fori_loop(0, nblk, body, jnp.zeros((rc, LANES), F32))
            outs.append(jnp.sum(acc, axis=1, keepdims=True))
        return outs[0] if len(outs) == 1 else jnp.concatenate(outs, axis=0)

    ge = lambda kt, kp, c: jnp.where(kt >= c, 1.0, 0.0)

    n_adm = jnp.minimum((qchunk + 1) * CHUNK, l_valid).astype(F32)

    def bit_cond(st):
        t, _, _, unsettled = st
        return jnp.logical_and(t < 32, unsettled)

    def bit_body(st):
        t, ans, cnt_ans, _ = st
        cand_u = ans | lax.shift_left(jnp.int32(1), 31 - t)
        cand_s = cand_u ^ INT_MIN
        cnt = count(ge, cand_s)
        take = cnt >= ksel
        ans = jnp.where(take, cand_u, ans)
        cnt_ans = jnp.where(take, cnt, cnt_ans)
        open_rows = jnp.where(cnt_ans == ksel, 0.0, jnp.where(n_adm <= ksel, 0.0, 1.0))
        return t + 1, ans, cnt_ans, jnp.max(open_rows) > 0.0

    _, ans, _, _ = lax.while_loop(
        bit_cond, bit_body, (jnp.int32(0), jnp.zeros((tq, 1), I32), jnp.zeros((tq, 1), F32), jnp.bool_(True)))
    tau = jnp.maximum(ans ^ INT_MIN, INT_MIN + 1)
    c_ge = count(ge, tau)
    c_gt = count(lambda kt, kp, c: jnp.where(kt > c, 1.0, 0.0), tau)
    need = ksel - c_gt
    any_tie = jnp.max(jnp.where(c_ge > ksel, 1.0, 0.0)) > 0.0

    def pos_body(t, p):
        cand = p | lax.shift_left(jnp.int32(1), pos_bits - 1 - t)
        cnt = count(lambda kt, kp, tv, cv: jnp.where(kt == tv, jnp.where(kp < cv, 1.0, 0.0), 0.0), tau, cand)
        return jnp.where(cnt < need, cand, p)

    p_lim = lax.fori_loop(0, jnp.where(any_tie, pos_bits, 0), pos_body, jnp.zeros((tq, 1), I32))
    p_lim = jnp.where(any_tie, p_lim, INT_MAX)

    def bias_body(j, carry):
        start = pl.multiple_of(j * tk, tk)
        kt = key_scr[:, pl.ds(start, tk)]
        kpos = start + kcol
        bias = jnp.where(kt > tau, 0.0, jnp.where(kt == tau, jnp.where(kpos <= p_lim, 0.0, NEG), NEG))
        key_scr[:, pl.ds(start, tk)] = lax.bitcast_convert_type(bias.astype(F32), I32)
        return carry

    lax.fori_loop(0, nblk, bias_body, 0)

    _init_online(m_scr, acc_scr)

    def att_body(j, carry):
        start = pl.multiple_of(j * tk, tk)
        bias = lax.bitcast_convert_type(key_scr[:, pl.ds(start, tk)], F32)
        ss = _head_scores(q_ref, k_ref, start, tk)
        for h in range(HEADS):
            _online_update(h, ss[h] + bias, v_ref[pl.ds(start, tk), h * LANES:(h + 1) * LANES], m_scr, acc_scr)
        return carry

    lax.fori_loop(0, nblk, att_body, 0)
    o_ref[...] = jnp.concatenate([_normalized(acc_scr[h]) for h in range(HEADS)], axis=1).astype(BF16)


def _dsa(qi, misc, q, ki, k, v, *, q_off, l_valid, tq):
    b, t, _ = q.shape
    lp = v.shape[1]
    tk = 512
    assert lp % tk == 0 and t % tq == 0
    ksel = min(TOPK_MAX, l_valid // 4)
    qtok = lambda width: pl.BlockSpec((None, tq, width), lambda bi, i: (bi, i, 0))
    res = lambda width: pl.BlockSpec((None, lp, width), lambda bi, i: (bi, 0, 0), pipeline_mode=pl.Buffered(1))
    res_t = lambda rows: pl.BlockSpec((None, rows, lp), lambda bi, i: (bi, 0, 0), pipeline_mode=pl.Buffered(1))
    kern = functools.partial(_dsa_kernel, tq=tq, tk=tk, q_off=q_off, l_valid=l_valid, ksel=float(ksel),
                             pos_bits=int(lp - 1).bit_length())
    return pl.pallas_call(
        kern,
        grid=(b, t // tq),
        in_specs=[qtok(IDX_HEADS * IDX_DIM), qtok(MISC_W), qtok(HEADS * LANES), res_t(MISC_W), res_t(GROUP_W),
                  res(HEADS * LANES)],
        out_specs=qtok(GROUP_W),
        out_shape=jax.ShapeDtypeStruct((b, t, GROUP_W), BF16),
        scratch_shapes=[pltpu.VMEM((tq, lp), I32)] + _attn_scratch(HEADS, tq),
        compiler_params=_cparams(2),
        name="dsa",
    )(qi, misc, q, ki, k, v)


def _fox_kernel(q_ref, cq_ref, k_ref, v_ref, ck_ref, o_ref, m_scr, acc_scr, *, tq, tk, q_off, l_valid):
    i = pl.program_id(1)
    q0 = q_off + i * tq
    qpos = q0 + lax.broadcasted_iota(I32, (tq, 1), 0)
    nblk = (jnp.minimum(q0 + tq, l_valid) + tk - 1) // tk
    nfull = jnp.minimum((q0 + 1) // tk, l_valid // tk)
    kcol = lax.broadcasted_iota(I32, (1, tk), 1)
    cq = cq_ref[...]
    cqs = [jnp.broadcast_to(cq[:, h:h + 1], (tq, LANES)) for h in range(HEADS)]
    _init_online(m_scr, acc_scr)

    def body(j, carry, masked):
        start = pl.multiple_of(j * tk, tk)
        if masked:
            kpos = start + kcol
            valid = kpos <= qpos
        ss = _head_scores(q_ref, k_ref, start, tk)
        for h in range(HEADS):
            s = ss[h] - ck_ref[h:h + 1, pl.ds(start, tk)]
            if masked:
                s = jnp.where(valid, s, NEG)
            _online_update(h, s, v_ref[pl.ds(start, tk), h * LANES:(h + 1) * LANES], m_scr, acc_scr, shift=cqs[h])
        return carry

    lax.fori_loop(0, nfull, functools.partial(body, masked=False), 0)
    lax.fori_loop(nfull, nblk, functools.partial(body, masked=True), 0)
    o_ref[...] = jnp.concatenate([_normalized(acc_scr[h]) for h in range(HEADS)], axis=1).astype(BF16)


def _fox(q, cq, k, v, ck_t, *, q_off, l_valid, tq):
    b, t, _ = q.shape
    lp = v.shape[1]
    tk = 512
    assert lp % tk == 0 and t % tq == 0
    qtok = lambda width: pl.BlockSpec((None, tq, width), lambda bi, i: (bi, i, 0))
    res = lambda rows, width: pl.BlockSpec((None, rows, width), lambda bi, i: (bi, 0, 0),
                                           pipeline_mode=pl.Buffered(1))
    kern = functools.partial(_fox_kernel, tq=tq, tk=tk, q_off=q_off, l_valid=l_valid)
    return pl.pallas_call(
        kern,
        grid=(b, t // tq),
        in_specs=[qtok(HEADS * LANES), qtok(HEADS), res(GROUP_W, lp), res(lp, HEADS * LANES), res(HEADS, lp)],
        out_specs=qtok(GROUP_W),
        out_shape=jax.ShapeDtypeStruct((b, t, GROUP_W), BF16),
        scratch_shapes=_attn_scratch(HEADS, tq),
        compiler_params=_cparams(2),
        name="fox",
    )(q, cq, k, v, ck_t)


def _diff_kernel(q_ref, k_ref, v_ref, lam_ref, gsub_ref, o_ref, m_scr, acc_scr,
                 *, tq, tk, q_off, l_valid, out_scale):
    i = pl.program_id(1)
    q0 = q_off + i * tq
    qchunk = (q0 + lax.broadcasted_iota(I32, (tq, 1), 0)) // CHUNK
    n_end = jnp.minimum(((q0 + tq - 1) // CHUNK + 1) * CHUNK, l_valid)
    nblk = (n_end + tk - 1) // tk
    nfull = jnp.minimum(((q0 // CHUNK + 1) * CHUNK) // tk, l_valid // tk)
    kcol = lax.broadcasted_iota(I32, (1, tk), 1)
    _init_online(m_scr, acc_scr)

    def body(j, carry, masked):
        start = pl.multiple_of(j * tk, tk)
        if masked:
            kpos = start + kcol
            valid = jnp.where(kpos < l_valid, kpos // CHUNK, INT_MAX) <= qchunk
        for pair in range(HEADS // 2):
            kpair = k_ref[pair * LANES:(pair + 1) * LANES, pl.ds(start, tk)]
            ss = [_dot(q_ref[:, n * LANES:(n + 1) * LANES], kpair) for n in range(4 * pair, 4 * pair + 4)]
            for n in range(4 * pair, 4 * pair + 4):
                s = ss[n - 4 * pair]
                if masked:
                    s = jnp.where(valid, s, NEG)
                _online_update(n, s, v_ref[pl.ds(start, tk), n // 2 * LANES:(n // 2 + 1) * LANES], m_scr, acc_scr)
        return carry

    lax.fori_loop(0, nfull, functools.partial(body, masked=False), 0)
    lax.fori_loop(nfull, nblk, functools.partial(body, masked=True), 0)

    lv = lam_ref[...]
    lam = (jnp.exp(jnp.sum(lv[0:1] * lv[1:2], axis=1, keepdims=True))
           - jnp.exp(jnp.sum(lv[2:3] * lv[3:4], axis=1, keepdims=True)) + (1.0 - out_scale))
    outs = []
    for h in range(HEADS):
        o = _normalized(acc_scr[2 * h]) - lam * _normalized(acc_scr[2 * h + 1])
        outs.append(_rms(o, gsub_ref[...]) * out_scale)
    o_ref[...] = jnp.concatenate(outs, axis=1).astype(BF16)


def _diff(q, k, v, lam_vecs, g_sub, *, q_off, l_valid, tq, lam_init):
    b, t, _ = q.shape
    lp = v.shape[1]
    tk = 512
    assert lp % tk == 0 and t % tq == 0
    qtok = lambda width: pl.BlockSpec((None, tq, width), lambda bi, i: (bi, i, 0))
    res = lambda rows, width: pl.BlockSpec((None, rows, width), lambda bi, i: (bi, 0, 0),
                                           pipeline_mode=pl.Buffered(1))
    const = lambda shape: pl.BlockSpec(shape, lambda bi, i: (0, 0))
    kern = functools.partial(_diff_kernel, tq=tq, tk=tk, q_off=q_off, l_valid=l_valid, out_scale=1.0 - lam_init)
    return pl.pallas_call(
        kern,
        grid=(b, t // tq),
        in_specs=[qtok(2 * HEADS * LANES), res(GROUP_W, lp), res(lp, HEADS * LANES), const((4, DIFF_DIM)),
                  const((1, HEAD_DIM))],
        out_specs=qtok(GROUP_W),
        out_shape=jax.ShapeDtypeStruct((b, t, GROUP_W), BF16),
        scratch_shapes=_attn_scratch(2 * HEADS, tq),
        compiler_params=_cparams(2),
        name="diff",
    )(q, k, v, lam_vecs, g_sub.reshape(1, HEAD_DIM))


def _relbias_kernel(rb_ref, o_ref):
    qi = lax.broadcasted_iota(I32, (CHUNK, BAND_KEYS), 0)
    kj = lax.broadcasted_iota(I32, (CHUNK, BAND_KEYS), 1)
    idx = jnp.clip(BAND_PREV + qi - kj, -REL_CLIP, REL_CLIP) + REL_CLIP

    def body(r, accs):
        return tuple(jnp.where(idx == r, rb_ref[h, r], accs[h]) for h in range(HEADS))

    accs = lax.fori_loop(0, 2 * REL_CLIP + 1, body, tuple(jnp.zeros((CHUNK, BAND_KEYS), F32) for _ in range(HEADS)))
    for h in range(HEADS):
        o_ref[h] = accs[h]


def _relbias(rel_bias_l):
    return pl.pallas_call(
        _relbias_kernel,
        in_specs=[pl.BlockSpec(memory_space=pltpu.SMEM)],
        out_specs=pl.BlockSpec(memory_space=pltpu.VMEM),
        out_shape=jax.ShapeDtypeStruct((HEADS, CHUNK, BAND_KEYS), F32),
        name="relbias",
    )(rel_bias_l.astype(F32))


def _band_kernel(q_ref, kp_ref, kc_ref, vp_ref, vc_ref, bias_ref, o_ref, k_scr, v_scr, *, tq, first_has_no_prev):
    i = pl.program_id(1)
    k_scr[0:BAND_PREV, :] = kp_ref[...]
    k_scr[BAND_PREV:BAND_PREV + tq, :] = kc_ref[...]
    v_scr[0:BAND_PREV, :] = vp_ref[...]
    v_scr[BAND_PREV:BAND_PREV + tq, :] = vc_ref[...]
    kcol = lax.broadcasted_iota(I32, (1, BAND_KEYS), 1)
    first = jnp.where(i == 0, BAND_PREV, 0) if first_has_no_prev else 0
    for c in range(tq // CHUNK):
        r0 = c * CHUNK
        valid = (r0 + kcol) >= first
        outs = []
        for h in range(HEADS):
            hs = slice(h * HEAD_DIM, (h + 1) * HEAD_DIM)
            s = _dot_nt(q_ref[r0:r0 + CHUNK, hs], k_scr[r0:r0 + BAND_KEYS, hs]) + bias_ref[h]
            s = jnp.where(valid, s, NEG)
            e = jnp.exp(s - jnp.max(s, axis=1, keepdims=True))
            o = _dot(e.astype(BF16), v_scr[r0:r0 + BAND_KEYS, hs])
            outs.append(o / jnp.sum(e, axis=1, keepdims=True))
        o_ref[r0:r0 + CHUNK, :] = jnp.concatenate(outs, axis=1).astype(BF16)


def _band(q, k_prev, k_cur, v_prev, v_cur, bias, *, tq, prev_from_cur):
    b, t, _ = q.shape
    assert t % tq == 0 and (not prev_from_cur or tq == BAND_PREV)
    qtok = pl.BlockSpec((None, tq, GROUP_W), lambda bi, i: (bi, i, 0))
    if prev_from_cur:
        prev = pl.BlockSpec((None, BAND_PREV, GROUP_W), lambda bi, i: (bi, jnp.maximum(i - 1, 0), 0))
    else:
        prev = pl.BlockSpec((None, BAND_PREV, GROUP_W), lambda bi, i: (bi, 0, 0))
    bspec = pl.BlockSpec((HEADS, CHUNK, BAND_KEYS), lambda bi, i: (0, 0, 0))
    kern = functools.partial(_band_kernel, tq=tq, first_has_no_prev=prev_from_cur)
    return pl.pallas_call(
        kern,
        grid=(b, t // tq),
        in_specs=[qtok, prev, qtok, prev, qtok, bspec],
        out_specs=qtok,
        out_shape=jax.ShapeDtypeStruct((b, t, GROUP_W), BF16),
        scratch_shapes=[pltpu.VMEM((BAND_PREV + tq, GROUP_W), BF16), pltpu.VMEM((BAND_PREV + tq, GROUP_W), BF16)],
        compiler_params=_cparams(2),
        name="band",
    )(q, k_prev, k_cur, v_prev, v_cur, bias)


def _memkv_kernel(m_ref, g_ref, wk_ref, wv_ref, kf_o, vf_o, kb_o, vb_o):
    hb = _rms(m_ref[...], g_ref[...]).astype(BF16)
    k = _dot(hb, wk_ref[...])
    v = _dot(hb, wv_ref[...])
    kf_o[...] = k
    vf_o[...] = v
    kb_o[...] = k.astype(BF16)
    vb_o[...] = v.astype(BF16)


def _memkv(mem, g, wk, wv):
    b, m, d = mem.shape
    xw = wk.shape[1]
    tok = lambda width: pl.BlockSpec((None, m, width), lambda bi: (bi, 0, 0))
    const = lambda shape: pl.BlockSpec(shape, lambda bi: (0, 0))
    return pl.pallas_call(
        _memkv_kernel,
        grid=(b,),
        in_specs=[tok(d), const((1, d)), const((d, xw)), const((d, xw))],
        out_specs=[tok(xw)] * 4,
        out_shape=[jax.ShapeDtypeStruct((b, m, xw), F32)] * 2 + [jax.ShapeDtypeStruct((b, m, xw), BF16)] * 2,
        compiler_params=_cparams(1),
        name="memkv",
    )(mem, g.reshape(1, d), wk, wv)


def _mix_cross_kernel(x_ref, oa_ref, ob_ref, oc_ref, od_ref, wout_ref, g_ref, wq_ref, mk_ref, mv_ref, wo_ref, o_ref):
    x = x_ref[...]
    for n, oref in enumerate((oa_ref, ob_ref, oc_ref, od_ref)):
        x = x + _dot(oref[...], wout_ref[n * GROUP_W:(n + 1) * GROUP_W, :])
    hb = _rms(x, g_ref[...]).astype(BF16)
    q = _dot(hb, wq_ref[...])
    outs = []
    for h in range(X_HEADS):
        hs = slice(h * X_HEAD_DIM, (h + 1) * X_HEAD_DIM)
        s = _dot_nt(q[:, hs].astype(BF16), mk_ref[:, hs]) * (X_HEAD_DIM ** -0.5)
        e = jnp.exp(s - jnp.max(s, axis=1, keepdims=True))
        p = e / jnp.sum(e, axis=1, keepdims=True)
        outs.append(_dot(p.astype(BF16), mv_ref[:, hs]))
    o = jnp.concatenate(outs, axis=1).astype(BF16)
    o_ref[...] = x + _dot(o, wo_ref[...])


def _mix_cross(x, oa, ob, oc, od, w_out, g, wq, mk, mv, wo):
    b, t, d = x.shape
    tm = min(512, t)
    m, xw = mk.shape[1], mk.shape[2]
    tok = lambda width: pl.BlockSpec((None, tm, width), lambda bi, i: (bi, i, 0))
    const = lambda shape: pl.BlockSpec(shape, lambda bi, i: (0, 0))
    mem = pl.BlockSpec((None, m, xw), lambda bi, i: (bi, 0, 0))
    return pl.pallas_call(
        _mix_cross_kernel,
        grid=(b, t // tm),
        in_specs=[tok(d)] + [tok(GROUP_W)] * 4 + [const(w_out.shape), const((1, d)), const(wq.shape), mem, mem,
                                                   const(wo.shape)],
        out_specs=tok(d),
        out_shape=jax.ShapeDtypeStruct((b, t, d), F32),
        compiler_params=_cparams(2),
        name="mix_cross",
    )(x, oa, ob, oc, od, w_out, g.reshape(1, d), wq, mk, mv, wo)


def _ffn_kernel(x_ref, g_ref, prev_ref, wup_ref, cw_ref, cb_ref, wdn_ref, o_ref, conv_ref, halo_scr, acc_scr,
                *, tm, d_ff, fc):
    i = pl.program_id(1)

    @pl.when(i == 0)
    def _():
        halo_scr[...] = prev_ref[...]

    x = x_ref[...]
    hb = _rms(x, g_ref[...]).astype(BF16)
    row = lax.broadcasted_iota(I32, (tm, fc), 0)
    for c in range(d_ff // fc):
        cs = slice(c * fc, (c + 1) * fc)
        g = _dot(hb, wup_ref[:, cs])
        u = _dot(hb, wup_ref[:, d_ff + c * fc:d_ff + (c + 1) * fc])
        h0 = halo_scr[0:1, cs]
        h1 = halo_scr[1:2, cs]
        gm1 = jnp.where(row == 0, h1, pltpu.roll(g, 1, 0))
        gm2 = jnp.where(row == 0, h0, jnp.where(row == 1, h1, pltpu.roll(g, 2, 0)))
        a = cb_ref[:, cs] + gm2 * cw_ref[0:1, cs]
        a = a + gm1 * cw_ref[1:2, cs]
        a = a + g * cw_ref[2:3, cs]
        y = (a * (1.0 / (1.0 + jnp.exp(-a)))) * u
        f = _dot(y.astype(BF16), wdn_ref[cs, :])
        if c == 0:
            acc_scr[...] = f
        else:
            acc_scr[...] += f
        tail = g[tm - (CONV_W - 1):tm, :]
        halo_scr[:, cs] = tail
        conv_ref[:, cs] = tail
    o_ref[...] = x + acc_scr[...]


def _ffn(x, g, prev, w_up, conv_w, conv_b, w_down):
    b, t, d = x.shape
    d_ff = w_down.shape[0]
    tm = min(512, t)
    fc = 256
    assert d_ff % fc == 0 and t % tm == 0 and tm >= CONV_W - 1
    tok = pl.BlockSpec((None, tm, d), lambda bi, i: (bi, i, 0))
    const = lambda shape: pl.BlockSpec(shape, lambda bi, i: (0, 0))
    st = pl.BlockSpec((None, CONV_W - 1, d_ff), lambda bi, i: (bi, 0, 0))
    kern = functools.partial(_ffn_kernel, tm=tm, d_ff=d_ff, fc=fc)
    return pl.pallas_call(
        kern,
        grid=(b, t // tm),
        in_specs=[tok, const((1, d)), st, const(w_up.shape), const(conv_w.shape), const((1, d_ff)),
                  const(w_down.shape)],
        out_specs=[tok, st],
        out_shape=[jax.ShapeDtypeStruct((b, t, d), F32), jax.ShapeDtypeStruct((b, CONV_W - 1, d_ff), F32)],
        scratch_shapes=[pltpu.VMEM((CONV_W - 1, d_ff), F32), pltpu.VMEM((tm, d), F32)],
        compiler_params=_cparams(2),
        name="ffn",
    )(x, g.reshape(1, d), prev, w_up, conv_w, conv_b.reshape(1, d_ff), w_down)


def _rmsnorm_kernel(x_ref, g_ref, o_ref):
    o_ref[...] = _rms(x_ref[...], g_ref[...])


def _rmsnorm(x, g):
    b, t, d = x.shape
    tm = min(1024, t)
    tok = pl.BlockSpec((None, tm, d), lambda bi, i: (bi, i, 0))
    return pl.pallas_call(
        _rmsnorm_kernel,
        grid=(b, t // tm),
        in_specs=[tok, pl.BlockSpec((1, d), lambda bi, i: (0, 0))],
        out_specs=tok,
        out_shape=jax.ShapeDtypeStruct((b, t, d), F32),
        compiler_params=_cparams(2),
        name="final_norm",
    )(x, g.reshape(1, d))


def _pad_rows(a, rows):
    return a if a.shape[1] == rows else jnp.pad(a, ((0, 0), (0, rows - a.shape[1]), (0, 0)))


def _round_up(n, m):
    return (n + m - 1) // m * m


def _value_slabs_host(v):
    b, l = v.shape[:2]
    one = jnp.zeros((b, l, HEADS, HEAD_DIM), BF16).at[..., 0].set(1.0)
    return jnp.concatenate([v.astype(BF16), one], axis=-1).reshape(b, l, HEADS * LANES)


def _mixers(pr, keys, *, q_off, l_valid, tq_a, tq_bd, band_args, band_tq, prev_from_cur, lam_vecs, g_sub,
            lam_init, logf_all):
    lp = keys["av"].shape[1]
    oa = _dsa(pr["aqi"], pr["misc"], pr["aq"], keys["aki"], keys["ak"], keys["av"],
              q_off=q_off, l_valid=l_valid, tq=tq_a)
    lc = _round_up(logf_all.shape[1], 1024)
    c_t = _cumsum_t(jnp.swapaxes(_pad_rows(logf_all, lc), 1, 2))
    t = pr["bq"].shape[1]
    cq = jnp.swapaxes(c_t[:, :, q_off:q_off + t], 1, 2)
    ck_t = c_t[:, :, :lp] if lc >= lp else jnp.pad(c_t, ((0, 0), (0, 0), (0, lp - lc)))
    ob = _fox(pr["bq"], cq, keys["bk"], keys["bv"], ck_t, q_off=q_off, l_valid=l_valid, tq=tq_bd)
    oc = _band(pr["cq"], *band_args, tq=band_tq, prev_from_cur=prev_from_cur)
    od = _diff(pr["dq"], keys["dk"], keys["dv"], lam_vecs, g_sub, q_off=q_off, l_valid=l_valid, tq=tq_bd,
               lam_init=lam_init)
    return oa, ob, oc, od


def kernel(x_prompt, x_sample, mem_prompt, cache_a_k, cache_a_v, cache_a_kidx, cache_b_k, cache_b_v,
           cache_b_logf, cache_c_k, cache_c_v, cache_d_k, cache_d_v, cache_mem_k, cache_mem_v,
           state_ffn_conv, g_mix, w_in, b_forget, rel_bias, lam_q1, lam_k1, lam_q2, lam_k2, g_sub, w_out,
           g_cross, g_mem, w_xq, w_xk, w_xv, w_xo, g_ffn, w_up, conv_w, conv_b, w_down, g_final):
    depth = w_in.shape[0]
    bp, s_len, d_model = x_prompt.shape
    bs, t_len, _ = x_sample.shape
    past = cache_a_k.shape[2]
    d_ff = w_down.shape[1]
    c_keep = min(BAND_PREV, s_len)
    l_s = past + t_len
    lp_s = _round_up(l_s, 512)

    pos_p = jnp.arange(s_len, dtype=jnp.int32)
    pos_s = jnp.tile(past + jnp.arange(t_len, dtype=jnp.int32), bs)
    h64 = HEAD_DIM // ROT_FRACTION // 2
    h32 = DIFF_DIM // ROT_FRACTION // 2
    t64_p, t32_p = _rope_table(pos_p, h64, HEAD_DIM), _rope_table(pos_p, h32, DIFF_DIM)
    t64_s, t32_s = _rope_table(pos_s, h64, HEAD_DIM), _rope_table(pos_s, h32, DIFF_DIM)

    flat = lambda a: a.reshape(a.shape[0], a.shape[1], -1)
    heads = lambda a, hd=HEAD_DIM: a.reshape(a.shape[0], a.shape[1], a.shape[2] // hd, hd)

    xp, xs = x_prompt, x_sample
    p_states, s_states = [], []
    for l in range(depth):
        lam_init = 0.8 - 0.6 * math.exp(-0.3 * l)
        w_pack, bfp = _pack_w_in(w_in[l], b_forget[l])
        lam_vecs = jnp.stack([lam_q1[l], lam_k1[l], lam_q2[l], lam_k2[l]]).astype(F32)
        bias = _relbias(rel_bias[l])
        w_out_b, wq_b, wo_b = w_out[l].astype(BF16), w_xq[l].astype(BF16), w_xo[l].astype(BF16)
        w_up_b, w_dn_b = w_up[l].astype(BF16), w_down[l].astype(BF16)

        pr = _proj(xp, g_mix[l], w_pack, bfp, t64_p, t32_p, c_keep)
        logf = pr["misc"][:, :, MISC_BF:MISC_BF + HEADS]
        oa, ob, oc, od = _mixers(
            pr, pr, q_off=0, l_valid=s_len, tq_a=256, tq_bd=512,
            band_args=(pr["ck"], pr["ck"], pr["cv"], pr["cv"], bias), band_tq=BAND_PREV, prev_from_cur=True,
            lam_vecs=lam_vecs, g_sub=g_sub[l], lam_init=lam_init, logf_all=logf)
        mkf, mvf, mkb, mvb = _memkv(mem_prompt, g_mem[l], w_xk[l].astype(BF16), w_xv[l].astype(BF16))
        xp = _mix_cross(xp, oa, ob, oc, od, w_out_b, g_cross[l], wq_b, mkb, mvb, wo_b)
        xp, conv_p = _ffn(xp, g_ffn[l], jnp.zeros((bp, CONV_W - 1, d_ff), F32), w_up_b, conv_w[l], conv_b[l], w_dn_b)
        p_states.append((heads(pr["akf"]), heads(pr["avf"]), pr["misc"][:, :, :IDX_DIM],
                         heads(pr["bkf"]), heads(pr["bvf"]), logf,
                         heads(pr["ckf"]), heads(pr["cvf"]), heads(pr["dkf"]), heads(pr["dvf"]),
                         conv_p, heads(mkf, X_HEAD_DIM), heads(mvf, X_HEAD_DIM)))

        prs = _proj(xs.reshape(1, bs * t_len, d_model), g_mix[l], w_pack, bfp, t64_s, t32_s, bs * t_len)
        prs = {n: a.reshape(bs, t_len, a.shape[-1]) for n, a in prs.items() if n not in ("ak", "aki", "bk", "dk")}
        cat_t = lambda cache, new: jnp.swapaxes(
            _pad_rows(jnp.concatenate([flat(cache), new], axis=1), lp_s).astype(BF16), 1, 2)
        kidx = jnp.concatenate([cache_a_kidx[l], prs["misc"][:, :, :IDX_DIM]], axis=1)
        kidx = jnp.pad(kidx, ((0, 0), (0, lp_s - l_s), (0, MISC_W - IDX_DIM)))
        catv = lambda cache, new: _pad_rows(jnp.concatenate([_value_slabs_host(cache), new], axis=1), lp_s)
        keys = {"aki": jnp.swapaxes(kidx.astype(BF16), 1, 2),
                "ak": cat_t(cache_a_k[l], prs["akf"]), "av": catv(cache_a_v[l], prs["av"]),
                "bk": cat_t(cache_b_k[l], prs["bkf"]), "bv": catv(cache_b_v[l], prs["bv"]),
                "dk": cat_t(cache_d_k[l], prs["dkf"]), "dv": catv(cache_d_v[l], prs["dv"])}
        logf_s = prs["misc"][:, :, MISC_BF:MISC_BF + HEADS]
        logf_all = jnp.concatenate([cache_b_logf[l].astype(F32), logf_s], axis=1)
        oa, ob, oc, od = _mixers(
            prs, keys, q_off=past, l_valid=l_s, tq_a=t_len, tq_bd=t_len,
            band_args=(flat(cache_c_k[l]).astype(BF16), prs["ck"], flat(cache_c_v[l]).astype(BF16), prs["cv"], bias),
            band_tq=t_len, prev_from_cur=False,
            lam_vecs=lam_vecs, g_sub=g_sub[l], lam_init=lam_init, logf_all=logf_all)
        xs = _mix_cross(xs, oa, ob, oc, od, w_out_b, g_cross[l], wq_b,
                        flat(cache_mem_k[l]).astype(BF16), flat(cache_mem_v[l]).astype(BF16), wo_b)
        xs, conv_s = _ffn(xs, g_ffn[l], state_ffn_conv[l], w_up_b, conv_w[l], conv_b[l], w_dn_b)
        s_states.append((heads(prs["akf"]), heads(prs["avf"]), prs["misc"][:, :, :IDX_DIM],
                         heads(prs["bkf"]), heads(prs["bvf"]), logf_s,
                         heads(prs["ckf"]), heads(prs["cvf"]), heads(prs["dkf"]), heads(prs["dvf"]), conv_s))

    y_prompt = _rmsnorm(xp, g_final)
    y_sample = _rmsnorm(xs, g_final)
    p_out = [jnp.stack(z, axis=0) for z in zip(*p_states)]
    s_out = [jnp.stack(z, axis=0) for z in zip(*s_states)]
    return (y_prompt, y_sample, *p_out, *s_out)
```

```python
import functools
import math

import numpy as np
import jax
import jax.numpy as jnp
from jax import lax
from jax.experimental import pallas as pl
from jax.experimental.pallas import tpu as pltpu

F32 = jnp.float32
BF16 = jnp.bfloat16
I32 = jnp.int32

CHUNK = 64
HEAD_DIM = 64
HEADS = 4
GROUP_W = HEADS * HEAD_DIM
ROT_FRACTION = 4
ROPE_THETA = 500000.0
IDX_HEADS = 8
IDX_DIM = 64
TOPK_MAX = 256
BAND_CHUNKS = 8
BAND_KEYS = (BAND_CHUNKS + 1) * CHUNK
BAND_PREV = BAND_CHUNKS * CHUNK
REL_CLIP = 128
DIFF_DIM = HEAD_DIM // 2
X_HEADS = 4
X_HEAD_DIM = 128
CONV_W = 3
EPS = 1e-6

LANES = 128
VMEM_LIMIT_BYTES = 56 * 1024 * 1024

NEG = -1e30
INT_MIN = -2147483648
INT_MAX = 2147483647

MISC_W = LANES
MISC_AW = IDX_DIM
MISC_BF = IDX_DIM + IDX_HEADS
PROJ_COLS = 3 * GROUP_W + IDX_HEADS * IDX_DIM + MISC_W + 9 * GROUP_W


def _cparams(n_axes):
    return pltpu.CompilerParams(dimension_semantics=("arbitrary",) * n_axes,
                                vmem_limit_bytes=VMEM_LIMIT_BYTES)


def _rms(x, g):
    return (x * lax.rsqrt(jnp.mean(x * x, axis=-1, keepdims=True) + EPS)) * g


def _dot(a, b):
    return jnp.dot(a, b, preferred_element_type=F32)


def _dot_nt(a, b):
    return lax.dot_general(a, b, (((1,), (1,)), ((), ())), preferred_element_type=F32)


def _rope(z, tab, half):
    w = z.shape[1]
    reps = w // LANES
    rep = lambda t: t if reps == 1 else jnp.concatenate([t] * reps, axis=1)
    c = rep(tab[:, 0:LANES])
    s1 = rep(tab[:, LANES:2 * LANES])
    s2 = rep(tab[:, 2 * LANES:3 * LANES])
    return z * c + pltpu.roll(z, w - half, 1) * s1 + pltpu.roll(z, half, 1) * s2


def _slab_masked(z, widths):
    lane = lax.broadcasted_iota(I32, (z.shape[0], LANES), 1)
    outs = []
    for j in range(GROUP_W // widths):
        lo = (j * widths) % LANES
        zs = z[:, (j * widths) // LANES * LANES:((j * widths) // LANES + 1) * LANES]
        outs.append(jnp.where(lane < lo, 0.0, jnp.where(lane < lo + widths, zs, 0.0)))
    return jnp.concatenate(outs, axis=1)


def _value_slabs(z):
    one = jnp.where(lax.broadcasted_iota(I32, (z.shape[0], HEAD_DIM), 1) == 0, 1.0, 0.0)
    return jnp.concatenate([t for h in range(HEADS) for t in (z[:, h * HEAD_DIM:(h + 1) * HEAD_DIM], one)], axis=1)


def _proj_kernel(x_ref, g_ref, w_ref, bf_ref, t64_ref, t32_ref,
                 aq_o, ak_o, av_o, aqi_o, aki_o, bq_o, bk_o, bv_o, cq_o, ck_o, cv_o, dq_o, dk_o, dv_o,
                 akf_o, avf_o, misc_o, bkf_o, bvf_o, ckf_o, cvf_o, dkf_o, dvf_o):
    hb = _rms(x_ref[...], g_ref[...]).astype(BF16)
    t64 = t64_ref[...]
    t32 = t32_ref[...]
    h64 = HEAD_DIM // ROT_FRACTION // 2
    h32 = DIFF_DIM // ROT_FRACTION // 2
    qscale = HEAD_DIM ** -0.5
    col = [0]

    def mm(width):
        c0 = col[0]
        col[0] = c0 + width
        return _dot(hb, w_ref[:, c0:c0 + width])

    z = _rope(mm(GROUP_W), t64, h64)
    aq_o[...] = _slab_masked(z * qscale, HEAD_DIM).astype(BF16)
    z = _rope(mm(GROUP_W), t64, h64)
    akf_o[...] = z
    ak_o[...] = z.T.astype(BF16)
    z = mm(GROUP_W)
    avf_o[...] = z
    av_o[...] = _value_slabs(z).astype(BF16)
    z = _rope(mm(IDX_HEADS * IDX_DIM), t64, h64)
    aqi_o[...] = (z * (IDX_DIM ** -0.5)).astype(BF16)
    z = mm(MISC_W)
    r = _rope(z, t64, h64)
    lane = lax.broadcasted_iota(I32, z.shape, 1)
    zf = z + bf_ref[...]
    logsig = jnp.minimum(zf, 0.0) - jnp.log1p(jnp.exp(-jnp.abs(zf)))
    misc_o[...] = jnp.where(lane < MISC_AW, r, jnp.where(lane < MISC_BF, z * (IDX_HEADS ** -0.5), logsig))
    aki_o[...] = jnp.where(lane < MISC_AW, r, 0.0).T.astype(BF16)
    z = mm(GROUP_W)
    bq_o[...] = _slab_masked(z * qscale, HEAD_DIM).astype(BF16)
    z = mm(GROUP_W)
    bkf_o[...] = z
    bk_o[...] = z.T.astype(BF16)
    z = mm(GROUP_W)
    bvf_o[...] = z
    bv_o[...] = _value_slabs(z).astype(BF16)
    z = mm(GROUP_W)
    cq_o[...] = (z * qscale).astype(BF16)
    z = mm(GROUP_W)
    ckf_o[...] = z
    ck_o[...] = z.astype(BF16)
    z = mm(GROUP_W)
    cvf_o[...] = z
    cv_o[...] = z.astype(BF16)
    z = _rope(mm(GROUP_W), t32, h32)
    dq_o[...] = _slab_masked(z * (DIFF_DIM ** -0.5), DIFF_DIM).astype(BF16)
    z = _rope(mm(GROUP_W), t32, h32)
    dkf_o[...] = z
    dk_o[...] = z.T.astype(BF16)
    z = mm(GROUP_W)
    dvf_o[...] = z
    dv_o[...] = _value_slabs(z).astype(BF16)


def _rope_table(pos, half, period):
    inv = ROPE_THETA ** (-jnp.arange(half, dtype=F32) / half)
    ang = pos.astype(F32)[:, None] * inv[None, :]
    cos, sin = jnp.cos(ang), jnp.sin(ang)
    t = pos.shape[0]
    rest = period - 2 * half
    zh = jnp.zeros((t, half), F32)
    zr = jnp.zeros((t, rest), F32)
    c = jnp.concatenate([cos, cos, jnp.ones((t, rest), F32)], axis=1)
    s1 = jnp.concatenate([-sin, zh, zr], axis=1)
    s2 = jnp.concatenate([zh, sin, zr], axis=1)
    rep = lambda a: jnp.tile(a, (1, LANES // period))
    return jnp.concatenate([rep(c), rep(s1), rep(s2)], axis=1)


def _pack_w_in(w_in_l, b_forget_l):
    sizes = (GROUP_W, GROUP_W, GROUP_W, IDX_HEADS * IDX_DIM, IDX_DIM, IDX_HEADS,
             GROUP_W, GROUP_W, GROUP_W, HEADS, GROUP_W, GROUP_W, GROUP_W, GROUP_W, GROUP_W, GROUP_W)
    pts = [int(p) for p in np.cumsum(sizes)[:-1]]
    (aq, ak, av, aqi, aki, aw, bq, bk, bv, bf, cq, ck, cv, dq, dk, dv) = jnp.split(w_in_l, pts, axis=1)
    d = w_in_l.shape[0]
    pad = jnp.zeros((d, MISC_W - IDX_DIM - IDX_HEADS - HEADS), w_in_l.dtype)
    misc = jnp.concatenate([aki, aw, bf, pad], axis=1)
    w = jnp.concatenate([aq, ak, av, aqi, misc, bq, bk, bv, cq, ck, cv, dq, dk, dv], axis=1).astype(BF16)
    bfp = jnp.zeros((1, MISC_W), F32).at[0, MISC_BF:MISC_BF + HEADS].set(b_forget_l.astype(F32))
    return w, bfp


def _proj(x, g, w, bfp, t64, t32, c_keep):
    b, t, d = x.shape
    tm = min(512, t)
    nt = t // tm
    nkeep = c_keep // tm
    tok = lambda width: pl.BlockSpec((None, tm, width), lambda bi, i: (bi, i, 0))
    ctok = pl.BlockSpec((None, tm, GROUP_W), lambda bi, i: (bi, jnp.maximum(i - (nt - nkeep), 0), 0))
    const = lambda shape: pl.BlockSpec(shape, lambda bi, i: (0, 0))
    tab = pl.BlockSpec((tm, 3 * LANES), lambda bi, i: (i, 0))
    sds = lambda width, dt, rows=t: jax.ShapeDtypeStruct((b, rows, width), dt)
    bf_names = ["aq", "ak", "av", "aqi", "aki", "bq", "bk", "bv", "cq", "ck", "cv", "dq", "dk", "dv"]
    qx_w, vx_w = HEADS * LANES, HEADS * LANES
    bf_w = [qx_w, GROUP_W, vx_w, IDX_HEADS * IDX_DIM, MISC_W, qx_w, GROUP_W, vx_w, GROUP_W, GROUP_W, GROUP_W,
            2 * qx_w, GROUP_W, vx_w]
    f_names = ["akf", "avf", "misc", "bkf", "bvf", "ckf", "cvf", "dkf", "dvf"]
    f_w = [GROUP_W, GROUP_W, MISC_W] + [GROUP_W] * 6
    out_shape = [sds(wd, BF16) for wd in bf_w]
    out_specs = [tok(wd) for wd in bf_w]
    for n, wd in zip(bf_names, bf_w):
        if n in ("ak", "aki", "bk", "dk"):
            out_shape[bf_names.index(n)] = jax.ShapeDtypeStruct((b, wd, t), BF16)
            out_specs[bf_names.index(n)] = pl.BlockSpec((None, wd, tm), lambda bi, i: (bi, 0, i))
    for n, wd in zip(f_names, f_w):
        if n in ("ckf", "cvf"):
            out_shape.append(sds(wd, F32, c_keep))
            out_specs.append(ctok)
        else:
            out_shape.append(sds(wd, F32))
            out_specs.append(tok(wd))
    outs = pl.pallas_call(
        _proj_kernel,
        grid=(b, nt),
        in_specs=[tok(d), const((1, d)), const((d, PROJ_COLS)), const((1, MISC_W)), tab, tab],
        out_specs=out_specs,
        out_shape=out_shape,
        compiler_params=_cparams(2),
        name="proj",
    )(x, g.reshape(1, d), w, bfp, t64, t32)
    return dict(zip(bf_names + f_names, outs))


def _cumsum_kernel(x_ref, o_ref, *, nb):
    x = x_ref[...]
    lane = lax.broadcasted_iota(I32, x.shape, 1)
    d = 1
    while d < LANES:
        x = x + jnp.where(lane >= d, pltpu.roll(x, d, 1), 0.0)
        d *= 2
    row = lax.broadcasted_iota(I32, x.shape, 0) % nb
    tot = jnp.broadcast_to(x[:, LANES - 1:LANES], x.shape)
    exc = jnp.where(row >= 1, pltpu.roll(tot, 1, 0), 0.0)
    d = 1
    while d < nb:
        exc = exc + jnp.where(row >= d, pltpu.roll(exc, d, 0), 0.0)
        d *= 2
    o_ref[...] = x + exc


def _cumsum_t(logf_t):
    b, h, l = logf_t.shape
    nb = l // LANES
    x = logf_t.reshape(b, h * nb, LANES)
    spec = pl.BlockSpec((None, h * nb, LANES), lambda bi: (bi, 0, 0))
    out = pl.pallas_call(
        functools.partial(_cumsum_kernel, nb=nb),
        grid=(b,),
        in_specs=[spec],
        out_specs=spec,
        out_shape=jax.ShapeDtypeStruct(x.shape, F32),
        compiler_params=_cparams(1),
        name="cumsum",
    )(x)
    return out.reshape(b, h, l)


def _lane_tile(a, width):
    return a if width == LANES else jnp.concatenate([a] * (width // LANES), axis=1)


def _online_update(j, s, vx, m_scr, acc_scr, shift=None):
    m_prev = m_scr[j]
    rmax = jnp.max(s, axis=1, keepdims=True)
    if shift is not None:
        rmax = rmax + shift
    m_new = jnp.maximum(m_prev, rmax)
    sub = m_new if shift is None else m_new - shift
    p = jnp.exp(s - _lane_tile(sub, s.shape[1]))
    acc_scr[j] = jnp.exp(m_prev - m_new) * acc_scr[j] + _dot(p.astype(BF16), vx)
    m_scr[j] = m_new


def _head_scores(q_ref, kt_ref, start, tk):
    return [_dot(q_ref[:, h * LANES:(h + 1) * LANES], kt_ref[h // 2 * LANES:(h // 2 + 1) * LANES, pl.ds(start, tk)])
            for h in range(HEADS)]


def _init_online(m_scr, acc_scr):
    m_scr[...] = jnp.full(m_scr.shape, NEG, F32)
    acc_scr[...] = jnp.zeros(acc_scr.shape, F32)


def _normalized(acc):
    return acc[:, 0:HEAD_DIM] / acc[:, HEAD_DIM:HEAD_DIM + 1]


def _attn_scratch(n_state, tq):
    return [pltpu.VMEM((n_state, tq, LANES), F32), pltpu.VMEM((n_state, tq, LANES), F32)]


def _dsa_kernel(qi_ref, w_ref, q_ref, ki_ref, k_ref, v_ref, o_ref, key_scr, gmax_scr, m_scr, acc_scr,
                *, tq, tk, q_off, l_valid, ksel, pos_bits):
    i = pl.program_id(1)
    q0 = q_off + i * tq
    qchunk = (q0 + lax.broadcasted_iota(I32, (tq, 1), 0)) // CHUNK
    n_end = jnp.minimum(((q0 + tq - 1) // CHUNK + 1) * CHUNK, l_valid)
    nblk = (n_end + tk - 1) // tk
    kcol = lax.broadcasted_iota(I32, (1, tk), 1)

    qi = qi_ref[...]
    w = w_ref[...]
    qis = [qi[:, g * IDX_DIM:(g + 1) * IDX_DIM] for g in range(IDX_HEADS)]
    ws = [jnp.broadcast_to(w[:, MISC_AW + g:MISC_AW + g + 1], (tq, LANES)) for g in range(IDX_HEADS)]

    def idx_body(j, carry):
        start = pl.multiple_of(j * tk, tk)
        ki_t = ki_ref[0:IDX_DIM, pl.ds(start, tk)]
        idx = jnp.zeros((tq, tk), F32)
        for g in range(IDX_HEADS):
            idx = idx + jnp.maximum(_dot(qis[g], ki_t), 0.0) * _lane_tile(ws[g], tk)
        bits = lax.bitcast_convert_type(idx, I32)
        key = jnp.where(bits < 0, bits ^ INT_MAX, bits)
        key = jnp.where(idx == 0.0, 0, key)
        kpos = start + kcol
        kchunk = jnp.where(kpos < l_valid, kpos // CHUNK, INT_MAX)
        key = jnp.where(kchunk <= qchunk, key, INT_MIN)
        key_scr[:, pl.ds(start, tk)] = key
        gmax_scr[...] = jnp.maximum(gmax_scr[...], key)
        return carry

    gmax_scr[...] = jnp.full((tq, tk), INT_MIN, I32)
    lax.fori_loop(0, nblk, idx_body, 0)

    rc = min(tq, 128)

    lane_col = lax.broadcasted_iota(I32, (1, LANES), 1)

    def count(indicator, *row_args, src=key_scr, ntiles=nblk):
        outs = []
        for r0 in range(0, tq, rc):
            args = [a[r0:r0 + rc] for a in row_args]

            def body(j, acc, r0=r0, args=args):
                for c in range(tk // LANES):
                    start = pl.multiple_of(j * tk + c * LANES, LANES)
                    acc = acc + indicator(src[r0:r0 + rc, pl.ds(start, LANES)], start + lane_col, *args)
                return acc

            acc = lax.fori_loop(0, ntiles, body, jnp.zeros((rc, LANES), F32))
            outs.append(jnp.broadcast_to(jnp.sum(acc, axis=1, keepdims=True), (rc, LANES)))
        return outs[0] if len(outs) == 1 else jnp.concatenate(outs, axis=0)

    ge = lambda kt, kp, c: jnp.where(kt >= c, 1.0, 0.0)
    n_adm = jnp.broadcast_to(jnp.minimum((qchunk + 1) * CHUNK, l_valid).astype(F32), (tq, LANES))

    def lb_body(t, ans):
        cand_u = ans | lax.shift_left(jnp.int32(1), 31 - t)
        cnt = count(ge, cand_u ^ INT_MIN, src=gmax_scr, ntiles=1)
        return jnp.where(cnt >= ksel, cand_u, ans)

    lb = lax.fori_loop(0, 32, lb_body, jnp.zeros((tq, LANES), I32)) ^ INT_MIN
    gm = gmax_scr[...]
    gbits = jnp.where(gm < 0, gm ^ INT_MAX, gm)
    gval = jnp.where(gm == INT_MIN, -jnp.inf, lax.bitcast_convert_type(gbits, F32))
    vmax = jnp.broadcast_to(jnp.max(gval, axis=1, keepdims=True), (tq, LANES))
    mbits = lax.bitcast_convert_type(vmax, I32)
    kmax = jnp.where(mbits < 0, mbits ^ INT_MAX, mbits)
    lo0 = jnp.maximum(lb, INT_MIN + 1)
    hi0 = jnp.minimum(kmax, INT_MAX - 1) + 1

    def bis_cond(st):
        t, _, _, _, unsettled = st
        return jnp.logical_and(t < 34, unsettled)

    def unsettled_rows(lo, hi, c_lo):
        open_rows = jnp.where(c_lo == ksel, 0.0, jnp.where(lo + 1 >= hi, 0.0, jnp.where(n_adm <= ksel, 0.0, 1.0)))
        return jnp.max(open_rows) > 0.0

    def bis_body(st):
        t, lo, hi, c_lo, _ = st
        mid = (lo >> 1) + (hi >> 1) + (lo & hi & 1)
        cnt = count(ge, mid)
        take = cnt >= ksel
        lo = jnp.where(take, mid, lo)
        c_lo = jnp.where(take, cnt, c_lo)
        hi = jnp.where(take, hi, mid)
        return t + 1, lo, hi, c_lo, unsettled_rows(lo, hi, c_lo)

    c_lo0 = count(ge, lo0)
    _, tau, _, c_ge, _ = lax.while_loop(
        bis_cond, bis_body, (jnp.int32(0), lo0, hi0, c_lo0, unsettled_rows(lo0, hi0, c_lo0)))
    any_tie = jnp.max(jnp.where(c_ge > ksel, 1.0, 0.0)) > 0.0

    tie_tiles = jnp.where(any_tie, nblk, 0)
    need = ksel - count(lambda kt, kp, c: jnp.where(kt > c, 1.0, 0.0), tau, ntiles=tie_tiles)

    def pos_body(t, p):
        cand = p | lax.shift_left(jnp.int32(1), pos_bits - 1 - t)
        cnt = count(lambda kt, kp, tv, cv: jnp.where(kt == tv, jnp.where(kp < cv, 1.0, 0.0), 0.0), tau, cand)
        return jnp.where(cnt < need, cand, p)

    p_lim = lax.fori_loop(0, jnp.where(any_tie, pos_bits, 0), pos_body, jnp.zeros((tq, LANES), I32))
    p_lim = jnp.where(any_tie, p_lim, INT_MAX)

    def bias_body(j, carry):
        start = pl.multiple_of(j * tk, tk)
        kt = key_scr[:, pl.ds(start, tk)]
        kpos = start + kcol
        tau_t = _lane_tile(tau, tk)
        bias = jnp.where(kt > tau_t, 0.0,
                         jnp.where(kt == tau_t, jnp.where(kpos <= _lane_tile(p_lim, tk), 0.0, NEG), NEG))
        key_scr[:, pl.ds(start, tk)] = lax.bitcast_convert_type(bias.astype(F32), I32)
        return carry

    lax.fori_loop(0, nblk, bias_body, 0)

    _init_online(m_scr, acc_scr)

    def att_body(j, carry):
        start = pl.multiple_of(j * tk, tk)
        bias = lax.bitcast_convert_type(key_scr[:, pl.ds(start, tk)], F32)
        ss = _head_scores(q_ref, k_ref, start, tk)
        for h in range(HEADS):
            _online_update(h, ss[h] + bias, v_ref[pl.ds(start, tk), h * LANES:(h + 1) * LANES], m_scr, acc_scr)
        return carry

    lax.fori_loop(0, nblk, att_body, 0)
    o_ref[...] = jnp.concatenate([_normalized(acc_scr[h]) for h in range(HEADS)], axis=1).astype(BF16)


def _dsa(qi, misc, q, ki, k, v, *, q_off, l_valid, tq):
    b, t, _ = q.shape
    lp = v.shape[1]
    tk = 512
    assert lp % tk == 0 and t % tq == 0
    ksel = min(TOPK_MAX, l_valid // 4)
    qtok = lambda width: pl.BlockSpec((None, tq, width), lambda bi, i: (bi, i, 0))
    res = lambda width: pl.BlockSpec((None, lp, width), lambda bi, i: (bi, 0, 0), pipeline_mode=pl.Buffered(1))
    res_t = lambda rows: pl.BlockSpec((None, rows, lp), lambda bi, i: (bi, 0, 0), pipeline_mode=pl.Buffered(1))
    kern = functools.partial(_dsa_kernel, tq=tq, tk=tk, q_off=q_off, l_valid=l_valid, ksel=float(ksel),
                             pos_bits=int(lp - 1).bit_length())
    return pl.pallas_call(
        kern,
        grid=(b, t // tq),
        in_specs=[qtok(IDX_HEADS * IDX_DIM), qtok(MISC_W), qtok(HEADS * LANES), res_t(MISC_W), res_t(GROUP_W),
                  res(HEADS * LANES)],
        out_specs=qtok(GROUP_W),
        out_shape=jax.ShapeDtypeStruct((b, t, GROUP_W), BF16),
        scratch_shapes=[pltpu.VMEM((tq, lp), I32), pltpu.VMEM((tq, tk), I32)] + _attn_scratch(HEADS, tq),
        compiler_params=_cparams(2),
        name="dsa",
    )(qi, misc, q, ki, k, v)


def _fox_kernel(q_ref, cq_ref, k_ref, v_ref, ck_ref, o_ref, m_scr, acc_scr, *, tq, tk, q_off, l_valid):
    i = pl.program_id(1)
    q0 = q_off + i * tq
    qpos = q0 + lax.broadcasted_iota(I32, (tq, 1), 0)
    nblk = (jnp.minimum(q0 + tq, l_valid) + tk - 1) // tk
    nfull = jnp.minimum((q0 + 1) // tk, l_valid // tk)
    kcol = lax.broadcasted_iota(I32, (1, tk), 1)
    cq = cq_ref[...]
    cqs = [jnp.broadcast_to(cq[:, h:h + 1], (tq, LANES)) for h in range(HEADS)]
    _init_online(m_scr, acc_scr)

    def body(j, carry, masked):
        start = pl.multiple_of(j * tk, tk)
        if masked:
            kpos = start + kcol
            valid = kpos <= qpos
        ss = _head_scores(q_ref, k_ref, start, tk)
        for h in range(HEADS):
            s = ss[h] - ck_ref[h:h + 1, pl.ds(start, tk)]
            if masked:
                s = jnp.where(valid, s, NEG)
            _online_update(h, s, v_ref[pl.ds(start, tk), h * LANES:(h + 1) * LANES], m_scr, acc_scr, shift=cqs[h])
        return carry

    lax.fori_loop(0, nfull, functools.partial(body, masked=False), 0)
    lax.fori_loop(nfull, nblk, functools.partial(body, masked=True), 0)
    o_ref[...] = jnp.concatenate([_normalized(acc_scr[h]) for h in range(HEADS)], axis=1).astype(BF16)


def _fox(q, cq, k, v, ck_t, *, q_off, l_valid, tq):
    b, t, _ = q.shape
    lp = v.shape[1]
    tk = 512
    assert lp % tk == 0 and t % tq == 0
    qtok = lambda width: pl.BlockSpec((None, tq, width), lambda bi, i: (bi, i, 0))
    res = lambda rows, width: pl.BlockSpec((None, rows, width), lambda bi, i: (bi, 0, 0),
                                           pipeline_mode=pl.Buffered(1))
    kern = functools.partial(_fox_kernel, tq=tq, tk=tk, q_off=q_off, l_valid=l_valid)
    return pl.pallas_call(
        kern,
        grid=(b, t // tq),
        in_specs=[qtok(HEADS * LANES), qtok(HEADS), res(GROUP_W, lp), res(lp, HEADS * LANES), res(HEADS, lp)],
        out_specs=qtok(GROUP_W),
        out_shape=jax.ShapeDtypeStruct((b, t, GROUP_W), BF16),
        scratch_shapes=_attn_scratch(HEADS, tq),
        compiler_params=_cparams(2),
        name="fox",
    )(q, cq, k, v, ck_t)


def _diff_kernel(q_ref, k_ref, v_ref, lam_ref, gsub_ref, o_ref, m_scr, acc_scr,
                 *, tq, tk, q_off, l_valid, out_scale):
    i = pl.program_id(1)
    q0 = q_off + i * tq
    qchunk = (q0 + lax.broadcasted_iota(I32, (tq, 1), 0)) // CHUNK
    n_end = jnp.minimum(((q0 + tq - 1) // CHUNK + 1) * CHUNK, l_valid)
    nblk = (n_end + tk - 1) // tk
    nfull = jnp.minimum(((q0 // CHUNK + 1) * CHUNK) // tk, l_valid // tk)
    kcol = lax.broadcasted_iota(I32, (1, tk), 1)
    _init_online(m_scr, acc_scr)

    def body(j, carry, masked):
        start = pl.multiple_of(j * tk, tk)
        if masked:
            kpos = start + kcol
            valid = jnp.where(kpos < l_valid, kpos // CHUNK, INT_MAX) <= qchunk
        for pair in range(HEADS // 2):
            kpair = k_ref[pair * LANES:(pair + 1) * LANES, pl.ds(start, tk)]
            ss = [_dot(q_ref[:, n * LANES:(n + 1) * LANES], kpair) for n in range(4 * pair, 4 * pair + 4)]
            for n in range(4 * pair, 4 * pair + 4):
                s = ss[n - 4 * pair]
                if masked:
                    s = jnp.where(valid, s, NEG)
                _online_update(n, s, v_ref[pl.ds(start, tk), n // 2 * LANES:(n // 2 + 1) * LANES], m_scr, acc_scr)
        return carry

    lax.fori_loop(0, nfull, functools.partial(body, masked=False), 0)
    lax.fori_loop(nfull, nblk, functools.partial(body, masked=True), 0)

    lv = lam_ref[...]
    lam = (jnp.exp(jnp.sum(lv[0:1] * lv[1:2], axis=1, keepdims=True))
           - jnp.exp(jnp.sum(lv[2:3] * lv[3:4], axis=1, keepdims=True)) + (1.0 - out_scale))
    outs = []
    for h in range(HEADS):
        o = _normalized(acc_scr[2 * h]) - lam * _normalized(acc_scr[2 * h + 1])
        outs.append(_rms(o, gsub_ref[...]) * out_scale)
    o_ref[...] = jnp.concatenate(outs, axis=1).astype(BF16)


def _diff(q, k, v, lam_vecs, g_sub, *, q_off, l_valid, tq, lam_init):
    b, t, _ = q.shape
    lp = v.shape[1]
    tk = 512
    assert lp % tk == 0 and t % tq == 0
    qtok = lambda width: pl.BlockSpec((None, tq, width), lambda bi, i: (bi, i, 0))
    res = lambda rows, width: pl.BlockSpec((None, rows, width), lambda bi, i: (bi, 0, 0),
                                           pipeline_mode=pl.Buffered(1))
    const = lambda shape: pl.BlockSpec(shape, lambda bi, i: (0, 0))
    kern = functools.partial(_diff_kernel, tq=tq, tk=tk, q_off=q_off, l_valid=l_valid, out_scale=1.0 - lam_init)
    return pl.pallas_call(
        kern,
        grid=(b, t // tq),
        in_specs=[qtok(2 * HEADS * LANES), res(GROUP_W, lp), res(lp, HEADS * LANES), const((4, DIFF_DIM)),
                  const((1, HEAD_DIM))],
        out_specs=qtok(GROUP_W),
        out_shape=jax.ShapeDtypeStruct((b, t, GROUP_W), BF16),
        scratch_shapes=_attn_scratch(2 * HEADS, tq),
        compiler_params=_cparams(2),
        name="diff",
    )(q, k, v, lam_vecs, g_sub.reshape(1, HEAD_DIM))


def _relbias_kernel(rb_ref, o_ref):
    qi = lax.broadcasted_iota(I32, (CHUNK, BAND_KEYS), 0)
    kj = lax.broadcasted_iota(I32, (CHUNK, BAND_KEYS), 1)
    idx = jnp.clip(BAND_PREV + qi - kj, -REL_CLIP, REL_CLIP) + REL_CLIP

    def body(r, accs):
        return tuple(jnp.where(idx == r, rb_ref[h, r], accs[h]) for h in range(HEADS))

    accs = lax.fori_loop(0, 2 * REL_CLIP + 1, body, tuple(jnp.zeros((CHUNK, BAND_KEYS), F32) for _ in range(HEADS)))
    for h in range(HEADS):
        o_ref[h] = accs[h]


def _relbias(rel_bias_l):
    return pl.pallas_call(
        _relbias_kernel,
        in_specs=[pl.BlockSpec(memory_space=pltpu.SMEM)],
        out_specs=pl.BlockSpec(memory_space=pltpu.VMEM),
        out_shape=jax.ShapeDtypeStruct((HEADS, CHUNK, BAND_KEYS), F32),
        name="relbias",
    )(rel_bias_l.astype(F32))


def _band_kernel(q_ref, kp_ref, kc_ref, vp_ref, vc_ref, bias_ref, o_ref, k_scr, v_scr, *, tq, first_has_no_prev):
    i = pl.program_id(1)
    k_scr[0:BAND_PREV, :] = kp_ref[...]
    k_scr[BAND_PREV:BAND_PREV + tq, :] = kc_ref[...]
    v_scr[0:BAND_PREV, :] = vp_ref[...]
    v_scr[BAND_PREV:BAND_PREV + tq, :] = vc_ref[...]
    kcol = lax.broadcasted_iota(I32, (1, BAND_KEYS), 1)
    first = jnp.where(i == 0, BAND_PREV, 0) if first_has_no_prev else 0
    for c in range(tq // CHUNK):
        r0 = c * CHUNK
        valid = (r0 + kcol) >= first
        outs = []
        for h in range(HEADS):
            hs = slice(h * HEAD_DIM, (h + 1) * HEAD_DIM)
            s = _dot_nt(q_ref[r0:r0 + CHUNK, hs], k_scr[r0:r0 + BAND_KEYS, hs]) + bias_ref[h]
            s = jnp.where(valid, s, NEG)
            e = jnp.exp(s - jnp.max(s, axis=1, keepdims=True))
            o = _dot(e.astype(BF16), v_scr[r0:r0 + BAND_KEYS, hs])
            outs.append(o / jnp.sum(e, axis=1, keepdims=True))
        o_ref[r0:r0 + CHUNK, :] = jnp.concatenate(outs, axis=1).astype(BF16)


def _band(q, k_prev, k_cur, v_prev, v_cur, bias, *, tq, prev_from_cur):
    b, t, _ = q.shape
    assert t % tq == 0 and (not prev_from_cur or tq == BAND_PREV)
    qtok = pl.BlockSpec((None, tq, GROUP_W), lambda bi, i: (bi, i, 0))
    if prev_from_cur:
        prev = pl.BlockSpec((None, BAND_PREV, GROUP_W), lambda bi, i: (bi, jnp.maximum(i - 1, 0), 0))
    else:
        prev = pl.BlockSpec((None, BAND_PREV, GROUP_W), lambda bi, i: (bi, 0, 0))
    bspec = pl.BlockSpec((HEADS, CHUNK, BAND_KEYS), lambda bi, i: (0, 0, 0))
    kern = functools.partial(_band_kernel, tq=tq, first_has_no_prev=prev_from_cur)
    return pl.pallas_call(
        kern,
        grid=(b, t // tq),
        in_specs=[qtok, prev, qtok, prev, qtok, bspec],
        out_specs=qtok,
        out_shape=jax.ShapeDtypeStruct((b, t, GROUP_W), BF16),
        scratch_shapes=[pltpu.VMEM((BAND_PREV + tq, GROUP_W), BF16), pltpu.VMEM((BAND_PREV + tq, GROUP_W), BF16)],
        compiler_params=_cparams(2),
        name="band",
    )(q, k_prev, k_cur, v_prev, v_cur, bias)


def _memkv_kernel(m_ref, g_ref, wk_ref, wv_ref, kf_o, vf_o, kb_o, vb_o):
    hb = _rms(m_ref[...], g_ref[...]).astype(BF16)
    k = _dot(hb, wk_ref[...])
    v = _dot(hb, wv_ref[...])
    kf_o[...] = k
    vf_o[...] = v
    kb_o[...] = k.astype(BF16)
    vb_o[...] = v.astype(BF16)


def _memkv(mem, g, wk, wv):
    b, m, d = mem.shape
    xw = wk.shape[1]
    tok = lambda width: pl.BlockSpec((None, m, width), lambda bi: (bi, 0, 0))
    const = lambda shape: pl.BlockSpec(shape, lambda bi: (0, 0))
    return pl.pallas_call(
        _memkv_kernel,
        grid=(b,),
        in_specs=[tok(d), const((1, d)), const((d, xw)), const((d, xw))],
        out_specs=[tok(xw)] * 4,
        out_shape=[jax.ShapeDtypeStruct((b, m, xw), F32)] * 2 + [jax.ShapeDtypeStruct((b, m, xw), BF16)] * 2,
        compiler_params=_cparams(1),
        name="memkv",
    )(mem, g.reshape(1, d), wk, wv)


def _mix_cross_kernel(x_ref, oa_ref, ob_ref, oc_ref, od_ref, wout_ref, g_ref, wq_ref, mk_ref, mv_ref, wo_ref, o_ref):
    x = x_ref[...]
    for n, oref in enumerate((oa_ref, ob_ref, oc_ref, od_ref)):
        x = x + _dot(oref[...], wout_ref[n * GROUP_W:(n + 1) * GROUP_W, :])
    hb = _rms(x, g_ref[...]).astype(BF16)
    q = _dot(hb, wq_ref[...])
    outs = []
    for h in range(X_HEADS):
        hs = slice(h * X_HEAD_DIM, (h + 1) * X_HEAD_DIM)
        s = _dot_nt(q[:, hs].astype(BF16), mk_ref[:, hs]) * (X_HEAD_DIM ** -0.5)
        e = jnp.exp(s - jnp.max(s, axis=1, keepdims=True))
        p = e / jnp.sum(e, axis=1, keepdims=True)
        outs.append(_dot(p.astype(BF16), mv_ref[:, hs]))
    o = jnp.concatenate(outs, axis=1).astype(BF16)
    o_ref[...] = x + _dot(o, wo_ref[...])


def _mix_cross(x, oa, ob, oc, od, w_out, g, wq, mk, mv, wo):
    b, t, d = x.shape
    tm = min(512, t)
    m, xw = mk.shape[1], mk.shape[2]
    tok = lambda width: pl.BlockSpec((None, tm, width), lambda bi, i: (bi, i, 0))
    const = lambda shape: pl.BlockSpec(shape, lambda bi, i: (0, 0))
    mem = pl.BlockSpec((None, m, xw), lambda bi, i: (bi, 0, 0))
    return pl.pallas_call(
        _mix_cross_kernel,
        grid=(b, t // tm),
        in_specs=[tok(d)] + [tok(GROUP_W)] * 4 + [const(w_out.shape), const((1, d)), const(wq.shape), mem, mem,
                                                   const(wo.shape)],
        out_specs=tok(d),
        out_shape=jax.ShapeDtypeStruct((b, t, d), F32),
        compiler_params=_cparams(2),
        name="mix_cross",
    )(x, oa, ob, oc, od, w_out, g.reshape(1, d), wq, mk, mv, wo)


def _ffn_kernel(x_ref, g_ref, prev_ref, wup_ref, cw_ref, cb_ref, wdn_ref, o_ref, conv_ref, halo_scr, acc_scr,
                *, tm, d_ff, fc):
    i = pl.program_id(1)

    @pl.when(i == 0)
    def _():
        halo_scr[...] = prev_ref[...]

    x = x_ref[...]
    hb = _rms(x, g_ref[...]).astype(BF16)
    row = lax.broadcasted_iota(I32, (tm, fc), 0)
    for c in range(d_ff // fc):
        cs = slice(c * fc, (c + 1) * fc)
        g = _dot(hb, wup_ref[:, cs])
        u = _dot(hb, wup_ref[:, d_ff + c * fc:d_ff + (c + 1) * fc])
        h0 = halo_scr[0:1, cs]
        h1 = halo_scr[1:2, cs]
        gm1 = jnp.where(row == 0, h1, pltpu.roll(g, 1, 0))
        gm2 = jnp.where(row == 0, h0, jnp.where(row == 1, h1, pltpu.roll(g, 2, 0)))
        a = cb_ref[:, cs] + gm2 * cw_ref[0:1, cs]
        a = a + gm1 * cw_ref[1:2, cs]
        a = a + g * cw_ref[2:3, cs]
        y = (a * (1.0 / (1.0 + jnp.exp(-a)))) * u
        f = _dot(y.astype(BF16), wdn_ref[cs, :])
        if c == 0:
            acc_scr[...] = f
        else:
            acc_scr[...] += f
        tail = g[tm - (CONV_W - 1):tm, :]
        halo_scr[:, cs] = tail
        conv_ref[:, cs] = tail
    o_ref[...] = x + acc_scr[...]


def _ffn(x, g, prev, w_up, conv_w, conv_b, w_down):
    b, t, d = x.shape
    d_ff = w_down.shape[0]
    tm = min(512, t)
    fc = 256
    assert d_ff % fc == 0 and t % tm == 0 and tm >= CONV_W - 1
    tok = pl.BlockSpec((None, tm, d), lambda bi, i: (bi, i, 0))
    const = lambda shape: pl.BlockSpec(shape, lambda bi, i: (0, 0))
    st = pl.BlockSpec((None, CONV_W - 1, d_ff), lambda bi, i: (bi, 0, 0))
    kern = functools.partial(_ffn_kernel, tm=tm, d_ff=d_ff, fc=fc)
    return pl.pallas_call(
        kern,
        grid=(b, t // tm),
        in_specs=[tok, const((1, d)), st, const(w_up.shape), const(conv_w.shape), const((1, d_ff)),
                  const(w_down.shape)],
        out_specs=[tok, st],
        out_shape=[jax.ShapeDtypeStruct((b, t, d), F32), jax.ShapeDtypeStruct((b, CONV_W - 1, d_ff), F32)],
        scratch_shapes=[pltpu.VMEM((CONV_W - 1, d_ff), F32), pltpu.VMEM((tm, d), F32)],
        compiler_params=_cparams(2),
        name="ffn",
    )(x, g.reshape(1, d), prev, w_up, conv_w, conv_b.reshape(1, d_ff), w_down)


def _rmsnorm_kernel(x_ref, g_ref, o_ref):
    o_ref[...] = _rms(x_ref[...], g_ref[...])


def _rmsnorm(x, g):
    b, t, d = x.shape
    tm = min(1024, t)
    tok = pl.BlockSpec((None, tm, d), lambda bi, i: (bi, i, 0))
    return pl.pallas_call(
        _rmsnorm_kernel,
        grid=(b, t // tm),
        in_specs=[tok, pl.BlockSpec((1, d), lambda bi, i: (0, 0))],
        out_specs=tok,
        out_shape=jax.ShapeDtypeStruct((b, t, d), F32),
        compiler_params=_cparams(2),
        name="final_norm",
    )(x, g.reshape(1, d))


def _pad_rows(a, rows):
    return a if a.shape[1] == rows else jnp.pad(a, ((0, 0), (0, rows - a.shape[1]), (0, 0)))


def _round_up(n, m):
    return (n + m - 1) // m * m


def _value_slabs_host(v):
    b, l = v.shape[:2]
    one = jnp.zeros((b, l, HEADS, HEAD_DIM), BF16).at[..., 0].set(1.0)
    return jnp.concatenate([v.astype(BF16), one], axis=-1).reshape(b, l, HEADS * LANES)


def _mixers(pr, keys, *, q_off, l_valid, tq_a, tq_bd, band_args, band_tq, prev_from_cur, lam_vecs, g_sub,
            lam_init, logf_all):
    lp = keys["av"].shape[1]
    oa = _dsa(pr["aqi"], pr["misc"], pr["aq"], keys["aki"], keys["ak"], keys["av"],
              q_off=q_off, l_valid=l_valid, tq=tq_a)
    lc = _round_up(logf_all.shape[1], 1024)
    c_t = _cumsum_t(jnp.swapaxes(_pad_rows(logf_all, lc), 1, 2))
    t = pr["bq"].shape[1]
    cq = jnp.swapaxes(c_t[:, :, q_off:q_off + t], 1, 2)
    ck_t = c_t[:, :, :lp] if lc >= lp else jnp.pad(c_t, ((0, 0), (0, 0), (0, lp - lc)))
    ob = _fox(pr["bq"], cq, keys["bk"], keys["bv"], ck_t, q_off=q_off, l_valid=l_valid, tq=tq_bd)
    oc = _band(pr["cq"], *band_args, tq=band_tq, prev_from_cur=prev_from_cur)
    od = _diff(pr["dq"], keys["dk"], keys["dv"], lam_vecs, g_sub, q_off=q_off, l_valid=l_valid, tq=tq_bd,
               lam_init=lam_init)
    return oa, ob, oc, od


def kernel(x_prompt, x_sample, mem_prompt, cache_a_k, cache_a_v, cache_a_kidx, cache_b_k, cache_b_v,
           cache_b_logf, cache_c_k, cache_c_v, cache_d_k, cache_d_v, cache_mem_k, cache_mem_v,
           state_ffn_conv, g_mix, w_in, b_forget, rel_bias, lam_q1, lam_k1, lam_q2, lam_k2, g_sub, w_out,
           g_cross, g_mem, w_xq, w_xk, w_xv, w_xo, g_ffn, w_up, conv_w, conv_b, w_down, g_final):
    depth = w_in.shape[0]
    bp, s_len, d_model = x_prompt.shape
    bs, t_len, _ = x_sample.shape
    past = cache_a_k.shape[2]
    d_ff = w_down.shape[1]
    c_keep = min(BAND_PREV, s_len)
    l_s = past + t_len
    lp_s = _round_up(l_s, 512)

    pos_p = jnp.arange(s_len, dtype=jnp.int32)
    pos_s = jnp.tile(past + jnp.arange(t_len, dtype=jnp.int32), bs)
    h64 = HEAD_DIM // ROT_FRACTION // 2
    h32 = DIFF_DIM // ROT_FRACTION // 2
    t64_p, t32_p = _rope_table(pos_p, h64, HEAD_DIM), _rope_table(pos_p, h32, DIFF_DIM)
    t64_s, t32_s = _rope_table(pos_s, h64, HEAD_DIM), _rope_table(pos_s, h32, DIFF_DIM)

    flat = lambda a: a.reshape(a.shape[0], a.shape[1], -1)
    heads = lambda a, hd=HEAD_DIM: a.reshape(a.shape[0], a.shape[1], a.shape[2] // hd, hd)

    xp, xs = x_prompt, x_sample
    p_states, s_states = [], []
    for l in range(depth):
        lam_init = 0.8 - 0.6 * math.exp(-0.3 * l)
        w_pack, bfp = _pack_w_in(w_in[l], b_forget[l])
        lam_vecs = jnp.stack([lam_q1[l], lam_k1[l], lam_q2[l], lam_k2[l]]).astype(F32)
        bias = _relbias(rel_bias[l])
        w_out_b, wq_b, wo_b = w_out[l].astype(BF16), w_xq[l].astype(BF16), w_xo[l].astype(BF16)
        w_up_b, w_dn_b = w_up[l].astype(BF16), w_down[l].astype(BF16)

        pr = _proj(xp, g_mix[l], w_pack, bfp, t64_p, t32_p, c_keep)
        logf = pr["misc"][:, :, MISC_BF:MISC_BF + HEADS]
        oa, ob, oc, od = _mixers(
            pr, pr, q_off=0, l_valid=s_len, tq_a=256, tq_bd=512,
            band_args=(pr["ck"], pr["ck"], pr["cv"], pr["cv"], bias), band_tq=BAND_PREV, prev_from_cur=True,
            lam_vecs=lam_vecs, g_sub=g_sub[l], lam_init=lam_init, logf_all=logf)
        mkf, mvf, mkb, mvb = _memkv(mem_prompt, g_mem[l], w_xk[l].astype(BF16), w_xv[l].astype(BF16))
        xp = _mix_cross(xp, oa, ob, oc, od, w_out_b, g_cross[l], wq_b, mkb, mvb, wo_b)
        xp, conv_p = _ffn(xp, g_ffn[l], jnp.zeros((bp, CONV_W - 1, d_ff), F32), w_up_b, conv_w[l], conv_b[l], w_dn_b)
        p_states.append((heads(pr["akf"]), heads(pr["avf"]), pr["misc"][:, :, :IDX_DIM],
                         heads(pr["bkf"]), heads(pr["bvf"]), logf,
                         heads(pr["ckf"]), heads(pr["cvf"]), heads(pr["dkf"]), heads(pr["dvf"]),
                         conv_p, heads(mkf, X_HEAD_DIM), heads(mvf, X_HEAD_DIM)))

        prs = _proj(xs.reshape(1, bs * t_len, d_model), g_mix[l], w_pack, bfp, t64_s, t32_s, bs * t_len)
        prs = {n: a.reshape(bs, t_len, a.shape[-1]) for n, a in prs.items() if n not in ("ak", "aki", "bk", "dk")}
        cat_t = lambda cache, new: jnp.swapaxes(
            _pad_rows(jnp.concatenate([flat(cache), new], axis=1), lp_s).astype(BF16), 1, 2)
        kidx = jnp.concatenate([cache_a_kidx[l], prs["misc"][:, :, :IDX_DIM]], axis=1)
        kidx = jnp.pad(kidx, ((0, 0), (0, lp_s - l_s), (0, MISC_W - IDX_DIM)))
        catv = lambda cache, new: _pad_rows(jnp.concatenate([_value_slabs_host(cache), new], axis=1), lp_s)
        keys = {"aki": jnp.swapaxes(kidx.astype(BF16), 1, 2),
                "ak": cat_t(cache_a_k[l], prs["akf"]), "av": catv(cache_a_v[l], prs["av"]),
                "bk": cat_t(cache_b_k[l], prs["bkf"]), "bv": catv(cache_b_v[l], prs["bv"]),
                "dk": cat_t(cache_d_k[l], prs["dkf"]), "dv": catv(cache_d_v[l], prs["dv"])}
        logf_s = prs["misc"][:, :, MISC_BF:MISC_BF + HEADS]
        logf_all = jnp.concatenate([cache_b_logf[l].astype(F32), logf_s], axis=1)
        oa, ob, oc, od = _mixers(
            prs, keys, q_off=past, l_valid=l_s, tq_a=t_len, tq_bd=t_len,
            band_args=(flat(cache_c_k[l]).astype(BF16), prs["ck"], flat(cache_c_v[l]).astype(BF16), prs["cv"], bias),
            band_tq=t_len, prev_from_cur=False,
            lam_vecs=lam_vecs, g_sub=g_sub[l], lam_init=lam_init, logf_all=logf_all)
        xs = _mix_cross(xs, oa, ob, oc, od, w_out_b, g_cross[l], wq_b,
                        flat(cache_mem_k[l]).astype(BF16), flat(cache_mem_v[l]).astype(BF16), wo_b)
        xs, conv_s = _ffn(xs, g_ffn[l], state_ffn_conv[l], w_up_b, conv_w[l], conv_b[l], w_dn_b)
        s_states.append((heads(prs["akf"]), heads(prs["avf"]), prs["misc"][:, :, :IDX_DIM],
                         heads(prs["bkf"]), heads(prs["bvf"]), logf_s,
                         heads(prs["ckf"]), heads(prs["cvf"]), heads(prs["dkf"]), heads(prs["dvf"]), conv_s))

    y_prompt = _rmsnorm(xp, g_final)
    y_sample = _rmsnorm(xs, g_final)
    p_out = [jnp.stack(z, axis=0) for z in zip(*p_states)]
    s_out = [jnp.stack(z, axis=0) for z in zip(*s_states)]
    return (y_prompt, y_sample, *p_out, *s_out)
```

```python
import functools
import math

import numpy as np
import jax
import jax.numpy as jnp
from jax import lax
from jax.experimental import pallas as pl
from jax.experimental.pallas import tpu as pltpu

F32 = jnp.float32
BF16 = jnp.bfloat16
I32 = jnp.int32

CHUNK = 64
HEAD_DIM = 64
HEADS = 4
GROUP_W = HEADS * HEAD_DIM
ROT_FRACTION = 4
ROPE_THETA = 500000.0
IDX_HEADS = 8
IDX_DIM = 64
TOPK_MAX = 256
BAND_CHUNKS = 8
BAND_KEYS = (BAND_CHUNKS + 1) * CHUNK
BAND_PREV = BAND_CHUNKS * CHUNK
REL_CLIP = 128
DIFF_DIM = HEAD_DIM // 2
X_HEADS = 4
X_HEAD_DIM = 128
CONV_W = 3
EPS = 1e-6

LANES = 128
VMEM_LIMIT_BYTES = 56 * 1024 * 1024

NEG = -1e30
INT_MIN = -2147483648
INT_MAX = 2147483647

MISC_W = LANES
MISC_AW = IDX_DIM
MISC_BF = IDX_DIM + IDX_HEADS
PROJ_COLS = 3 * GROUP_W + IDX_HEADS * IDX_DIM + MISC_W + 9 * GROUP_W


def _cparams(n_axes):
    return pltpu.CompilerParams(dimension_semantics=("arbitrary",) * n_axes,
                                vmem_limit_bytes=VMEM_LIMIT_BYTES)


def _rms(x, g):
    return (x * lax.rsqrt(jnp.mean(x * x, axis=-1, keepdims=True) + EPS)) * g


def _dot(a, b):
    return jnp.dot(a, b, preferred_element_type=F32)


def _dot_nt(a, b):
    return lax.dot_general(a, b, (((1,), (1,)), ((), ())), preferred_element_type=F32)


def _rope(z, tab, half):
    w = z.shape[1]
    reps = w // LANES
    rep = lambda t: t if reps == 1 else jnp.concatenate([t] * reps, axis=1)
    c = rep(tab[:, 0:LANES])
    s1 = rep(tab[:, LANES:2 * LANES])
    s2 = rep(tab[:, 2 * LANES:3 * LANES])
    return z * c + pltpu.roll(z, w - half, 1) * s1 + pltpu.roll(z, half, 1) * s2


def _slab_masked(z, widths):
    lane = lax.broadcasted_iota(I32, (z.shape[0], LANES), 1)
    outs = []
    for j in range(GROUP_W // widths):
        lo = (j * widths) % LANES
        zs = z[:, (j * widths) // LANES * LANES:((j * widths) // LANES + 1) * LANES]
        outs.append(jnp.where(lane < lo, 0.0, jnp.where(lane < lo + widths, zs, 0.0)))
    return jnp.concatenate(outs, axis=1)


def _value_slabs(z):
    one = jnp.where(lax.broadcasted_iota(I32, (z.shape[0], HEAD_DIM), 1) == 0, 1.0, 0.0)
    return jnp.concatenate([t for h in range(HEADS) for t in (z[:, h * HEAD_DIM:(h + 1) * HEAD_DIM], one)], axis=1)


def _proj_kernel(x_ref, g_ref, w_ref, bf_ref, t64_ref, t32_ref,
                 aq_o, ak_o, av_o, aqi_o, aki_o, bq_o, bk_o, bv_o, cq_o, ck_o, cv_o, dq_o, dk_o, dv_o,
                 akf_o, avf_o, misc_o, bkf_o, bvf_o, ckf_o, cvf_o, dkf_o, dvf_o):
    hb = _rms(x_ref[...], g_ref[...]).astype(BF16)
    t64 = t64_ref[...]
    t32 = t32_ref[...]
    h64 = HEAD_DIM // ROT_FRACTION // 2
    h32 = DIFF_DIM // ROT_FRACTION // 2
    qscale = HEAD_DIM ** -0.5
    col = [0]

    def mm(width):
        c0 = col[0]
        col[0] = c0 + width
        return _dot(hb, w_ref[:, c0:c0 + width])

    z = _rope(mm(GROUP_W), t64, h64)
    aq_o[...] = _slab_masked(z * qscale, HEAD_DIM).astype(BF16)
    z = _rope(mm(GROUP_W), t64, h64)
    akf_o[...] = z
    ak_o[...] = z.T.astype(BF16)
    z = mm(GROUP_W)
    avf_o[...] = z
    av_o[...] = _value_slabs(z).astype(BF16)
    z = _rope(mm(IDX_HEADS * IDX_DIM), t64, h64)
    aqi_o[...] = (z * (IDX_DIM ** -0.5)).astype(BF16)
    z = mm(MISC_W)
    r = _rope(z, t64, h64)
    lane = lax.broadcasted_iota(I32, z.shape, 1)
    zf = z + bf_ref[...]
    logsig = jnp.minimum(zf, 0.0) - jnp.log1p(jnp.exp(-jnp.abs(zf)))
    misc_o[...] = jnp.where(lane < MISC_AW, r, jnp.where(lane < MISC_BF, z * (IDX_HEADS ** -0.5), logsig))
    aki_o[...] = jnp.where(lane < MISC_AW, r, 0.0).T.astype(BF16)
    z = mm(GROUP_W)
    bq_o[...] = _slab_masked(z * qscale, HEAD_DIM).astype(BF16)
    z = mm(GROUP_W)
    bkf_o[...] = z
    bk_o[...] = z.T.astype(BF16)
    z = mm(GROUP_W)
    bvf_o[...] = z
    bv_o[...] = _value_slabs(z).astype(BF16)
    z = mm(GROUP_W)
    cq_o[...] = (z * qscale).astype(BF16)
    z = mm(GROUP_W)
    ckf_o[...] = z
    ck_o[...] = z.astype(BF16)
    z = mm(GROUP_W)
    cvf_o[...] = z
    cv_o[...] = z.astype(BF16)
    z = _rope(mm(GROUP_W), t32, h32)
    dq_o[...] = _slab_masked(z * (DIFF_DIM ** -0.5), DIFF_DIM).astype(BF16)
    z = _rope(mm(GROUP_W), t32, h32)
    dkf_o[...] = z
    dk_o[...] = z.T.astype(BF16)
    z = mm(GROUP_W)
    dvf_o[...] = z
    dv_o[...] = _value_slabs(z).astype(BF16)


def _rope_table(pos, half, period):
    inv = ROPE_THETA ** (-jnp.arange(half, dtype=F32) / half)
    ang = pos.astype(F32)[:, None] * inv[None, :]
    cos, sin = jnp.cos(ang), jnp.sin(ang)
    t = pos.shape[0]
    rest = period - 2 * half
    zh = jnp.zeros((t, half), F32)
    zr = jnp.zeros((t, rest), F32)
    c = jnp.concatenate([cos, cos, jnp.ones((t, rest), F32)], axis=1)
    s1 = jnp.concatenate([-sin, zh, zr], axis=1)
    s2 = jnp.concatenate([zh, sin, zr], axis=1)
    rep = lambda a: jnp.tile(a, (1, LANES // period))
    return jnp.concatenate([rep(c), rep(s1), rep(s2)], axis=1)


def _pack_w_in(w_in_l, b_forget_l):
    sizes = (GROUP_W, GROUP_W, GROUP_W, IDX_HEADS * IDX_DIM, IDX_DIM, IDX_HEADS,
             GROUP_W, GROUP_W, GROUP_W, HEADS, GROUP_W, GROUP_W, GROUP_W, GROUP_W, GROUP_W, GROUP_W)
    pts = [int(p) for p in np.cumsum(sizes)[:-1]]
    (aq, ak, av, aqi, aki, aw, bq, bk, bv, bf, cq, ck, cv, dq, dk, dv) = jnp.split(w_in_l, pts, axis=1)
    d = w_in_l.shape[0]
    pad = jnp.zeros((d, MISC_W - IDX_DIM - IDX_HEADS - HEADS), w_in_l.dtype)
    misc = jnp.concatenate([aki, aw, bf, pad], axis=1)
    w = jnp.concatenate([aq, ak, av, aqi, misc, bq, bk, bv, cq, ck, cv, dq, dk, dv], axis=1).astype(BF16)
    bfp = jnp.zeros((1, MISC_W), F32).at[0, MISC_BF:MISC_BF + HEADS].set(b_forget_l.astype(F32))
    return w, bfp


def _proj(x, g, w, bfp, t64, t32, c_keep):
    b, t, d = x.shape
    tm = min(512, t)
    nt = t // tm
    nkeep = c_keep // tm
    tok = lambda width: pl.BlockSpec((None, tm, width), lambda bi, i: (bi, i, 0))
    ctok = pl.BlockSpec((None, tm, GROUP_W), lambda bi, i: (bi, jnp.maximum(i - (nt - nkeep), 0), 0))
    const = lambda shape: pl.BlockSpec(shape, lambda bi, i: (0, 0))
    tab = pl.BlockSpec((tm, 3 * LANES), lambda bi, i: (i, 0))
    sds = lambda width, dt, rows=t: jax.ShapeDtypeStruct((b, rows, width), dt)
    bf_names = ["aq", "ak", "av", "aqi", "aki", "bq", "bk", "bv", "cq", "ck", "cv", "dq", "dk", "dv"]
    qx_w, vx_w = HEADS * LANES, HEADS * LANES
    bf_w = [qx_w, GROUP_W, vx_w, IDX_HEADS * IDX_DIM, MISC_W, qx_w, GROUP_W, vx_w, GROUP_W, GROUP_W, GROUP_W,
            2 * qx_w, GROUP_W, vx_w]
    f_names = ["akf", "avf", "misc", "bkf", "bvf", "ckf", "cvf", "dkf", "dvf"]
    f_w = [GROUP_W, GROUP_W, MISC_W] + [GROUP_W] * 6
    out_shape = [sds(wd, BF16) for wd in bf_w]
    out_specs = [tok(wd) for wd in bf_w]
    for n, wd in zip(bf_names, bf_w):
        if n in ("ak", "aki", "bk", "dk"):
            out_shape[bf_names.index(n)] = jax.ShapeDtypeStruct((b, wd, t), BF16)
            out_specs[bf_names.index(n)] = pl.BlockSpec((None, wd, tm), lambda bi, i: (bi, 0, i))
    for n, wd in zip(f_names, f_w):
        if n in ("ckf", "cvf"):
            out_shape.append(sds(wd, F32, c_keep))
            out_specs.append(ctok)
        else:
            out_shape.append(sds(wd, F32))
            out_specs.append(tok(wd))
    outs = pl.pallas_call(
        _proj_kernel,
        grid=(b, nt),
        in_specs=[tok(d), const((1, d)), const((d, PROJ_COLS)), const((1, MISC_W)), tab, tab],
        out_specs=out_specs,
        out_shape=out_shape,
        compiler_params=_cparams(2),
        name="proj",
    )(x, g.reshape(1, d), w, bfp, t64, t32)
    return dict(zip(bf_names + f_names, outs))


def _cumsum_kernel(x_ref, o_ref, *, nb):
    x = x_ref[...]
    lane = lax.broadcasted_iota(I32, x.shape, 1)
    d = 1
    while d < LANES:
        x = x + jnp.where(lane >= d, pltpu.roll(x, d, 1), 0.0)
        d *= 2
    row = lax.broadcasted_iota(I32, x.shape, 0) % nb
    tot = jnp.broadcast_to(x[:, LANES - 1:LANES], x.shape)
    exc = jnp.where(row >= 1, pltpu.roll(tot, 1, 0), 0.0)
    d = 1
    while d < nb:
        exc = exc + jnp.where(row >= d, pltpu.roll(exc, d, 0), 0.0)
        d *= 2
    o_ref[...] = x + exc


def _cumsum_t(logf_t):
    b, h, l = logf_t.shape
    nb = l // LANES
    x = logf_t.reshape(b, h * nb, LANES)
    spec = pl.BlockSpec((None, h * nb, LANES), lambda bi: (bi, 0, 0))
    out = pl.pallas_call(
        functools.partial(_cumsum_kernel, nb=nb),
        grid=(b,),
        in_specs=[spec],
        out_specs=spec,
        out_shape=jax.ShapeDtypeStruct(x.shape, F32),
        compiler_params=_cparams(1),
        name="cumsum",
    )(x)
    return out.reshape(b, h, l)


def _lane_tile(a, width):
    return a if width == LANES else jnp.concatenate([a] * (width // LANES), axis=1)


def _online_update(js, ss, vx, m_scr, acc_scr, shift=None):
    tq = ss[0].shape[0]
    ps, alphas = [], []
    for j, s in zip(js, ss):
        m_prev = m_scr[j]
        rmax = jnp.max(s, axis=1, keepdims=True)
        if shift is not None:
            rmax = rmax + shift
        m_new = jnp.maximum(m_prev, rmax)
        sub = m_new if shift is None else m_new - shift
        ps.append(jnp.exp(s - _lane_tile(sub, s.shape[1])).astype(BF16))
        alphas.append(jnp.exp(m_prev - m_new))
        m_scr[j] = m_new
    pv = _dot(ps[0] if len(ps) == 1 else jnp.concatenate(ps, axis=0), vx)
    for i, j in enumerate(js):
        acc_scr[j] = alphas[i] * acc_scr[j] + pv[i * tq:(i + 1) * tq]


def _stack_maps(q_ref, per_pair):
    n_pairs = q_ref.shape[1] // LANES // per_pair
    return [jnp.concatenate([q_ref[:, m * LANES:(m + 1) * LANES] for m in range(p * per_pair, (p + 1) * per_pair)],
                            axis=0) for p in range(n_pairs)]


def _map_scores(q_stacks, kt_ref, start, tk, tq):
    out = []
    for p, qs in enumerate(q_stacks):
        sp = _dot(qs, kt_ref[p * LANES:(p + 1) * LANES, pl.ds(start, tk)])
        out += [sp[i * tq:(i + 1) * tq] for i in range(qs.shape[0] // tq)]
    return out


def _init_online(m_scr, acc_scr):
    m_scr[...] = jnp.full(m_scr.shape, NEG, F32)
    acc_scr[...] = jnp.zeros(acc_scr.shape, F32)


def _normalized(acc):
    return acc[:, 0:HEAD_DIM] / acc[:, HEAD_DIM:HEAD_DIM + 1]


def _attn_scratch(n_state, tq):
    return [pltpu.VMEM((n_state, tq, LANES), F32), pltpu.VMEM((n_state, tq, LANES), F32)]


def _dsa_kernel(qi_ref, w_ref, q_ref, ki_ref, k_ref, v_ref, o_ref, key_scr, gmax_scr, m_scr, acc_scr,
                *, tq, tk, q_off, l_valid, ksel, pos_bits):
    i = pl.program_id(1)
    q0 = q_off + i * tq
    qchunk = (q0 + lax.broadcasted_iota(I32, (tq, 1), 0)) // CHUNK
    n_end = jnp.minimum(((q0 + tq - 1) // CHUNK + 1) * CHUNK, l_valid)
    nblk = (n_end + tk - 1) // tk
    kcol = lax.broadcasted_iota(I32, (1, tk), 1)

    qi = qi_ref[...]
    w = w_ref[...]
    qi_stack = jnp.concatenate([qi[:, g * IDX_DIM:(g + 1) * IDX_DIM] for g in range(IDX_HEADS)], axis=0)
    ws = [jnp.broadcast_to(w[:, MISC_AW + g:MISC_AW + g + 1], (tq, LANES)) for g in range(IDX_HEADS)]

    def idx_body(j, carry):
        start = pl.multiple_of(j * tk, tk)
        s_all = _dot(qi_stack, ki_ref[0:IDX_DIM, pl.ds(start, tk)])
        idx = jnp.zeros((tq, tk), F32)
        for g in range(IDX_HEADS):
            idx = idx + jnp.maximum(s_all[g * tq:(g + 1) * tq], 0.0) * _lane_tile(ws[g], tk)
        bits = lax.bitcast_convert_type(idx, I32)
        key = jnp.where(bits < 0, bits ^ INT_MAX, bits)
        key = jnp.where(idx == 0.0, 0, key)
        kpos = start + kcol
        kchunk = jnp.where(kpos < l_valid, kpos // CHUNK, INT_MAX)
        key = jnp.where(kchunk <= qchunk, key, INT_MIN)
        key_scr[:, pl.ds(start, tk)] = key
        gmax_scr[...] = jnp.maximum(gmax_scr[...], key)
        return carry

    gmax_scr[...] = jnp.full((tq, tk), INT_MIN, I32)
    lax.fori_loop(0, nblk, idx_body, 0)

    @pl.when(nblk % 2 == 1)
    def _():
        key_scr[:, pl.ds(pl.multiple_of(nblk * tk, tk), tk)] = jnp.full((tq, tk), INT_MIN, I32)

    npair = (nblk + 1) // 2

    rc = min(tq, 128)

    lane_col = lax.broadcasted_iota(I32, (1, LANES), 1)

    def count(indicator, *row_args, src=key_scr, nsteps=npair, width=2 * tk, active=None):
        outs = []
        for ci, r0 in enumerate(range(0, tq, rc)):
            args = [a[r0:r0 + rc] for a in row_args]

            def body(j, acc, r0=r0, args=args):
                for c in range(width // LANES):
                    start = pl.multiple_of(j * width + c * LANES, LANES)
                    acc = acc + indicator(src[r0:r0 + rc, pl.ds(start, LANES)], start + lane_col, *args)
                return acc

            steps = nsteps if active is None else jnp.where(active[ci], nsteps, 0)
            acc = lax.fori_loop(0, steps, body, jnp.zeros((rc, LANES), F32))
            outs.append(jnp.broadcast_to(jnp.sum(acc, axis=1, keepdims=True), (rc, LANES)))
        return outs[0] if len(outs) == 1 else jnp.concatenate(outs, axis=0)

    ge = lambda kt, kp, c: jnp.where(kt >= c, 1.0, 0.0)
    n_adm = jnp.broadcast_to(jnp.minimum((qchunk + 1) * CHUNK, l_valid).astype(F32), (tq, LANES))

    def lb_body(t, ans):
        cand_u = ans | lax.shift_left(jnp.int32(1), 31 - t)
        cnt = count(ge, cand_u ^ INT_MIN, src=gmax_scr, nsteps=1, width=tk)
        return jnp.where(cnt >= ksel, cand_u, ans)

    lb = lax.fori_loop(0, 32, lb_body, jnp.zeros((tq, LANES), I32)) ^ INT_MIN
    gm = gmax_scr[...]
    gbits = jnp.where(gm < 0, gm ^ INT_MAX, gm)
    gval = jnp.where(gm == INT_MIN, -jnp.inf, lax.bitcast_convert_type(gbits, F32))
    vmax = jnp.broadcast_to(jnp.max(gval, axis=1, keepdims=True), (tq, LANES))
    mbits = lax.bitcast_convert_type(vmax, I32)
    kmax = jnp.where(mbits < 0, mbits ^ INT_MAX, mbits)
    lo0 = jnp.maximum(lb, INT_MIN + 1)
    hi0 = jnp.minimum(kmax, INT_MAX - 1) + 1

    def bis_cond(st):
        unsettled = st[4]
        for u in st[5:]:
            unsettled = jnp.logical_or(unsettled, u)
        return jnp.logical_and(st[0] < 34, unsettled)

    def unsettled_chunks(lo, hi, c_lo):
        open_rows = jnp.where(c_lo == ksel, 0.0, jnp.where(lo + 1 >= hi, 0.0, jnp.where(n_adm <= ksel, 0.0, 1.0)))
        return tuple(jnp.max(open_rows[r0:r0 + rc]) > 0.0 for r0 in range(0, tq, rc))

    def bis_body(st):
        t, lo, hi, c_lo = st[:4]
        for _ in range(2):
            mid = (lo >> 1) + (hi >> 1) + (lo & hi & 1)
            cnt = count(ge, mid, active=st[4:])
            take = cnt >= ksel
            lo = jnp.where(take, mid, lo)
            c_lo = jnp.where(take, cnt, c_lo)
            hi = jnp.where(take, hi, mid)
        return (t + 2, lo, hi, c_lo) + unsettled_chunks(lo, hi, c_lo)

    c_lo0 = count(ge, lo0)
    _, tau, _, c_ge = lax.while_loop(
        bis_cond, bis_body, (jnp.int32(0), lo0, hi0, c_lo0) + unsettled_chunks(lo0, hi0, c_lo0))[:4]
    any_tie = jnp.max(jnp.where(c_ge > ksel, 1.0, 0.0)) > 0.0

    tie_steps = jnp.where(any_tie, npair, 0)
    need = ksel - count(lambda kt, kp, c: jnp.where(kt > c, 1.0, 0.0), tau, nsteps=tie_steps)

    def pos_body(t, p):
        cand = p | lax.shift_left(jnp.int32(1), pos_bits - 1 - t)
        cnt = count(lambda kt, kp, tv, cv: jnp.where(kt == tv, jnp.where(kp < cv, 1.0, 0.0), 0.0), tau, cand)
        return jnp.where(cnt < need, cand, p)

    p_lim = lax.fori_loop(0, jnp.where(any_tie, pos_bits, 0), pos_body, jnp.zeros((tq, LANES), I32))
    p_lim = jnp.where(any_tie, p_lim, INT_MAX)

    _init_online(m_scr, acc_scr)
    q_stacks = _stack_maps(q_ref, 2)
    tau_t = _lane_tile(tau, tk)
    p_lim_t = _lane_tile(p_lim, tk)

    def att_body(j, carry):
        start = pl.multiple_of(j * tk, tk)
        kt = key_scr[:, pl.ds(start, tk)]
        bias = jnp.where(kt > tau_t, 0.0, jnp.where(kt == tau_t, jnp.where(start + kcol <= p_lim_t, 0.0, NEG), NEG))
        ss = _map_scores(q_stacks, k_ref, start, tk, tq)
        for h in range(HEADS):
            _online_update([h], [ss[h] + bias], v_ref[pl.ds(start, tk), h * LANES:(h + 1) * LANES], m_scr, acc_scr)
        return carry

    lax.fori_loop(0, nblk, att_body, 0)
    o_ref[...] = jnp.concatenate([_normalized(acc_scr[h]) for h in range(HEADS)], axis=1).astype(BF16)


def _dsa(qi, misc, q, ki, k, v, *, q_off, l_valid, tq):
    b, t, _ = q.shape
    lp = v.shape[1]
    tk = 512
    assert lp % (2 * tk) == 0 and t % tq == 0
    ksel = min(TOPK_MAX, l_valid // 4)
    qtok = lambda width: pl.BlockSpec((None, tq, width), lambda bi, i: (bi, i, 0))
    res = lambda width: pl.BlockSpec((None, lp, width), lambda bi, i: (bi, 0, 0), pipeline_mode=pl.Buffered(1))
    res_t = lambda rows: pl.BlockSpec((None, rows, lp), lambda bi, i: (bi, 0, 0), pipeline_mode=pl.Buffered(1))
    kern = functools.partial(_dsa_kernel, tq=tq, tk=tk, q_off=q_off, l_valid=l_valid, ksel=float(ksel),
                             pos_bits=int(lp - 1).bit_length())
    return pl.pallas_call(
        kern,
        grid=(b, t // tq),
        in_specs=[qtok(IDX_HEADS * IDX_DIM), qtok(MISC_W), qtok(HEADS * LANES), res_t(MISC_W), res_t(GROUP_W),
                  res(HEADS * LANES)],
        out_specs=qtok(GROUP_W),
        out_shape=jax.ShapeDtypeStruct((b, t, GROUP_W), BF16),
        scratch_shapes=[pltpu.VMEM((tq, lp), I32), pltpu.VMEM((tq, tk), I32)] + _attn_scratch(HEADS, tq),
        compiler_params=_cparams(2),
        name="dsa",
    )(qi, misc, q, ki, k, v)


def _fox_kernel(q_ref, cq_ref, k_ref, v_ref, ck_ref, o_ref, m_scr, acc_scr, *, tq, tk, q_off, l_valid):
    i = pl.program_id(1)
    q0 = q_off + i * tq
    qpos = q0 + lax.broadcasted_iota(I32, (tq, 1), 0)
    nblk = (jnp.minimum(q0 + tq, l_valid) + tk - 1) // tk
    nfull = jnp.minimum((q0 + 1) // tk, l_valid // tk)
    kcol = lax.broadcasted_iota(I32, (1, tk), 1)
    cq = cq_ref[...]
    cqs = [jnp.broadcast_to(cq[:, h:h + 1], (tq, LANES)) for h in range(HEADS)]
    _init_online(m_scr, acc_scr)
    q_stacks = _stack_maps(q_ref, 2)

    def body(j, carry, masked):
        start = pl.multiple_of(j * tk, tk)
        if masked:
            kpos = start + kcol
            valid = kpos <= qpos
        ss = _map_scores(q_stacks, k_ref, start, tk, tq)
        for h in range(HEADS):
            s = ss[h] - ck_ref[h:h + 1, pl.ds(start, tk)]
            if masked:
                s = jnp.where(valid, s, NEG)
            _online_update([h], [s], v_ref[pl.ds(start, tk), h * LANES:(h + 1) * LANES], m_scr, acc_scr,
                           shift=cqs[h])
        return carry

    lax.fori_loop(0, nfull, functools.partial(body, masked=False), 0)
    lax.fori_loop(nfull, nblk, functools.partial(body, masked=True), 0)
    o_ref[...] = jnp.concatenate([_normalized(acc_scr[h]) for h in range(HEADS)], axis=1).astype(BF16)


def _fox(q, cq, k, v, ck_t, *, q_off, l_valid, tq):
    b, t, _ = q.shape
    lp = v.shape[1]
    tk = 512
    assert lp % tk == 0 and t % tq == 0
    qtok = lambda width: pl.BlockSpec((None, tq, width), lambda bi, i: (bi, i, 0))
    res = lambda rows, width: pl.BlockSpec((None, rows, width), lambda bi, i: (bi, 0, 0),
                                           pipeline_mode=pl.Buffered(1))
    kern = functools.partial(_fox_kernel, tq=tq, tk=tk, q_off=q_off, l_valid=l_valid)
    return pl.pallas_call(
        kern,
        grid=(b, t // tq),
        in_specs=[qtok(HEADS * LANES), qtok(HEADS), res(GROUP_W, lp), res(lp, HEADS * LANES), res(HEADS, lp)],
        out_specs=qtok(GROUP_W),
        out_shape=jax.ShapeDtypeStruct((b, t, GROUP_W), BF16),
        scratch_shapes=_attn_scratch(HEADS, tq),
        compiler_params=_cparams(2),
        name="fox",
    )(q, cq, k, v, ck_t)


def _diff_kernel(q_ref, k_ref, v_ref, lam_ref, gsub_ref, o_ref, m_scr, acc_scr,
                 *, tq, tk, q_off, l_valid, out_scale):
    i = pl.program_id(1)
    q0 = q_off + i * tq
    qchunk = (q0 + lax.broadcasted_iota(I32, (tq, 1), 0)) // CHUNK
    n_end = jnp.minimum(((q0 + tq - 1) // CHUNK + 1) * CHUNK, l_valid)
    nblk = (n_end + tk - 1) // tk
    nfull = jnp.minimum(((q0 // CHUNK + 1) * CHUNK) // tk, l_valid // tk)
    kcol = lax.broadcasted_iota(I32, (1, tk), 1)
    _init_online(m_scr, acc_scr)
    q_stacks = _stack_maps(q_ref, 4)

    def body(j, carry, masked):
        start = pl.multiple_of(j * tk, tk)
        if masked:
            kpos = start + kcol
            valid = jnp.where(kpos < l_valid, kpos // CHUNK, INT_MAX) <= qchunk
        ss = _map_scores(q_stacks, k_ref, start, tk, tq)
        if masked:
            ss = [jnp.where(valid, s, NEG) for s in ss]
        for h in range(HEADS):
            _online_update([2 * h, 2 * h + 1], ss[2 * h:2 * h + 2], v_ref[pl.ds(start, tk), h * LANES:(h + 1) * LANES],
                           m_scr, acc_scr)
        return carry

    lax.fori_loop(0, nfull, functools.partial(body, masked=False), 0)
    lax.fori_loop(nfull, nblk, functools.partial(body, masked=True), 0)

    lv = lam_ref[...]
    lam = (jnp.exp(jnp.sum(lv[0:1] * lv[1:2], axis=1, keepdims=True))
           - jnp.exp(jnp.sum(lv[2:3] * lv[3:4], axis=1, keepdims=True)) + (1.0 - out_scale))
    outs = []
    for h in range(HEADS):
        o = _normalized(acc_scr[2 * h]) - lam * _normalized(acc_scr[2 * h + 1])
        outs.append(_rms(o, gsub_ref[...]) * out_scale)
    o_ref[...] = jnp.concatenate(outs, axis=1).astype(BF16)


def _diff(q, k, v, lam_vecs, g_sub, *, q_off, l_valid, tq, lam_init):
    b, t, _ = q.shape
    lp = v.shape[1]
    tk = 512
    assert lp % tk == 0 and t % tq == 0
    qtok = lambda width: pl.BlockSpec((None, tq, width), lambda bi, i: (bi, i, 0))
    res = lambda rows, width: pl.BlockSpec((None, rows, width), lambda bi, i: (bi, 0, 0),
                                           pipeline_mode=pl.Buffered(1))
    const = lambda shape: pl.BlockSpec(shape, lambda bi, i: (0, 0))
    kern = functools.partial(_diff_kernel, tq=tq, tk=tk, q_off=q_off, l_valid=l_valid, out_scale=1.0 - lam_init)
    return pl.pallas_call(
        kern,
        grid=(b, t // tq),
        in_specs=[qtok(2 * HEADS * LANES), res(GROUP_W, lp), res(lp, HEADS * LANES), const((4, DIFF_DIM)),
                  const((1, HEAD_DIM))],
        out_specs=qtok(GROUP_W),
        out_shape=jax.ShapeDtypeStruct((b, t, GROUP_W), BF16),
        scratch_shapes=_attn_scratch(2 * HEADS, tq),
        compiler_params=_cparams(2),
        name="diff",
    )(q, k, v, lam_vecs, g_sub.reshape(1, HEAD_DIM))


def _relbias_kernel(rb_ref, o_ref):
    qi = lax.broadcasted_iota(I32, (CHUNK, BAND_KEYS), 0)
    kj = lax.broadcasted_iota(I32, (CHUNK, BAND_KEYS), 1)
    idx = jnp.clip(BAND_PREV + qi - kj, -REL_CLIP, REL_CLIP) + REL_CLIP

    def body(r, accs):
        return tuple(jnp.where(idx == r, rb_ref[h, r], accs[h]) for h in range(HEADS))

    accs = lax.fori_loop(0, 2 * REL_CLIP + 1, body, tuple(jnp.zeros((CHUNK, BAND_KEYS), F32) for _ in range(HEADS)))
    for h in range(HEADS):
        o_ref[h] = accs[h]


def _relbias(rel_bias_l):
    return pl.pallas_call(
        _relbias_kernel,
        in_specs=[pl.BlockSpec(memory_space=pltpu.SMEM)],
        out_specs=pl.BlockSpec(memory_space=pltpu.VMEM),
        out_shape=jax.ShapeDtypeStruct((HEADS, CHUNK, BAND_KEYS), F32),
        name="relbias",
    )(rel_bias_l.astype(F32))


def _band_kernel(q_ref, kp_ref, kc_ref, vp_ref, vc_ref, bias_ref, o_ref, k_scr, v_scr, *, tq, first_has_no_prev):
    i = pl.program_id(1)
    k_scr[0:BAND_PREV, :] = kp_ref[...]
    k_scr[BAND_PREV:BAND_PREV + tq, :] = kc_ref[...]
    v_scr[0:BAND_PREV, :] = vp_ref[...]
    v_scr[BAND_PREV:BAND_PREV + tq, :] = vc_ref[...]
    kcol = lax.broadcasted_iota(I32, (1, BAND_KEYS), 1)
    first = jnp.where(i == 0, BAND_PREV, 0) if first_has_no_prev else 0
    for c in range(tq // CHUNK):
        r0 = c * CHUNK
        valid = (r0 + kcol) >= first
        outs = []
        for h in range(HEADS):
            hs = slice(h * HEAD_DIM, (h + 1) * HEAD_DIM)
            s = _dot_nt(q_ref[r0:r0 + CHUNK, hs], k_scr[r0:r0 + BAND_KEYS, hs]) + bias_ref[h]
            s = jnp.where(valid, s, NEG)
            e = jnp.exp(s - jnp.max(s, axis=1, keepdims=True))
            o = _dot(e.astype(BF16), v_scr[r0:r0 + BAND_KEYS, hs])
            outs.append(o / jnp.sum(e, axis=1, keepdims=True))
        o_ref[r0:r0 + CHUNK, :] = jnp.concatenate(outs, axis=1).astype(BF16)


def _band(q, k_prev, k_cur, v_prev, v_cur, bias, *, tq, prev_from_cur):
    b, t, _ = q.shape
    assert t % tq == 0 and (not prev_from_cur or tq == BAND_PREV)
    qtok = pl.BlockSpec((None, tq, GROUP_W), lambda bi, i: (bi, i, 0))
    if prev_from_cur:
        prev = pl.BlockSpec((None, BAND_PREV, GROUP_W), lambda bi, i: (bi, jnp.maximum(i - 1, 0), 0))
    else:
        prev = pl.BlockSpec((None, BAND_PREV, GROUP_W), lambda bi, i: (bi, 0, 0))
    bspec = pl.BlockSpec((HEADS, CHUNK, BAND_KEYS), lambda bi, i: (0, 0, 0))
    kern = functools.partial(_band_kernel, tq=tq, first_has_no_prev=prev_from_cur)
    return pl.pallas_call(
        kern,
        grid=(b, t // tq),
        in_specs=[qtok, prev, qtok, prev, qtok, bspec],
        out_specs=qtok,
        out_shape=jax.ShapeDtypeStruct((b, t, GROUP_W), BF16),
        scratch_shapes=[pltpu.VMEM((BAND_PREV + tq, GROUP_W), BF16), pltpu.VMEM((BAND_PREV + tq, GROUP_W), BF16)],
        compiler_params=_cparams(2),
        name="band",
    )(q, k_prev, k_cur, v_prev, v_cur, bias)


def _memkv_kernel(m_ref, g_ref, wk_ref, wv_ref, kf_o, vf_o, kb_o, vb_o):
    hb = _rms(m_ref[...], g_ref[...]).astype(BF16)
    k = _dot(hb, wk_ref[...])
    v = _dot(hb, wv_ref[...])
    kf_o[...] = k
    vf_o[...] = v
    kb_o[...] = k.astype(BF16)
    vb_o[...] = v.astype(BF16)


def _memkv(mem, g, wk, wv):
    b, m, d = mem.shape
    xw = wk.shape[1]
    tok = lambda width: pl.BlockSpec((None, m, width), lambda bi: (bi, 0, 0))
    const = lambda shape: pl.BlockSpec(shape, lambda bi: (0, 0))
    return pl.pallas_call(
        _memkv_kernel,
        grid=(b,),
        in_specs=[tok(d), const((1, d)), const((d, xw)), const((d, xw))],
        out_specs=[tok(xw)] * 4,
        out_shape=[jax.ShapeDtypeStruct((b, m, xw), F32)] * 2 + [jax.ShapeDtypeStruct((b, m, xw), BF16)] * 2,
        compiler_params=_cparams(1),
        name="memkv",
    )(mem, g.reshape(1, d), wk, wv)


def _mix_cross_kernel(x_ref, oa_ref, ob_ref, oc_ref, od_ref, wout_ref, g_ref, wq_ref, mk_ref, mv_ref, wo_ref, o_ref):
    x = x_ref[...]
    for n, oref in enumerate((oa_ref, ob_ref, oc_ref, od_ref)):
        x = x + _dot(oref[...], wout_ref[n * GROUP_W:(n + 1) * GROUP_W, :])
    hb = _rms(x, g_ref[...]).astype(BF16)
    q = _dot(hb, wq_ref[...])
    outs = []
    for h in range(X_HEADS):
        hs = slice(h * X_HEAD_DIM, (h + 1) * X_HEAD_DIM)
        s = _dot_nt(q[:, hs].astype(BF16), mk_ref[:, hs]) * (X_HEAD_DIM ** -0.5)
        e = jnp.exp(s - jnp.max(s, axis=1, keepdims=True))
        p = e / jnp.sum(e, axis=1, keepdims=True)
        outs.append(_dot(p.astype(BF16), mv_ref[:, hs]))
    o = jnp.concatenate(outs, axis=1).astype(BF16)
    o_ref[...] = x + _dot(o, wo_ref[...])


def _mix_cross(x, oa, ob, oc, od, w_out, g, wq, mk, mv, wo):
    b, t, d = x.shape
    tm = min(512, t)
    m, xw = mk.shape[1], mk.shape[2]
    tok = lambda width: pl.BlockSpec((None, tm, width), lambda bi, i: (bi, i, 0))
    const = lambda shape: pl.BlockSpec(shape, lambda bi, i: (0, 0))
    mem = pl.BlockSpec((None, m, xw), lambda bi, i: (bi, 0, 0))
    return pl.pallas_call(
        _mix_cross_kernel,
        grid=(b, t // tm),
        in_specs=[tok(d)] + [tok(GROUP_W)] * 4 + [const(w_out.shape), const((1, d)), const(wq.shape), mem, mem,
                                                   const(wo.shape)],
        out_specs=tok(d),
        out_shape=jax.ShapeDtypeStruct((b, t, d), F32),
        compiler_params=_cparams(2),
        name="mix_cross",
    )(x, oa, ob, oc, od, w_out, g.reshape(1, d), wq, mk, mv, wo)


def _ffn_kernel(x_ref, g_ref, prev_ref, wup_ref, cw_ref, cb_ref, wdn_ref, o_ref, conv_ref, halo_scr, acc_scr,
                *, tm, d_ff, fc):
    i = pl.program_id(1)

    @pl.when(i == 0)
    def _():
        halo_scr[...] = prev_ref[...]

    x = x_ref[...]
    hb = _rms(x, g_ref[...]).astype(BF16)
    row = lax.broadcasted_iota(I32, (tm, fc), 0)
    for c in range(d_ff // fc):
        cs = slice(c * fc, (c + 1) * fc)
        g = _dot(hb, wup_ref[:, cs])
        u = _dot(hb, wup_ref[:, d_ff + c * fc:d_ff + (c + 1) * fc])
        h0 = halo_scr[0:1, cs]
        h1 = halo_scr[1:2, cs]
        gm1 = jnp.where(row == 0, h1, pltpu.roll(g, 1, 0))
        gm2 = jnp.where(row == 0, h0, jnp.where(row == 1, h1, pltpu.roll(g, 2, 0)))
        a = cb_ref[:, cs] + gm2 * cw_ref[0:1, cs]
        a = a + gm1 * cw_ref[1:2, cs]
        a = a + g * cw_ref[2:3, cs]
        y = (a * (1.0 / (1.0 + jnp.exp(-a)))) * u
        f = _dot(y.astype(BF16), wdn_ref[cs, :])
        if c == 0:
            acc_scr[...] = f
        else:
            acc_scr[...] += f
        tail = g[tm - (CONV_W - 1):tm, :]
        halo_scr[:, cs] = tail
        conv_ref[:, cs] = tail
    o_ref[...] = x + acc_scr[...]


def _ffn(x, g, prev, w_up, conv_w, conv_b, w_down):
    b, t, d = x.shape
    d_ff = w_down.shape[0]
    tm = min(512, t)
    fc = 256
    assert d_ff % fc == 0 and t % tm == 0 and tm >= CONV_W - 1
    tok = pl.BlockSpec((None, tm, d), lambda bi, i: (bi, i, 0))
    const = lambda shape: pl.BlockSpec(shape, lambda bi, i: (0, 0))
    st = pl.BlockSpec((None, CONV_W - 1, d_ff), lambda bi, i: (bi, 0, 0))
    kern = functools.partial(_ffn_kernel, tm=tm, d_ff=d_ff, fc=fc)
    return pl.pallas_call(
        kern,
        grid=(b, t // tm),
        in_specs=[tok, const((1, d)), st, const(w_up.shape), const(conv_w.shape), const((1, d_ff)),
                  const(w_down.shape)],
        out_specs=[tok, st],
        out_shape=[jax.ShapeDtypeStruct((b, t, d), F32), jax.ShapeDtypeStruct((b, CONV_W - 1, d_ff), F32)],
        scratch_shapes=[pltpu.VMEM((CONV_W - 1, d_ff), F32), pltpu.VMEM((tm, d), F32)],
        compiler_params=_cparams(2),
        name="ffn",
    )(x, g.reshape(1, d), prev, w_up, conv_w, conv_b.reshape(1, d_ff), w_down)


def _rmsnorm_kernel(x_ref, g_ref, o_ref):
    o_ref[...] = _rms(x_ref[...], g_ref[...])


def _rmsnorm(x, g):
    b, t, d = x.shape
    tm = min(1024, t)
    tok = pl.BlockSpec((None, tm, d), lambda bi, i: (bi, i, 0))
    return pl.pallas_call(
        _rmsnorm_kernel,
        grid=(b, t // tm),
        in_specs=[tok, pl.BlockSpec((1, d), lambda bi, i: (0, 0))],
        out_specs=tok,
        out_shape=jax.ShapeDtypeStruct((b, t, d), F32),
        compiler_params=_cparams(2),
        name="final_norm",
    )(x, g.reshape(1, d))


def _pad_rows(a, rows):
    return a if a.shape[1] == rows else jnp.pad(a, ((0, 0), (0, rows - a.shape[1]), (0, 0)))


def _round_up(n, m):
    return (n + m - 1) // m * m


def _value_slabs_host(v):
    b, l = v.shape[:2]
    one = jnp.zeros((b, l, HEADS, HEAD_DIM), BF16).at[..., 0].set(1.0)
    return jnp.concatenate([v.astype(BF16), one], axis=-1).reshape(b, l, HEADS * LANES)


def _mixers(pr, keys, *, q_off, l_valid, tq_a, tq_bd, band_args, band_tq, prev_from_cur, lam_vecs, g_sub,
            lam_init, logf_all):
    lp = keys["av"].shape[1]
    oa = _dsa(pr["aqi"], pr["misc"], pr["aq"], keys["aki"], keys["ak"], keys["av"],
              q_off=q_off, l_valid=l_valid, tq=tq_a)
    lc = _round_up(logf_all.shape[1], 1024)
    c_t = _cumsum_t(jnp.swapaxes(_pad_rows(logf_all, lc), 1, 2))
    t = pr["bq"].shape[1]
    cq = jnp.swapaxes(c_t[:, :, q_off:q_off + t], 1, 2)
    ck_t = c_t[:, :, :lp] if lc >= lp else jnp.pad(c_t, ((0, 0), (0, 0), (0, lp - lc)))
    ob = _fox(pr["bq"], cq, keys["bk"], keys["bv"], ck_t, q_off=q_off, l_valid=l_valid, tq=tq_bd)
    oc = _band(pr["cq"], *band_args, tq=band_tq, prev_from_cur=prev_from_cur)
    od = _diff(pr["dq"], keys["dk"], keys["dv"], lam_vecs, g_sub, q_off=q_off, l_valid=l_valid, tq=tq_bd,
               lam_init=lam_init)
    return oa, ob, oc, od


def kernel(x_prompt, x_sample, mem_prompt, cache_a_k, cache_a_v, cache_a_kidx, cache_b_k, cache_b_v,
           cache_b_logf, cache_c_k, cache_c_v, cache_d_k, cache_d_v, cache_mem_k, cache_mem_v,
           state_ffn_conv, g_mix, w_in, b_forget, rel_bias, lam_q1, lam_k1, lam_q2, lam_k2, g_sub, w_out,
           g_cross, g_mem, w_xq, w_xk, w_xv, w_xo, g_ffn, w_up, conv_w, conv_b, w_down, g_final):
    depth = w_in.shape[0]
    bp, s_len, d_model = x_prompt.shape
    bs, t_len, _ = x_sample.shape
    past = cache_a_k.shape[2]
    d_ff = w_down.shape[1]
    c_keep = min(BAND_PREV, s_len)
    l_s = past + t_len
    lp_s = _round_up(l_s, 1024)

    pos_p = jnp.arange(s_len, dtype=jnp.int32)
    pos_s = jnp.tile(past + jnp.arange(t_len, dtype=jnp.int32), bs)
    h64 = HEAD_DIM // ROT_FRACTION // 2
    h32 = DIFF_DIM // ROT_FRACTION // 2
    t64_p, t32_p = _rope_table(pos_p, h64, HEAD_DIM), _rope_table(pos_p, h32, DIFF_DIM)
    t64_s, t32_s = _rope_table(pos_s, h64, HEAD_DIM), _rope_table(pos_s, h32, DIFF_DIM)

    flat = lambda a: a.reshape(a.shape[0], a.shape[1], -1)
    heads = lambda a, hd=HEAD_DIM: a.reshape(a.shape[0], a.shape[1], a.shape[2] // hd, hd)

    xp, xs = x_prompt, x_sample
    p_states, s_states = [], []
    for l in range(depth):
        lam_init = 0.8 - 0.6 * math.exp(-0.3 * l)
        w_pack, bfp = _pack_w_in(w_in[l], b_forget[l])
        lam_vecs = jnp.stack([lam_q1[l], lam_k1[l], lam_q2[l], lam_k2[l]]).astype(F32)
        bias = _relbias(rel_bias[l])
        w_out_b, wq_b, wo_b = w_out[l].astype(BF16), w_xq[l].astype(BF16), w_xo[l].astype(BF16)
        w_up_b, w_dn_b = w_up[l].astype(BF16), w_down[l].astype(BF16)

        pr = _proj(xp, g_mix[l], w_pack, bfp, t64_p, t32_p, c_keep)
        logf = pr["misc"][:, :, MISC_BF:MISC_BF + HEADS]
        oa, ob, oc, od = _mixers(
            pr, pr, q_off=0, l_valid=s_len, tq_a=256, tq_bd=512,
            band_args=(pr["ck"], pr["ck"], pr["cv"], pr["cv"], bias), band_tq=BAND_PREV, prev_from_cur=True,
            lam_vecs=lam_vecs, g_sub=g_sub[l], lam_init=lam_init, logf_all=logf)
        mkf, mvf, mkb, mvb = _memkv(mem_prompt, g_mem[l], w_xk[l].astype(BF16), w_xv[l].astype(BF16))
        xp = _mix_cross(xp, oa, ob, oc, od, w_out_b, g_cross[l], wq_b, mkb, mvb, wo_b)
        xp, conv_p = _ffn(xp, g_ffn[l], jnp.zeros((bp, CONV_W - 1, d_ff), F32), w_up_b, conv_w[l], conv_b[l], w_dn_b)
        p_states.append((heads(pr["akf"]), heads(pr["avf"]), pr["misc"][:, :, :IDX_DIM],
                         heads(pr["bkf"]), heads(pr["bvf"]), logf,
                         heads(pr["ckf"]), heads(pr["cvf"]), heads(pr["dkf"]), heads(pr["dvf"]),
                         conv_p, heads(mkf, X_HEAD_DIM), heads(mvf, X_HEAD_DIM)))

        prs = _proj(xs.reshape(1, bs * t_len, d_model), g_mix[l], w_pack, bfp, t64_s, t32_s, bs * t_len)
        prs = {n: a.reshape(bs, t_len, a.shape[-1]) for n, a in prs.items() if n not in ("ak", "aki", "bk", "dk")}
        cat_t = lambda cache, new: jnp.swapaxes(
            _pad_rows(jnp.concatenate([flat(cache), new], axis=1), lp_s).astype(BF16), 1, 2)
        kidx = jnp.concatenate([cache_a_kidx[l], prs["misc"][:, :, :IDX_DIM]], axis=1)
        kidx = jnp.pad(kidx, ((0, 0), (0, lp_s - l_s), (0, MISC_W - IDX_DIM)))
        catv = lambda cache, new: _pad_rows(jnp.concatenate([_value_slabs_host(cache), new], axis=1), lp_s)
        keys = {"aki": jnp.swapaxes(kidx.astype(BF16), 1, 2),
                "ak": cat_t(cache_a_k[l], prs["akf"]), "av": catv(cache_a_v[l], prs["av"]),
                "bk": cat_t(cache_b_k[l], prs["bkf"]), "bv": catv(cache_b_v[l], prs["bv"]),
                "dk": cat_t(cache_d_k[l], prs["dkf"]), "dv": catv(cache_d_v[l], prs["dv"])}
        logf_s = prs["misc"][:, :, MISC_BF:MISC_BF + HEADS]
        logf_all = jnp.concatenate([cache_b_logf[l].astype(F32), logf_s], axis=1)
        oa, ob, oc, od = _mixers(
            prs, keys, q_off=past, l_valid=l_s, tq_a=t_len, tq_bd=t_len,
            band_args=(flat(cache_c_k[l]).astype(BF16), prs["ck"], flat(cache_c_v[l]).astype(BF16), prs["cv"], bias),
            band_tq=t_len, prev_from_cur=False,
            lam_vecs=lam_vecs, g_sub=g_sub[l], lam_init=lam_init, logf_all=logf_all)
        xs = _mix_cross(xs, oa, ob, oc, od, w_out_b, g_cross[l], wq_b,
                        flat(cache_mem_k[l]).astype(BF16), flat(cache_mem_v[l]).astype(BF16), wo_b)
        xs, conv_s = _ffn(xs, g_ffn[l], state_ffn_conv[l], w_up_b, conv_w[l], conv_b[l], w_dn_b)
        s_states.append((heads(prs["akf"]), heads(prs["avf"]), prs["misc"][:, :, :IDX_DIM],
                         heads(prs["bkf"]), heads(prs["bvf"]), logf_s,
                         heads(prs["ckf"]), heads(prs["cvf"]), heads(prs["dkf"]), heads(prs["dvf"]), conv_s))

    y_prompt = _rmsnorm(xp, g_final)
    y_sample = _rmsnorm(xs, g_final)
    p_out = [jnp.stack(z, axis=0) for z in zip(*p_states)]
    s_out = [jnp.stack(z, axis=0) for z in zip(*s_states)]
    return (y_prompt, y_sample, *p_out, *s_out)
```

```python
import functools
import math

import numpy as np
import jax
import jax.numpy as jnp
from jax import lax
from jax.experimental import pallas as pl
from jax.experimental.pallas import tpu as pltpu

F32 = jnp.float32
BF16 = jnp.bfloat16
I32 = jnp.int32

CHUNK = 64
HEAD_DIM = 64
HEADS = 4
GROUP_W = HEADS * HEAD_DIM
ROT_FRACTION = 4
ROPE_THETA = 500000.0
IDX_HEADS = 8
IDX_DIM = 64
TOPK_MAX = 256
BAND_CHUNKS = 8
BAND_KEYS = (BAND_CHUNKS + 1) * CHUNK
BAND_PREV = BAND_CHUNKS * CHUNK
REL_CLIP = 128
DIFF_DIM = HEAD_DIM // 2
X_HEADS = 4
X_HEAD_DIM = 128
CONV_W = 3
EPS = 1e-6

LANES = 128
VMEM_LIMIT_BYTES = 56 * 1024 * 1024

NEG = -1e30
INT_MIN = -2147483648
INT_MAX = 2147483647

MISC_W = LANES
MISC_AW = IDX_DIM
MISC_BF = IDX_DIM + IDX_HEADS
PROJ_COLS = 3 * GROUP_W + IDX_HEADS * IDX_DIM + MISC_W + 9 * GROUP_W


def _cparams(n_axes):
    return pltpu.CompilerParams(dimension_semantics=("arbitrary",) * n_axes,
                                vmem_limit_bytes=VMEM_LIMIT_BYTES)


def _rms(x, g):
    return (x * lax.rsqrt(jnp.mean(x * x, axis=-1, keepdims=True) + EPS)) * g


def _dot(a, b):
    return jnp.dot(a, b, preferred_element_type=F32)


def _dot_nt(a, b):
    return lax.dot_general(a, b, (((1,), (1,)), ((), ())), preferred_element_type=F32)


def _rope(z, tab, half):
    w = z.shape[1]
    reps = w // LANES
    rep = lambda t: t if reps == 1 else jnp.concatenate([t] * reps, axis=1)
    c = rep(tab[:, 0:LANES])
    s1 = rep(tab[:, LANES:2 * LANES])
    s2 = rep(tab[:, 2 * LANES:3 * LANES])
    return z * c + pltpu.roll(z, w - half, 1) * s1 + pltpu.roll(z, half, 1) * s2


def _slab_masked(z, widths):
    lane = lax.broadcasted_iota(I32, (z.shape[0], LANES), 1)
    outs = []
    for j in range(GROUP_W // widths):
        lo = (j * widths) % LANES
        zs = z[:, (j * widths) // LANES * LANES:((j * widths) // LANES + 1) * LANES]
        outs.append(jnp.where(lane < lo, 0.0, jnp.where(lane < lo + widths, zs, 0.0)))
    return jnp.concatenate(outs, axis=1)


def _value_slabs(z):
    one = jnp.where(lax.broadcasted_iota(I32, (z.shape[0], HEAD_DIM), 1) == 0, 1.0, 0.0)
    return jnp.concatenate([t for h in range(HEADS) for t in (z[:, h * HEAD_DIM:(h + 1) * HEAD_DIM], one)], axis=1)


def _proj_kernel(x_ref, g_ref, w_ref, bf_ref, t64_ref, t32_ref,
                 aq_o, ak_o, av_o, aqi_o, aki_o, bq_o, bk_o, bv_o, cq_o, ck_o, cv_o, dq_o, dk_o, dv_o,
                 akf_o, avf_o, misc_o, bkf_o, bvf_o, ckf_o, cvf_o, dkf_o, dvf_o):
    hb = _rms(x_ref[...], g_ref[...]).astype(BF16)
    t64 = t64_ref[...]
    t32 = t32_ref[...]
    h64 = HEAD_DIM // ROT_FRACTION // 2
    h32 = DIFF_DIM // ROT_FRACTION // 2
    qscale = HEAD_DIM ** -0.5
    col = [0]

    def mm(width):
        c0 = col[0]
        col[0] = c0 + width
        return _dot(hb, w_ref[:, c0:c0 + width])

    z = _rope(mm(GROUP_W), t64, h64)
    aq_o[...] = _slab_masked(z * qscale, HEAD_DIM).astype(BF16)
    z = _rope(mm(GROUP_W), t64, h64)
    akf_o[...] = z
    ak_o[...] = z.T.astype(BF16)
    z = mm(GROUP_W)
    avf_o[...] = z
    av_o[...] = _value_slabs(z).astype(BF16)
    z = _rope(mm(IDX_HEADS * IDX_DIM), t64, h64)
    aqi_o[...] = (z * (IDX_DIM ** -0.5)).astype(BF16)
    z = mm(MISC_W)
    r = _rope(z, t64, h64)
    lane = lax.broadcasted_iota(I32, z.shape, 1)
    zf = z + bf_ref[...]
    logsig = jnp.minimum(zf, 0.0) - jnp.log1p(jnp.exp(-jnp.abs(zf)))
    misc_o[...] = jnp.where(lane < MISC_AW, r, jnp.where(lane < MISC_BF, z * (IDX_HEADS ** -0.5), logsig))
    aki_o[...] = jnp.where(lane < MISC_AW, r, 0.0).T.astype(BF16)
    z = mm(GROUP_W)
    bq_o[...] = _slab_masked(z * qscale, HEAD_DIM).astype(BF16)
    z = mm(GROUP_W)
    bkf_o[...] = z
    bk_o[...] = z.T.astype(BF16)
    z = mm(GROUP_W)
    bvf_o[...] = z
    bv_o[...] = _value_slabs(z).astype(BF16)
    z = mm(GROUP_W)
    cq_o[...] = (z * qscale).astype(BF16)
    z = mm(GROUP_W)
    ckf_o[...] = z
    ck_o[...] = z.astype(BF16)
    z = mm(GROUP_W)
    cvf_o[...] = z
    cv_o[...] = z.astype(BF16)
    z = _rope(mm(GROUP_W), t32, h32)
    dq_o[...] = _slab_masked(z * (DIFF_DIM ** -0.5), DIFF_DIM).astype(BF16)
    z = _rope(mm(GROUP_W), t32, h32)
    dkf_o[...] = z
    dk_o[...] = z.T.astype(BF16)
    z = mm(GROUP_W)
    dvf_o[...] = z
    dv_o[...] = _value_slabs(z).astype(BF16)


def _rope_table(pos, half, period):
    inv = ROPE_THETA ** (-jnp.arange(half, dtype=F32) / half)
    ang = pos.astype(F32)[:, None] * inv[None, :]
    cos, sin = jnp.cos(ang), jnp.sin(ang)
    t = pos.shape[0]
    rest = period - 2 * half
    zh = jnp.zeros((t, half), F32)
    zr = jnp.zeros((t, rest), F32)
    c = jnp.concatenate([cos, cos, jnp.ones((t, rest), F32)], axis=1)
    s1 = jnp.concatenate([-sin, zh, zr], axis=1)
    s2 = jnp.concatenate([zh, sin, zr], axis=1)
    rep = lambda a: jnp.tile(a, (1, LANES // period))
    return jnp.concatenate([rep(c), rep(s1), rep(s2)], axis=1)


def _pack_w_in(w_in_l, b_forget_l):
    sizes = (GROUP_W, GROUP_W, GROUP_W, IDX_HEADS * IDX_DIM, IDX_DIM, IDX_HEADS,
             GROUP_W, GROUP_W, GROUP_W, HEADS, GROUP_W, GROUP_W, GROUP_W, GROUP_W, GROUP_W, GROUP_W)
    pts = [int(p) for p in np.cumsum(sizes)[:-1]]
    (aq, ak, av, aqi, aki, aw, bq, bk, bv, bf, cq, ck, cv, dq, dk, dv) = jnp.split(w_in_l, pts, axis=1)
    d = w_in_l.shape[0]
    pad = jnp.zeros((d, MISC_W - IDX_DIM - IDX_HEADS - HEADS), w_in_l.dtype)
    misc = jnp.concatenate([aki, aw, bf, pad], axis=1)
    w = jnp.concatenate([aq, ak, av, aqi, misc, bq, bk, bv, cq, ck, cv, dq, dk, dv], axis=1).astype(BF16)
    bfp = jnp.zeros((1, MISC_W), F32).at[0, MISC_BF:MISC_BF + HEADS].set(b_forget_l.astype(F32))
    return w, bfp


def _proj(x, g, w, bfp, t64, t32, c_keep):
    b, t, d = x.shape
    tm = min(512, t)
    nt = t // tm
    nkeep = c_keep // tm
    tok = lambda width: pl.BlockSpec((None, tm, width), lambda bi, i: (bi, i, 0))
    ctok = pl.BlockSpec((None, tm, GROUP_W), lambda bi, i: (bi, jnp.maximum(i - (nt - nkeep), 0), 0))
    const = lambda shape: pl.BlockSpec(shape, lambda bi, i: (0, 0))
    tab = pl.BlockSpec((tm, 3 * LANES), lambda bi, i: (i, 0))
    sds = lambda width, dt, rows=t: jax.ShapeDtypeStruct((b, rows, width), dt)
    bf_names = ["aq", "ak", "av", "aqi", "aki", "bq", "bk", "bv", "cq", "ck", "cv", "dq", "dk", "dv"]
    qx_w, vx_w = HEADS * LANES, HEADS * LANES
    bf_w = [qx_w, GROUP_W, vx_w, IDX_HEADS * IDX_DIM, MISC_W, qx_w, GROUP_W, vx_w, GROUP_W, GROUP_W, GROUP_W,
            2 * qx_w, GROUP_W, vx_w]
    f_names = ["akf", "avf", "misc", "bkf", "bvf", "ckf", "cvf", "dkf", "dvf"]
    f_w = [GROUP_W, GROUP_W, MISC_W] + [GROUP_W] * 6
    out_shape = [sds(wd, BF16) for wd in bf_w]
    out_specs = [tok(wd) for wd in bf_w]
    for n, wd in zip(bf_names, bf_w):
        if n in ("ak", "aki", "bk", "dk"):
            out_shape[bf_names.index(n)] = jax.ShapeDtypeStruct((b, wd, t), BF16)
            out_specs[bf_names.index(n)] = pl.BlockSpec((None, wd, tm), lambda bi, i: (bi, 0, i))
    for n, wd in zip(f_names, f_w):
        if n in ("ckf", "cvf"):
            out_shape.append(sds(wd, F32, c_keep))
            out_specs.append(ctok)
        else:
            out_shape.append(sds(wd, F32))
            out_specs.append(tok(wd))
    outs = pl.pallas_call(
        _proj_kernel,
        grid=(b, nt),
        in_specs=[tok(d), const((1, d)), const((d, PROJ_COLS)), const((1, MISC_W)), tab, tab],
        out_specs=out_specs,
        out_shape=out_shape,
        compiler_params=_cparams(2),
        name="proj",
    )(x, g.reshape(1, d), w, bfp, t64, t32)
    return dict(zip(bf_names + f_names, outs))


def _cumsum_kernel(x_ref, o_ref, *, nb):
    x = x_ref[...]
    lane = lax.broadcasted_iota(I32, x.shape, 1)
    d = 1
    while d < LANES:
        x = x + jnp.where(lane >= d, pltpu.roll(x, d, 1), 0.0)
        d *= 2
    row = lax.broadcasted_iota(I32, x.shape, 0) % nb
    tot = jnp.broadcast_to(x[:, LANES - 1:LANES], x.shape)
    exc = jnp.where(row >= 1, pltpu.roll(tot, 1, 0), 0.0)
    d = 1
    while d < nb:
        exc = exc + jnp.where(row >= d, pltpu.roll(exc, d, 0), 0.0)
        d *= 2
    o_ref[...] = x + exc


def _cumsum_t(logf_t):
    b, h, l = logf_t.shape
    nb = l // LANES
    x = logf_t.reshape(b, h * nb, LANES)
    spec = pl.BlockSpec((None, h * nb, LANES), lambda bi: (bi, 0, 0))
    out = pl.pallas_call(
        functools.partial(_cumsum_kernel, nb=nb),
        grid=(b,),
        in_specs=[spec],
        out_specs=spec,
        out_shape=jax.ShapeDtypeStruct(x.shape, F32),
        compiler_params=_cparams(1),
        name="cumsum",
    )(x)
    return out.reshape(b, h, l)


def _lane_tile(a, width):
    return a if width == LANES else jnp.concatenate([a] * (width // LANES), axis=1)


def _online_update(js, ss, vx, m_scr, acc_scr, shift=None):
    tq = ss[0].shape[0]
    ps, alphas = [], []
    for j, s in zip(js, ss):
        m_prev = m_scr[j]
        rmax = jnp.max(s, axis=1, keepdims=True)
        if shift is not None:
            rmax = rmax + shift
        m_new = jnp.maximum(m_prev, rmax)
        sub = m_new if shift is None else m_new - shift
        ps.append(jnp.exp(s - _lane_tile(sub, s.shape[1])).astype(BF16))
        alphas.append(jnp.exp(m_prev - m_new))
        m_scr[j] = m_new
    pv = _dot(ps[0] if len(ps) == 1 else jnp.concatenate(ps, axis=0), vx)
    for i, j in enumerate(js):
        acc_scr[j] = alphas[i] * acc_scr[j] + pv[i * tq:(i + 1) * tq]


def _stack_maps(q_ref, per_pair):
    n_pairs = q_ref.shape[1] // LANES // per_pair
    return [jnp.concatenate([q_ref[:, m * LANES:(m + 1) * LANES] for m in range(p * per_pair, (p + 1) * per_pair)],
                            axis=0) for p in range(n_pairs)]


def _map_scores(q_stacks, kt_ref, start, tk, tq):
    out = []
    for p, qs in enumerate(q_stacks):
        sp = _dot(qs, kt_ref[p * LANES:(p + 1) * LANES, pl.ds(start, tk)])
        out += [sp[i * tq:(i + 1) * tq] for i in range(qs.shape[0] // tq)]
    return out


def _init_online(m_scr, acc_scr):
    m_scr[...] = jnp.full(m_scr.shape, NEG, F32)
    acc_scr[...] = jnp.zeros(acc_scr.shape, F32)


def _normalized(acc):
    return acc[:, 0:HEAD_DIM] / acc[:, HEAD_DIM:HEAD_DIM + 1]


def _attn_scratch(n_state, tq):
    return [pltpu.VMEM((n_state, tq, LANES), F32), pltpu.VMEM((n_state, tq, LANES), F32)]


def _dsa_kernel(qi_ref, w_ref, q_ref, ki_ref, k_ref, v_ref, o_ref, key_scr, gmax_scr, m_scr, acc_scr,
                *, tq, tk, q_off, l_valid, ksel, pos_bits):
    i = pl.program_id(1)
    q0 = q_off + i * tq
    qchunk = (q0 + lax.broadcasted_iota(I32, (tq, 1), 0)) // CHUNK
    n_end = jnp.minimum(((q0 + tq - 1) // CHUNK + 1) * CHUNK, l_valid)
    nblk = (n_end + tk - 1) // tk
    kcol = lax.broadcasted_iota(I32, (1, tk), 1)

    qi = qi_ref[...]
    w = w_ref[...]
    qi_stack = jnp.concatenate([qi[:, g * IDX_DIM:(g + 1) * IDX_DIM] for g in range(IDX_HEADS)], axis=0)
    ws = [jnp.broadcast_to(w[:, MISC_AW + g:MISC_AW + g + 1], (tq, LANES)) for g in range(IDX_HEADS)]

    def idx_body(j, carry):
        start = pl.multiple_of(j * tk, tk)
        s_all = _dot(qi_stack, ki_ref[0:IDX_DIM, pl.ds(start, tk)])
        idx = jnp.zeros((tq, tk), F32)
        for g in range(IDX_HEADS):
            idx = idx + jnp.maximum(s_all[g * tq:(g + 1) * tq], 0.0) * _lane_tile(ws[g], tk)
        bits = lax.bitcast_convert_type(idx, I32)
        key = jnp.where(bits < 0, bits ^ INT_MAX, bits)
        key = jnp.where(idx == 0.0, 0, key)
        kpos = start + kcol
        kchunk = jnp.where(kpos < l_valid, kpos // CHUNK, INT_MAX)
        key = jnp.where(kchunk <= qchunk, key, INT_MIN)
        key_scr[:, pl.ds(start, tk)] = key
        gmax_scr[...] = jnp.maximum(gmax_scr[...], key)
        return carry

    gmax_scr[...] = jnp.full((tq, tk), INT_MIN, I32)
    lax.fori_loop(0, nblk, idx_body, 0)

    @pl.when(nblk % 2 == 1)
    def _():
        key_scr[:, pl.ds(pl.multiple_of(nblk * tk, tk), tk)] = jnp.full((tq, tk), INT_MIN, I32)

    npair = (nblk + 1) // 2

    rc_coarse = min(tq, 128)
    rc_fine = min(tq, 32)

    lane_col = lax.broadcasted_iota(I32, (1, LANES), 1)

    def count(indicator, *row_args, src=key_scr, nsteps=npair, width=2 * tk, rc=rc_coarse, active=None):
        outs = []
        for ci, r0 in enumerate(range(0, tq, rc)):
            args = [a[r0:r0 + rc] for a in row_args]

            def body(j, acc, r0=r0, args=args):
                for c in range(width // LANES):
                    start = pl.multiple_of(j * width + c * LANES, LANES)
                    acc = acc + indicator(src[r0:r0 + rc, pl.ds(start, LANES)], start + lane_col, *args)
                return acc

            def sweep(body=body):
                acc = lax.fori_loop(0, nsteps, body, jnp.zeros((rc, LANES), F32))
                return jnp.broadcast_to(jnp.sum(acc, axis=1, keepdims=True), (rc, LANES))

            if active is None:
                outs.append(sweep())
            else:
                outs.append(lax.cond(active[ci], sweep, lambda: jnp.zeros((rc, LANES), F32)))
        return outs[0] if len(outs) == 1 else jnp.concatenate(outs, axis=0)

    ge = lambda kt, kp, c: jnp.where(kt >= c, 1.0, 0.0)
    n_adm = jnp.broadcast_to(jnp.minimum((qchunk + 1) * CHUNK, l_valid).astype(F32), (tq, LANES))

    def lb_body(t, ans):
        cand_u = ans | lax.shift_left(jnp.int32(1), 31 - t)
        cnt = count(ge, cand_u ^ INT_MIN, src=gmax_scr, nsteps=1, width=tk)
        return jnp.where(cnt >= ksel, cand_u, ans)

    lb_bits = 16
    lb = lax.fori_loop(0, lb_bits, lb_body, jnp.zeros((tq, LANES), I32)) ^ INT_MIN
    gm = gmax_scr[...]
    gbits = jnp.where(gm < 0, gm ^ INT_MAX, gm)
    gval = jnp.where(gm == INT_MIN, -jnp.inf, lax.bitcast_convert_type(gbits, F32))
    vmax = jnp.broadcast_to(jnp.max(gval, axis=1, keepdims=True), (tq, LANES))
    mbits = lax.bitcast_convert_type(vmax, I32)
    kmax = jnp.where(mbits < 0, mbits ^ INT_MAX, mbits)
    lo0 = jnp.maximum(lb, INT_MIN + 1)
    hi0 = jnp.minimum(kmax, INT_MAX - 1) + 1

    def open_rows(lo, hi, c_lo):
        return jnp.where(c_lo == ksel, 0.0, jnp.where(lo + 1 >= hi, 0.0, jnp.where(n_adm <= ksel, 0.0, 1.0)))

    def chunk_flags(rows01, rc):
        return tuple(jnp.max(rows01[r0:r0 + rc]) > 0.0 for r0 in range(0, tq, rc))

    def n_open(rows01):
        return jnp.sum(rows01[:, 0:1])

    def halve_twice(lo, hi, c_lo, rc, active):
        for _ in range(2):
            mid = (lo >> 1) + (hi >> 1) + (lo & hi & 1)
            cnt = count(ge, mid, rc=rc, active=active)
            take = cnt >= ksel
            lo = jnp.where(take, mid, lo)
            c_lo = jnp.where(take, cnt, c_lo)
            hi = jnp.where(take, hi, mid)
        return lo, hi, c_lo

    few_rows = 8.0

    def coarse_cond(st):
        return jnp.logical_and(st[0] < 34, st[4] > few_rows)

    def coarse_body(st):
        t, lo, hi, c_lo = st[:4]
        lo, hi, c_lo = halve_twice(lo, hi, c_lo, rc_coarse, st[5:])
        rows01 = open_rows(lo, hi, c_lo)
        return (t + 2, lo, hi, c_lo, n_open(rows01)) + chunk_flags(rows01, rc_coarse)

    def fine_cond(st):
        return jnp.logical_and(st[0] < 34, st[4] > 0.0)

    def fine_body(st):
        t, lo, hi, c_lo = st[:4]
        lo, hi, c_lo = halve_twice(lo, hi, c_lo, rc_fine, st[5:])
        rows01 = open_rows(lo, hi, c_lo)
        return (t + 2, lo, hi, c_lo, n_open(rows01)) + chunk_flags(rows01, rc_fine)

    c_lo0 = count(ge, lo0)
    rows01 = open_rows(lo0, hi0, c_lo0)
    st = lax.while_loop(coarse_cond, coarse_body,
                        (jnp.int32(0), lo0, hi0, c_lo0, n_open(rows01)) + chunk_flags(rows01, rc_coarse))
    rows01 = open_rows(*st[1:4])
    st = lax.while_loop(fine_cond, fine_body, st[:4] + (n_open(rows01),) + chunk_flags(rows01, rc_fine))
    tau, c_ge = st[1], st[3]

    tie_rows = jnp.where(c_ge > ksel, 1.0, 0.0)
    tie_chunks = chunk_flags(tie_rows, rc_fine)
    any_tie = jnp.max(tie_rows) > 0.0
    need = ksel - count(lambda kt, kp, c: jnp.where(kt > c, 1.0, 0.0), tau, rc=rc_fine, active=tie_chunks)

    def pos_body(t, p):
        cand = p | lax.shift_left(jnp.int32(1), pos_bits - 1 - t)
        cnt = count(lambda kt, kp, tv, cv: jnp.where(kt == tv, jnp.where(kp < cv, 1.0, 0.0), 0.0), tau, cand,
                    rc=rc_fine, active=tie_chunks)
        return jnp.where(cnt < need, cand, p)

    p_lim = lax.fori_loop(0, jnp.where(any_tie, pos_bits, 0), pos_body, jnp.zeros((tq, LANES), I32))
    p_lim = jnp.where(tie_rows > 0.0, p_lim, INT_MAX)

    _init_online(m_scr, acc_scr)
    q_stacks = _stack_maps(q_ref, 2)
    tau_t = _lane_tile(tau, tk)
    p_lim_t = _lane_tile(p_lim, tk)

    def att_body(j, carry):
        start = pl.multiple_of(j * tk, tk)
        kt = key_scr[:, pl.ds(start, tk)]
        bias = jnp.where(kt > tau_t, 0.0, jnp.where(kt == tau_t, jnp.where(start + kcol <= p_lim_t, 0.0, NEG), NEG))
        ss = _map_scores(q_stacks, k_ref, start, tk, tq)
        for h in range(HEADS):
            _online_update([h], [ss[h] + bias], v_ref[pl.ds(start, tk), h * LANES:(h + 1) * LANES], m_scr, acc_scr)
        return carry

    lax.fori_loop(0, nblk, att_body, 0)
    o_ref[...] = jnp.concatenate([_normalized(acc_scr[h]) for h in range(HEADS)], axis=1).astype(BF16)


def _dsa(qi, misc, q, ki, k, v, *, q_off, l_valid, tq):
    b, t, _ = q.shape
    lp = v.shape[1]
    tk = 512
    assert lp % (2 * tk) == 0 and t % tq == 0
    ksel = min(TOPK_MAX, l_valid // 4)
    qtok = lambda width: pl.BlockSpec((None, tq, width), lambda bi, i: (bi, i, 0))
    res = lambda width: pl.BlockSpec((None, lp, width), lambda bi, i: (bi, 0, 0), pipeline_mode=pl.Buffered(1))
    res_t = lambda rows: pl.BlockSpec((None, rows, lp), lambda bi, i: (bi, 0, 0), pipeline_mode=pl.Buffered(1))
    kern = functools.partial(_dsa_kernel, tq=tq, tk=tk, q_off=q_off, l_valid=l_valid, ksel=float(ksel),
                             pos_bits=int(lp - 1).bit_length())
    return pl.pallas_call(
        kern,
        grid=(b, t // tq),
        in_specs=[qtok(IDX_HEADS * IDX_DIM), qtok(MISC_W), qtok(HEADS * LANES), res_t(MISC_W), res_t(GROUP_W),
                  res(HEADS * LANES)],
        out_specs=qtok(GROUP_W),
        out_shape=jax.ShapeDtypeStruct((b, t, GROUP_W), BF16),
        scratch_shapes=[pltpu.VMEM((tq, lp), I32), pltpu.VMEM((tq, tk), I32)] + _attn_scratch(HEADS, tq),
        compiler_params=_cparams(2),
        name="dsa",
    )(qi, misc, q, ki, k, v)


def _fox_kernel(q_ref, cq_ref, k_ref, v_ref, ck_ref, o_ref, m_scr, acc_scr, *, tq, tk, q_off, l_valid):
    i = pl.program_id(1)
    q0 = q_off + i * tq
    qpos = q0 + lax.broadcasted_iota(I32, (tq, 1), 0)
    nblk = (jnp.minimum(q0 + tq, l_valid) + tk - 1) // tk
    nfull = jnp.minimum((q0 + 1) // tk, l_valid // tk)
    kcol = lax.broadcasted_iota(I32, (1, tk), 1)
    cq = cq_ref[...]
    cqs = [jnp.broadcast_to(cq[:, h:h + 1], (tq, LANES)) for h in range(HEADS)]
    _init_online(m_scr, acc_scr)
    q_stacks = _stack_maps(q_ref, 2)

    def body(j, carry, masked):
        start = pl.multiple_of(j * tk, tk)
        if masked:
            kpos = start + kcol
            valid = kpos <= qpos
        ss = _map_scores(q_stacks, k_ref, start, tk, tq)
        for h in range(HEADS):
            s = ss[h] - ck_ref[h:h + 1, pl.ds(start, tk)]
            if masked:
                s = jnp.where(valid, s, NEG)
            _online_update([h], [s], v_ref[pl.ds(start, tk), h * LANES:(h + 1) * LANES], m_scr, acc_scr,
                           shift=cqs[h])
        return carry

    lax.fori_loop(0, nfull, functools.partial(body, masked=False), 0)
    lax.fori_loop(nfull, nblk, functools.partial(body, masked=True), 0)
    o_ref[...] = jnp.concatenate([_normalized(acc_scr[h]) for h in range(HEADS)], axis=1).astype(BF16)


def _fox(q, cq, k, v, ck_t, *, q_off, l_valid, tq):
    b, t, _ = q.shape
    lp = v.shape[1]
    tk = 512
    assert lp % tk == 0 and t % tq == 0
    qtok = lambda width: pl.BlockSpec((None, tq, width), lambda bi, i: (bi, i, 0))
    res = lambda rows, width: pl.BlockSpec((None, rows, width), lambda bi, i: (bi, 0, 0),
                                           pipeline_mode=pl.Buffered(1))
    kern = functools.partial(_fox_kernel, tq=tq, tk=tk, q_off=q_off, l_valid=l_valid)
    return pl.pallas_call(
        kern,
        grid=(b, t // tq),
        in_specs=[qtok(HEADS * LANES), qtok(HEADS), res(GROUP_W, lp), res(lp, HEADS * LANES), res(HEADS, lp)],
        out_specs=qtok(GROUP_W),
        out_shape=jax.ShapeDtypeStruct((b, t, GROUP_W), BF16),
        scratch_shapes=_attn_scratch(HEADS, tq),
        compiler_params=_cparams(2),
        name="fox",
    )(q, cq, k, v, ck_t)


def _diff_kernel(q_ref, k_ref, v_ref, lam_ref, gsub_ref, o_ref, m_scr, acc_scr,
                 *, tq, tk, q_off, l_valid, out_scale):
    i = pl.program_id(1)
    q0 = q_off + i * tq
    qchunk = (q0 + lax.broadcasted_iota(I32, (tq, 1), 0)) // CHUNK
    n_end = jnp.minimum(((q0 + tq - 1) // CHUNK + 1) * CHUNK, l_valid)
    nblk = (n_end + tk - 1) // tk
    nfull = jnp.minimum(((q0 // CHUNK + 1) * CHUNK) // tk, l_valid // tk)
    kcol = lax.broadcasted_iota(I32, (1, tk), 1)
    _init_online(m_scr, acc_scr)
    q_stacks = _stack_maps(q_ref, 4)

    def body(j, carry, masked):
        start = pl.multiple_of(j * tk, tk)
        if masked:
            kpos = start + kcol
            valid = jnp.where(kpos < l_valid, kpos // CHUNK, INT_MAX) <= qchunk
        ss = _map_scores(q_stacks, k_ref, start, tk, tq)
        if masked:
            ss = [jnp.where(valid, s, NEG) for s in ss]
        for h in range(HEADS):
            _online_update([2 * h, 2 * h + 1], ss[2 * h:2 * h + 2], v_ref[pl.ds(start, tk), h * LANES:(h + 1) * LANES],
                           m_scr, acc_scr)
        return carry

    lax.fori_loop(0, nfull, functools.partial(body, masked=False), 0)
    lax.fori_loop(nfull, nblk, functools.partial(body, masked=True), 0)

    lv = lam_ref[...]
    lam = (jnp.exp(jnp.sum(lv[0:1] * lv[1:2], axis=1, keepdims=True))
           - jnp.exp(jnp.sum(lv[2:3] * lv[3:4], axis=1, keepdims=True)) + (1.0 - out_scale))
    outs = []
    for h in range(HEADS):
        o = _normalized(acc_scr[2 * h]) - lam * _normalized(acc_scr[2 * h + 1])
        outs.append(_rms(o, gsub_ref[...]) * out_scale)
    o_ref[...] = jnp.concatenate(outs, axis=1).astype(BF16)


def _diff(q, k, v, lam_vecs, g_sub, *, q_off, l_valid, tq, lam_init):
    b, t, _ = q.shape
    lp = v.shape[1]
    tk = 512
    assert lp % tk == 0 and t % tq == 0
    qtok = lambda width: pl.BlockSpec((None, tq, width), lambda bi, i: (bi, i, 0))
    res = lambda rows, width: pl.BlockSpec((None, rows, width), lambda bi, i: (bi, 0, 0),
                                           pipeline_mode=pl.Buffered(1))
    const = lambda shape: pl.BlockSpec(shape, lambda bi, i: (0, 0))
    kern = functools.partial(_diff_kernel, tq=tq, tk=tk, q_off=q_off, l_valid=l_valid, out_scale=1.0 - lam_init)
    return pl.pallas_call(
        kern,
        grid=(b, t // tq),
        in_specs=[qtok(2 * HEADS * LANES), res(GROUP_W, lp), res(lp, HEADS * LANES), const((4, DIFF_DIM)),
                  const((1, HEAD_DIM))],
        out_specs=qtok(GROUP_W),
        out_shape=jax.ShapeDtypeStruct((b, t, GROUP_W), BF16),
        scratch_shapes=_attn_scratch(2 * HEADS, tq),
        compiler_params=_cparams(2),
        name="diff",
    )(q, k, v, lam_vecs, g_sub.reshape(1, HEAD_DIM))


def _relbias_kernel(rb_ref, o_ref):
    qi = lax.broadcasted_iota(I32, (CHUNK, BAND_KEYS), 0)
    kj = lax.broadcasted_iota(I32, (CHUNK, BAND_KEYS), 1)
    idx = jnp.clip(BAND_PREV + qi - kj, -REL_CLIP, REL_CLIP) + REL_CLIP

    def body(r, accs):
        return tuple(jnp.where(idx == r, rb_ref[h, r], accs[h]) for h in range(HEADS))

    accs = lax.fori_loop(0, 2 * REL_CLIP + 1, body, tuple(jnp.zeros((CHUNK, BAND_KEYS), F32) for _ in range(HEADS)))
    for h in range(HEADS):
        o_ref[h] = accs[h]


def _relbias(rel_bias_l):
    return pl.pallas_call(
        _relbias_kernel,
        in_specs=[pl.BlockSpec(memory_space=pltpu.SMEM)],
        out_specs=pl.BlockSpec(memory_space=pltpu.VMEM),
        out_shape=jax.ShapeDtypeStruct((HEADS, CHUNK, BAND_KEYS), F32),
        name="relbias",
    )(rel_bias_l.astype(F32))


def _band_kernel(q_ref, kp_ref, kc_ref, vp_ref, vc_ref, bias_ref, o_ref, k_scr, v_scr, *, tq, first_has_no_prev):
    i = pl.program_id(1)
    k_scr[0:BAND_PREV, :] = kp_ref[...]
    k_scr[BAND_PREV:BAND_PREV + tq, :] = kc_ref[...]
    v_scr[0:BAND_PREV, :] = vp_ref[...]
    v_scr[BAND_PREV:BAND_PREV + tq, :] = vc_ref[...]
    kcol = lax.broadcasted_iota(I32, (1, BAND_KEYS), 1)
    first = jnp.where(i == 0, BAND_PREV, 0) if first_has_no_prev else 0
    for c in range(tq // CHUNK):
        r0 = c * CHUNK
        valid = (r0 + kcol) >= first
        outs = []
        for h in range(HEADS):
            hs = slice(h * HEAD_DIM, (h + 1) * HEAD_DIM)
            s = _dot_nt(q_ref[r0:r0 + CHUNK, hs], k_scr[r0:r0 + BAND_KEYS, hs]) + bias_ref[h]
            s = jnp.where(valid, s, NEG)
            e = jnp.exp(s - jnp.max(s, axis=1, keepdims=True))
            o = _dot(e.astype(BF16), v_scr[r0:r0 + BAND_KEYS, hs])
            outs.append(o / jnp.sum(e, axis=1, keepdims=True))
        o_ref[r0:r0 + CHUNK, :] = jnp.concatenate(outs, axis=1).astype(BF16)


def _band(q, k_prev, k_cur, v_prev, v_cur, bias, *, tq, prev_from_cur):
    b, t, _ = q.shape
    assert t % tq == 0 and (not prev_from_cur or tq == BAND_PREV)
    qtok = pl.BlockSpec((None, tq, GROUP_W), lambda bi, i: (bi, i, 0))
    if prev_from_cur:
        prev = pl.BlockSpec((None, BAND_PREV, GROUP_W), lambda bi, i: (bi, jnp.maximum(i - 1, 0), 0))
    else:
        prev = pl.BlockSpec((None, BAND_PREV, GROUP_W), lambda bi, i: (bi, 0, 0))
    bspec = pl.BlockSpec((HEADS, CHUNK, BAND_KEYS), lambda bi, i: (0, 0, 0))
    kern = functools.partial(_band_kernel, tq=tq, first_has_no_prev=prev_from_cur)
    return pl.pallas_call(
        kern,
        grid=(b, t // tq),
        in_specs=[qtok, prev, qtok, prev, qtok, bspec],
        out_specs=qtok,
        out_shape=jax.ShapeDtypeStruct((b, t, GROUP_W), BF16),
        scratch_shapes=[pltpu.VMEM((BAND_PREV + tq, GROUP_W), BF16), pltpu.VMEM((BAND_PREV + tq, GROUP_W), BF16)],
        compiler_params=_cparams(2),
        name="band",
    )(q, k_prev, k_cur, v_prev, v_cur, bias)


def _memkv_kernel(m_ref, g_ref, wk_ref, wv_ref, kf_o, vf_o, kb_o, vb_o):
    hb = _rms(m_ref[...], g_ref[...]).astype(BF16)
    k = _dot(hb, wk_ref[...])
    v = _dot(hb, wv_ref[...])
    kf_o[...] = k
    vf_o[...] = v
    kb_o[...] = k.astype(BF16)
    vb_o[...] = v.astype(BF16)


def _memkv(mem, g, wk, wv):
    b, m, d = mem.shape
    xw = wk.shape[1]
    tok = lambda width: pl.BlockSpec((None, m, width), lambda bi: (bi, 0, 0))
    const = lambda shape: pl.BlockSpec(shape, lambda bi: (0, 0))
    return pl.pallas_call(
        _memkv_kernel,
        grid=(b,),
        in_specs=[tok(d), const((1, d)), const((d, xw)), const((d, xw))],
        out_specs=[tok(xw)] * 4,
        out_shape=[jax.ShapeDtypeStruct((b, m, xw), F32)] * 2 + [jax.ShapeDtypeStruct((b, m, xw), BF16)] * 2,
        compiler_params=_cparams(1),
        name="memkv",
    )(mem, g.reshape(1, d), wk, wv)


def _mix_cross_kernel(x_ref, oa_ref, ob_ref, oc_ref, od_ref, wout_ref, g_ref, wq_ref, mk_ref, mv_ref, wo_ref, o_ref):
    x = x_ref[...]
    for n, oref in enumerate((oa_ref, ob_ref, oc_ref, od_ref)):
        x = x + _dot(oref[...], wout_ref[n * GROUP_W:(n + 1) * GROUP_W, :])
    hb = _rms(x, g_ref[...]).astype(BF16)
    q = _dot(hb, wq_ref[...])
    outs = []
    for h in range(X_HEADS):
        hs = slice(h * X_HEAD_DIM, (h + 1) * X_HEAD_DIM)
        s = _dot_nt(q[:, hs].astype(BF16), mk_ref[:, hs]) * (X_HEAD_DIM ** -0.5)
        e = jnp.exp(s - jnp.max(s, axis=1, keepdims=True))
        p = e / jnp.sum(e, axis=1, keepdims=True)
        outs.append(_dot(p.astype(BF16), mv_ref[:, hs]))
    o = jnp.concatenate(outs, axis=1).astype(BF16)
    o_ref[...] = x + _dot(o, wo_ref[...])


def _mix_cross(x, oa, ob, oc, od, w_out, g, wq, mk, mv, wo):
    b, t, d = x.shape
    tm = min(512, t)
    m, xw = mk.shape[1], mk.shape[2]
    tok = lambda width: pl.BlockSpec((None, tm, width), lambda bi, i: (bi, i, 0))
    const = lambda shape: pl.BlockSpec(shape, lambda bi, i: (0, 0))
    mem = pl.BlockSpec((None, m, xw), lambda bi, i: (bi, 0, 0))
    return pl.pallas_call(
        _mix_cross_kernel,
        grid=(b, t // tm),
        in_specs=[tok(d)] + [tok(GROUP_W)] * 4 + [const(w_out.shape), const((1, d)), const(wq.shape), mem, mem,
                                                   const(wo.shape)],
        out_specs=tok(d),
        out_shape=jax.ShapeDtypeStruct((b, t, d), F32),
        compiler_params=_cparams(2),
        name="mix_cross",
    )(x, oa, ob, oc, od, w_out, g.reshape(1, d), wq, mk, mv, wo)


def _ffn_kernel(x_ref, g_ref, prev_ref, wup_ref, cw_ref, cb_ref, wdn_ref, o_ref, conv_ref, halo_scr, acc_scr,
                *, tm, d_ff, fc):
    i = pl.program_id(1)

    @pl.when(i == 0)
    def _():
        halo_scr[...] = prev_ref[...]

    x = x_ref[...]
    hb = _rms(x, g_ref[...]).astype(BF16)
    row = lax.broadcasted_iota(I32, (tm, fc), 0)
    for c in range(d_ff // fc):
        cs = slice(c * fc, (c + 1) * fc)
        g = _dot(hb, wup_ref[:, cs])
        u = _dot(hb, wup_ref[:, d_ff + c * fc:d_ff + (c + 1) * fc])
        h0 = halo_scr[0:1, cs]
        h1 = halo_scr[1:2, cs]
        gm1 = jnp.where(row == 0, h1, pltpu.roll(g, 1, 0))
        gm2 = jnp.where(row == 0, h0, jnp.where(row == 1, h1, pltpu.roll(g, 2, 0)))
        a = cb_ref[:, cs] + gm2 * cw_ref[0:1, cs]
        a = a + gm1 * cw_ref[1:2, cs]
        a = a + g * cw_ref[2:3, cs]
        y = (a * (1.0 / (1.0 + jnp.exp(-a)))) * u
        f = _dot(y.astype(BF16), wdn_ref[cs, :])
        if c == 0:
            acc_scr[...] = f
        else:
            acc_scr[...] += f
        tail = g[tm - (CONV_W - 1):tm, :]
        halo_scr[:, cs] = tail
        conv_ref[:, cs] = tail
    o_ref[...] = x + acc_scr[...]


def _ffn(x, g, prev, w_up, conv_w, conv_b, w_down):
    b, t, d = x.shape
    d_ff = w_down.shape[0]
    tm = min(512, t)
    fc = 256
    assert d_ff % fc == 0 and t % tm == 0 and tm >= CONV_W - 1
    tok = pl.BlockSpec((None, tm, d), lambda bi, i: (bi, i, 0))
    const = lambda shape: pl.BlockSpec(shape, lambda bi, i: (0, 0))
    st = pl.BlockSpec((None, CONV_W - 1, d_ff), lambda bi, i: (bi, 0, 0))
    kern = functools.partial(_ffn_kernel, tm=tm, d_ff=d_ff, fc=fc)
    return pl.pallas_call(
        kern,
        grid=(b, t // tm),
        in_specs=[tok, const((1, d)), st, const(w_up.shape), const(conv_w.shape), const((1, d_ff)),
                  const(w_down.shape)],
        out_specs=[tok, st],
        out_shape=[jax.ShapeDtypeStruct((b, t, d), F32), jax.ShapeDtypeStruct((b, CONV_W - 1, d_ff), F32)],
        scratch_shapes=[pltpu.VMEM((CONV_W - 1, d_ff), F32), pltpu.VMEM((tm, d), F32)],
        compiler_params=_cparams(2),
        name="ffn",
    )(x, g.reshape(1, d), prev, w_up, conv_w, conv_b.reshape(1, d_ff), w_down)


def _rmsnorm_kernel(x_ref, g_ref, o_ref):
    o_ref[...] = _rms(x_ref[...], g_ref[...])


def _rmsnorm(x, g):
    b, t, d = x.shape
    tm = min(1024, t)
    tok = pl.BlockSpec((None, tm, d), lambda bi, i: (bi, i, 0))
    return pl.pallas_call(
        _rmsnorm_kernel,
        grid=(b, t // tm),
        in_specs=[tok, pl.BlockSpec((1, d), lambda bi, i: (0, 0))],
        out_specs=tok,
        out_shape=jax.ShapeDtypeStruct((b, t, d), F32),
        compiler_params=_cparams(2),
        name="final_norm",
    )(x, g.reshape(1, d))


def _pad_rows(a, rows):
    return a if a.shape[1] == rows else jnp.pad(a, ((0, 0), (0, rows - a.shape[1]), (0, 0)))


def _round_up(n, m):
    return (n + m - 1) // m * m


def _value_slabs_host(v):
    b, l = v.shape[:2]
    one = jnp.zeros((b, l, HEADS, HEAD_DIM), BF16).at[..., 0].set(1.0)
    return jnp.concatenate([v.astype(BF16), one], axis=-1).reshape(b, l, HEADS * LANES)


def _mixers(pr, keys, *, q_off, l_valid, tq_a, tq_bd, band_args, band_tq, prev_from_cur, lam_vecs, g_sub,
            lam_init, logf_all):
    lp = keys["av"].shape[1]
    oa = _dsa(pr["aqi"], pr["misc"], pr["aq"], keys["aki"], keys["ak"], keys["av"],
              q_off=q_off, l_valid=l_valid, tq=tq_a)
    lc = _round_up(logf_all.shape[1], 1024)
    c_t = _cumsum_t(jnp.swapaxes(_pad_rows(logf_all, lc), 1, 2))
    t = pr["bq"].shape[1]
    cq = jnp.swapaxes(c_t[:, :, q_off:q_off + t], 1, 2)
    ck_t = c_t[:, :, :lp] if lc >= lp else jnp.pad(c_t, ((0, 0), (0, 0), (0, lp - lc)))
    ob = _fox(pr["bq"], cq, keys["bk"], keys["bv"], ck_t, q_off=q_off, l_valid=l_valid, tq=tq_bd)
    oc = _band(pr["cq"], *band_args, tq=band_tq, prev_from_cur=prev_from_cur)
    od = _diff(pr["dq"], keys["dk"], keys["dv"], lam_vecs, g_sub, q_off=q_off, l_valid=l_valid, tq=tq_bd,
               lam_init=lam_init)
    return oa, ob, oc, od


def kernel(x_prompt, x_sample, mem_prompt, cache_a_k, cache_a_v, cache_a_kidx, cache_b_k, cache_b_v,
           cache_b_logf, cache_c_k, cache_c_v, cache_d_k, cache_d_v, cache_mem_k, cache_mem_v,
           state_ffn_conv, g_mix, w_in, b_forget, rel_bias, lam_q1, lam_k1, lam_q2, lam_k2, g_sub, w_out,
           g_cross, g_mem, w_xq, w_xk, w_xv, w_xo, g_ffn, w_up, conv_w, conv_b, w_down, g_final):
    depth = w_in.shape[0]
    bp, s_len, d_model = x_prompt.shape
    bs, t_len, _ = x_sample.shape
    past = cache_a_k.shape[2]
    d_ff = w_down.shape[1]
    c_keep = min(BAND_PREV, s_len)
    l_s = past + t_len
    lp_s = _round_up(l_s, 1024)

    pos_p = jnp.arange(s_len, dtype=jnp.int32)
    pos_s = jnp.tile(past + jnp.arange(t_len, dtype=jnp.int32), bs)
    h64 = HEAD_DIM // ROT_FRACTION // 2
    h32 = DIFF_DIM // ROT_FRACTION // 2
    t64_p, t32_p = _rope_table(pos_p, h64, HEAD_DIM), _rope_table(pos_p, h32, DIFF_DIM)
    t64_s, t32_s = _rope_table(pos_s, h64, HEAD_DIM), _rope_table(pos_s, h32, DIFF_DIM)

    flat = lambda a: a.reshape(a.shape[0], a.shape[1], -1)
    heads = lambda a, hd=HEAD_DIM: a.reshape(a.shape[0], a.shape[1], a.shape[2] // hd, hd)

    xp, xs = x_prompt, x_sample
    p_states, s_states = [], []
    for l in range(depth):
        lam_init = 0.8 - 0.6 * math.exp(-0.3 * l)
        w_pack, bfp = _pack_w_in(w_in[l], b_forget[l])
        lam_vecs = jnp.stack([lam_q1[l], lam_k1[l], lam_q2[l], lam_k2[l]]).astype(F32)
        bias = _relbias(rel_bias[l])
        w_out_b, wq_b, wo_b = w_out[l].astype(BF16), w_xq[l].astype(BF16), w_xo[l].astype(BF16)
        w_up_b, w_dn_b = w_up[l].astype(BF16), w_down[l].astype(BF16)

        pr = _proj(xp, g_mix[l], w_pack, bfp, t64_p, t32_p, c_keep)
        logf = pr["misc"][:, :, MISC_BF:MISC_BF + HEADS]
        oa, ob, oc, od = _mixers(
            pr, pr, q_off=0, l_valid=s_len, tq_a=256, tq_bd=512,
            band_args=(pr["ck"], pr["ck"], pr["cv"], pr["cv"], bias), band_tq=BAND_PREV, prev_from_cur=True,
            lam_vecs=lam_vecs, g_sub=g_sub[l], lam_init=lam_init, logf_all=logf)
        mkf, mvf, mkb, mvb = _memkv(mem_prompt, g_mem[l], w_xk[l].astype(BF16), w_xv[l].astype(BF16))
        xp = _mix_cross(xp, oa, ob, oc, od, w_out_b, g_cross[l], wq_b, mkb, mvb, wo_b)
        xp, conv_p = _ffn(xp, g_ffn[l], jnp.zeros((bp, CONV_W - 1, d_ff), F32), w_up_b, conv_w[l], conv_b[l], w_dn_b)
        p_states.append((heads(pr["akf"]), heads(pr["avf"]), pr["misc"][:, :, :IDX_DIM],
                         heads(pr["bkf"]), heads(pr["bvf"]), logf,
                         heads(pr["ckf"]), heads(pr["cvf"]), heads(pr["dkf"]), heads(pr["dvf"]),
                         conv_p, heads(mkf, X_HEAD_DIM), heads(mvf, X_HEAD_DIM)))

        prs = _proj(xs.reshape(1, bs * t_len, d_model), g_mix[l], w_pack, bfp, t64_s, t32_s, bs * t_len)
        prs = {n: a.reshape(bs, t_len, a.shape[-1]) for n, a in prs.items() if n not in ("ak", "aki", "bk", "dk")}
        cat_t = lambda cache, new: jnp.swapaxes(
            _pad_rows(jnp.concatenate([flat(cache), new], axis=1), lp_s).astype(BF16), 1, 2)
        kidx = jnp.concatenate([cache_a_kidx[l], prs["misc"][:, :, :IDX_DIM]], axis=1)
        kidx = jnp.pad(kidx, ((0, 0), (0, lp_s - l_s), (0, MISC_W - IDX_DIM)))
        catv = lambda cache, new: _pad_rows(jnp.concatenate([_value_slabs_host(cache), new], axis=1), lp_s)
        keys = {"aki": jnp.swapaxes(kidx.astype(BF16), 1, 2),
                "ak": cat_t(cache_a_k[l], prs["akf"]), "av": catv(cache_a_v[l], prs["av"]),
                "bk": cat_t(cache_b_k[l], prs["bkf"]), "bv": catv(cache_b_v[l], prs["bv"]),
                "dk": cat_t(cache_d_k[l], prs["dkf"]), "dv": catv(cache_d_v[l], prs["dv"])}
        logf_s = prs["misc"][:, :, MISC_BF:MISC_BF + HEADS]
        logf_all = jnp.concatenate([cache_b_logf[l].astype(F32), logf_s], axis=1)
        oa, ob, oc, od = _mixers(
            prs, keys, q_off=past, l_valid=l_s, tq_a=t_len, tq_bd=t_len,
            band_args=(flat(cache_c_k[l]).astype(BF16), prs["ck"], flat(cache_c_v[l]).astype(BF16), prs["cv"], bias),
            band_tq=t_len, prev_from_cur=False,
            lam_vecs=lam_vecs, g_sub=g_sub[l], lam_init=lam_init, logf_all=logf_all)
        xs = _mix_cross(xs, oa, ob, oc, od, w_out_b, g_cross[l], wq_b,
                        flat(cache_mem_k[l]).astype(BF16), flat(cache_mem_v[l]).astype(BF16), wo_b)
        xs, conv_s = _ffn(xs, g_ffn[l], state_ffn_conv[l], w_up_b, conv_w[l], conv_b[l], w_dn_b)
        s_states.append((heads(prs["akf"]), heads(prs["avf"]), prs["misc"][:, :, :IDX_DIM],
                         heads(prs["bkf"]), heads(prs["bvf"]), logf_s,
                         heads(prs["ckf"]), heads(prs["cvf"]), heads(prs["dkf"]), heads(prs["dvf"]), conv_s))

    y_prompt = _rmsnorm(xp, g_final)
    y_sample = _rmsnorm(xs, g_final)
    p_out = [jnp.stack(z, axis=0) for z in zip(*p_states)]
    s_out = [jnp.stack(z, axis=0) for z in zip(*s_states)]
    return (y_prompt, y_sample, *p_out, *s_out)
```

```python
import functools
import math

import numpy as np
import jax
import jax.numpy as jnp
from jax import lax
from jax.experimental import pallas as pl
from jax.experimental.pallas import tpu as pltpu

F32 = jnp.float32
BF16 = jnp.bfloat16
I32 = jnp.int32

CHUNK = 64
HEAD_DIM = 64
HEADS = 4
GROUP_W = HEADS * HEAD_DIM
ROT_FRACTION = 4
ROPE_THETA = 500000.0
IDX_HEADS = 8
IDX_DIM = 64
TOPK_MAX = 256
BAND_CHUNKS = 8
BAND_KEYS = (BAND_CHUNKS + 1) * CHUNK
BAND_PREV = BAND_CHUNKS * CHUNK
REL_CLIP = 128
DIFF_DIM = HEAD_DIM // 2
X_HEADS = 4
X_HEAD_DIM = 128
CONV_W = 3
EPS = 1e-6

LANES = 128
VMEM_LIMIT_BYTES = 56 * 1024 * 1024

NEG = -1e30
INT_MIN = -2147483648
INT_MAX = 2147483647

MISC_W = LANES
MISC_AW = IDX_DIM
MISC_BF = IDX_DIM + IDX_HEADS
PROJ_COLS = 3 * GROUP_W + IDX_HEADS * IDX_DIM + MISC_W + 9 * GROUP_W


def _cparams(n_axes):
    return pltpu.CompilerParams(dimension_semantics=("arbitrary",) * n_axes,
                                vmem_limit_bytes=VMEM_LIMIT_BYTES)


def _rms(x, g):
    return (x * lax.rsqrt(jnp.mean(x * x, axis=-1, keepdims=True) + EPS)) * g


def _dot(a, b):
    return jnp.dot(a, b, preferred_element_type=F32)


def _dot_nt(a, b):
    return lax.dot_general(a, b, (((1,), (1,)), ((), ())), preferred_element_type=F32)


def _rope(z, tab, half):
    w = z.shape[1]
    reps = w // LANES
    rep = lambda t: t if reps == 1 else jnp.concatenate([t] * reps, axis=1)
    c = rep(tab[:, 0:LANES])
    s1 = rep(tab[:, LANES:2 * LANES])
    s2 = rep(tab[:, 2 * LANES:3 * LANES])
    return z * c + pltpu.roll(z, w - half, 1) * s1 + pltpu.roll(z, half, 1) * s2


def _slab_masked(z, widths):
    lane = lax.broadcasted_iota(I32, (z.shape[0], LANES), 1)
    outs = []
    for j in range(GROUP_W // widths):
        lo = (j * widths) % LANES
        zs = z[:, (j * widths) // LANES * LANES:((j * widths) // LANES + 1) * LANES]
        outs.append(jnp.where(lane < lo, 0.0, jnp.where(lane < lo + widths, zs, 0.0)))
    return jnp.concatenate(outs, axis=1)


def _value_slabs(z):
    one = jnp.where(lax.broadcasted_iota(I32, (z.shape[0], HEAD_DIM), 1) == 0, 1.0, 0.0)
    return jnp.concatenate([t for h in range(HEADS) for t in (z[:, h * HEAD_DIM:(h + 1) * HEAD_DIM], one)], axis=1)


def _proj_kernel(x_ref, g_ref, w_ref, bf_ref, t64_ref, t32_ref,
                 aq_o, ak_o, av_o, aqi_o, aki_o, bq_o, bk_o, bv_o, cq_o, ck_o, cv_o, dq_o, dk_o, dv_o,
                 akf_o, avf_o, misc_o, bkf_o, bvf_o, ckf_o, cvf_o, dkf_o, dvf_o):
    hb = _rms(x_ref[...], g_ref[...]).astype(BF16)
    t64 = t64_ref[...]
    t32 = t32_ref[...]
    h64 = HEAD_DIM // ROT_FRACTION // 2
    h32 = DIFF_DIM // ROT_FRACTION // 2
    qscale = HEAD_DIM ** -0.5
    col = [0]

    def mm(width):
        c0 = col[0]
        col[0] = c0 + width
        return _dot(hb, w_ref[:, c0:c0 + width])

    z = _rope(mm(GROUP_W), t64, h64)
    aq_o[...] = _slab_masked(z * qscale, HEAD_DIM).astype(BF16)
    z = _rope(mm(GROUP_W), t64, h64)
    akf_o[...] = z
    ak_o[...] = z.T.astype(BF16)
    z = mm(GROUP_W)
    avf_o[...] = z
    av_o[...] = _value_slabs(z).astype(BF16)
    z = _rope(mm(IDX_HEADS * IDX_DIM), t64, h64)
    aqi_o[...] = (z * (IDX_DIM ** -0.5)).astype(BF16)
    z = mm(MISC_W)
    r = _rope(z, t64, h64)
    lane = lax.broadcasted_iota(I32, z.shape, 1)
    zf = z + bf_ref[...]
    logsig = jnp.minimum(zf, 0.0) - jnp.log1p(jnp.exp(-jnp.abs(zf)))
    misc_o[...] = jnp.where(lane < MISC_AW, r, jnp.where(lane < MISC_BF, z * (IDX_HEADS ** -0.5), logsig))
    aki_o[...] = jnp.where(lane < MISC_AW, r, 0.0).T.astype(BF16)
    z = mm(GROUP_W)
    bq_o[...] = _slab_masked(z * qscale, HEAD_DIM).astype(BF16)
    z = mm(GROUP_W)
    bkf_o[...] = z
    bk_o[...] = z.T.astype(BF16)
    z = mm(GROUP_W)
    bvf_o[...] = z
    bv_o[...] = _value_slabs(z).astype(BF16)
    z = mm(GROUP_W)
    cq_o[...] = (z * qscale).astype(BF16)
    z = mm(GROUP_W)
    ckf_o[...] = z
    ck_o[...] = z.astype(BF16)
    z = mm(GROUP_W)
    cvf_o[...] = z
    cv_o[...] = z.astype(BF16)
    z = _rope(mm(GROUP_W), t32, h32)
    dq_o[...] = _slab_masked(z * (DIFF_DIM ** -0.5), DIFF_DIM).astype(BF16)
    z = _rope(mm(GROUP_W), t32, h32)
    dkf_o[...] = z
    dk_o[...] = z.T.astype(BF16)
    z = mm(GROUP_W)
    dvf_o[...] = z
    dv_o[...] = _value_slabs(z).astype(BF16)


def _rope_table(pos, half, period):
    inv = ROPE_THETA ** (-jnp.arange(half, dtype=F32) / half)
    ang = pos.astype(F32)[:, None] * inv[None, :]
    cos, sin = jnp.cos(ang), jnp.sin(ang)
    t = pos.shape[0]
    rest = period - 2 * half
    zh = jnp.zeros((t, half), F32)
    zr = jnp.zeros((t, rest), F32)
    c = jnp.concatenate([cos, cos, jnp.ones((t, rest), F32)], axis=1)
    s1 = jnp.concatenate([-sin, zh, zr], axis=1)
    s2 = jnp.concatenate([zh, sin, zr], axis=1)
    rep = lambda a: jnp.tile(a, (1, LANES // period))
    return jnp.concatenate([rep(c), rep(s1), rep(s2)], axis=1)


def _pack_w_in(w_in_l, b_forget_l):
    sizes = (GROUP_W, GROUP_W, GROUP_W, IDX_HEADS * IDX_DIM, IDX_DIM, IDX_HEADS,
             GROUP_W, GROUP_W, GROUP_W, HEADS, GROUP_W, GROUP_W, GROUP_W, GROUP_W, GROUP_W, GROUP_W)
    pts = [int(p) for p in np.cumsum(sizes)[:-1]]
    (aq, ak, av, aqi, aki, aw, bq, bk, bv, bf, cq, ck, cv, dq, dk, dv) = jnp.split(w_in_l, pts, axis=1)
    d = w_in_l.shape[0]
    pad = jnp.zeros((d, MISC_W - IDX_DIM - IDX_HEADS - HEADS), w_in_l.dtype)
    misc = jnp.concatenate([aki, aw, bf, pad], axis=1)
    w = jnp.concatenate([aq, ak, av, aqi, misc, bq, bk, bv, cq, ck, cv, dq, dk, dv], axis=1).astype(BF16)
    bfp = jnp.zeros((1, MISC_W), F32).at[0, MISC_BF:MISC_BF + HEADS].set(b_forget_l.astype(F32))
    return w, bfp


def _proj(x, g, w, bfp, t64, t32, c_keep):
    b, t, d = x.shape
    tm = min(512, t)
    nt = t // tm
    nkeep = c_keep // tm
    tok = lambda width: pl.BlockSpec((None, tm, width), lambda bi, i: (bi, i, 0))
    ctok = pl.BlockSpec((None, tm, GROUP_W), lambda bi, i: (bi, jnp.maximum(i - (nt - nkeep), 0), 0))
    const = lambda shape: pl.BlockSpec(shape, lambda bi, i: (0, 0))
    tab = pl.BlockSpec((tm, 3 * LANES), lambda bi, i: (i, 0))
    sds = lambda width, dt, rows=t: jax.ShapeDtypeStruct((b, rows, width), dt)
    bf_names = ["aq", "ak", "av", "aqi", "aki", "bq", "bk", "bv", "cq", "ck", "cv", "dq", "dk", "dv"]
    qx_w, vx_w = HEADS * LANES, HEADS * LANES
    bf_w = [qx_w, GROUP_W, vx_w, IDX_HEADS * IDX_DIM, MISC_W, qx_w, GROUP_W, vx_w, GROUP_W, GROUP_W, GROUP_W,
            2 * qx_w, GROUP_W, vx_w]
    f_names = ["akf", "avf", "misc", "bkf", "bvf", "ckf", "cvf", "dkf", "dvf"]
    f_w = [GROUP_W, GROUP_W, MISC_W] + [GROUP_W] * 6
    out_shape = [sds(wd, BF16) for wd in bf_w]
    out_specs = [tok(wd) for wd in bf_w]
    for n, wd in zip(bf_names, bf_w):
        if n in ("ak", "aki", "bk", "dk"):
            out_shape[bf_names.index(n)] = jax.ShapeDtypeStruct((b, wd, t), BF16)
            out_specs[bf_names.index(n)] = pl.BlockSpec((None, wd, tm), lambda bi, i: (bi, 0, i))
    for n, wd in zip(f_names, f_w):
        if n in ("ckf", "cvf"):
            out_shape.append(sds(wd, F32, c_keep))
            out_specs.append(ctok)
        else:
            out_shape.append(sds(wd, F32))
            out_specs.append(tok(wd))
    outs = pl.pallas_call(
        _proj_kernel,
        grid=(b, nt),
        in_specs=[tok(d), const((1, d)), const((d, PROJ_COLS)), const((1, MISC_W)), tab, tab],
        out_specs=out_specs,
        out_shape=out_shape,
        compiler_params=_cparams(2),
        name="proj",
    )(x, g.reshape(1, d), w, bfp, t64, t32)
    return dict(zip(bf_names + f_names, outs))


def _cumsum_kernel(x_ref, o_ref, *, nb):
    x = x_ref[...]
    lane = lax.broadcasted_iota(I32, x.shape, 1)
    d = 1
    while d < LANES:
        x = x + jnp.where(lane >= d, pltpu.roll(x, d, 1), 0.0)
        d *= 2
    row = lax.broadcasted_iota(I32, x.shape, 0) % nb
    tot = jnp.broadcast_to(x[:, LANES - 1:LANES], x.shape)
    exc = jnp.where(row >= 1, pltpu.roll(tot, 1, 0), 0.0)
    d = 1
    while d < nb:
        exc = exc + jnp.where(row >= d, pltpu.roll(exc, d, 0), 0.0)
        d *= 2
    o_ref[...] = x + exc


def _cumsum_t(logf_t):
    b, h, l = logf_t.shape
    nb = l // LANES
    x = logf_t.reshape(b, h * nb, LANES)
    spec = pl.BlockSpec((None, h * nb, LANES), lambda bi: (bi, 0, 0))
    out = pl.pallas_call(
        functools.partial(_cumsum_kernel, nb=nb),
        grid=(b,),
        in_specs=[spec],
        out_specs=spec,
        out_shape=jax.ShapeDtypeStruct(x.shape, F32),
        compiler_params=_cparams(1),
        name="cumsum",
    )(x)
    return out.reshape(b, h, l)


def _lane_tile(a, width):
    return a if width == LANES else jnp.concatenate([a] * (width // LANES), axis=1)


def _online_update(js, ss, vx, m_scr, acc_scr, shift=None):
    tq = ss[0].shape[0]
    ps, alphas = [], []
    for j, s in zip(js, ss):
        m_prev = m_scr[j]
        rmax = jnp.max(s, axis=1, keepdims=True)
        if shift is not None:
            rmax = rmax + shift
        m_new = jnp.maximum(m_prev, rmax)
        sub = m_new if shift is None else m_new - shift
        ps.append(jnp.exp(s - _lane_tile(sub, s.shape[1])).astype(BF16))
        alphas.append(jnp.exp(m_prev - m_new))
        m_scr[j] = m_new
    pv = _dot(ps[0] if len(ps) == 1 else jnp.concatenate(ps, axis=0), vx)
    for i, j in enumerate(js):
        acc_scr[j] = alphas[i] * acc_scr[j] + pv[i * tq:(i + 1) * tq]


def _stack_maps(q_ref, per_pair):
    n_pairs = q_ref.shape[1] // LANES // per_pair
    return [jnp.concatenate([q_ref[:, m * LANES:(m + 1) * LANES] for m in range(p * per_pair, (p + 1) * per_pair)],
                            axis=0) for p in range(n_pairs)]


def _map_scores(q_stacks, kt_ref, start, tk, tq):
    out = []
    for p, qs in enumerate(q_stacks):
        sp = _dot(qs, kt_ref[p * LANES:(p + 1) * LANES, pl.ds(start, tk)])
        out += [sp[i * tq:(i + 1) * tq] for i in range(qs.shape[0] // tq)]
    return out


def _init_online(m_scr, acc_scr):
    m_scr[...] = jnp.full(m_scr.shape, NEG, F32)
    acc_scr[...] = jnp.zeros(acc_scr.shape, F32)


def _normalized(acc):
    return acc[:, 0:HEAD_DIM] / acc[:, HEAD_DIM:HEAD_DIM + 1]


def _attn_scratch(n_state, tq):
    return [pltpu.VMEM((n_state, tq, LANES), F32), pltpu.VMEM((n_state, tq, LANES), F32)]


def _dsa_kernel(qi_ref, w_ref, q_ref, ki_ref, k_ref, v_ref, o_ref, key_scr, gmax_scr, m_scr, acc_scr,
                *, tq, tk, q_off, l_valid, ksel, pos_bits):
    i = pl.program_id(1)
    q0 = q_off + i * tq
    qchunk = (q0 + lax.broadcasted_iota(I32, (tq, 1), 0)) // CHUNK
    n_end = jnp.minimum(((q0 + tq - 1) // CHUNK + 1) * CHUNK, l_valid)
    nblk = (n_end + tk - 1) // tk
    kcol = lax.broadcasted_iota(I32, (1, tk), 1)

    qi = qi_ref[...]
    w = w_ref[...]
    qi_stack = jnp.concatenate([qi[:, g * IDX_DIM:(g + 1) * IDX_DIM] for g in range(IDX_HEADS)], axis=0)
    ws = [jnp.broadcast_to(w[:, MISC_AW + g:MISC_AW + g + 1], (tq, LANES)) for g in range(IDX_HEADS)]

    def idx_body(j, carry):
        start = pl.multiple_of(j * tk, tk)
        s_all = _dot(qi_stack, ki_ref[0:IDX_DIM, pl.ds(start, tk)])
        idx = jnp.zeros((tq, tk), F32)
        for g in range(IDX_HEADS):
            idx = idx + jnp.maximum(s_all[g * tq:(g + 1) * tq], 0.0) * _lane_tile(ws[g], tk)
        bits = lax.bitcast_convert_type(idx, I32)
        key = jnp.where(bits < 0, bits ^ INT_MAX, bits)
        key = jnp.where(idx == 0.0, 0, key)
        kpos = start + kcol
        kchunk = jnp.where(kpos < l_valid, kpos // CHUNK, INT_MAX)
        key = jnp.where(kchunk <= qchunk, key, INT_MIN)
        key_scr[:, pl.ds(start, tk)] = key
        gmax_scr[...] = jnp.maximum(gmax_scr[...], key)
        return carry

    gmax_scr[...] = jnp.full((tq, tk), INT_MIN, I32)
    lax.fori_loop(0, nblk, idx_body, 0)

    @pl.when(nblk % 2 == 1)
    def _():
        key_scr[:, pl.ds(pl.multiple_of(nblk * tk, tk), tk)] = jnp.full((tq, tk), INT_MIN, I32)

    npair = (nblk + 1) // 2

    rc_coarse = min(tq, 128)
    rc_fine = min(tq, 32)

    lane_col = lax.broadcasted_iota(I32, (1, LANES), 1)

    def count(indicator, *row_args, src=key_scr, nsteps=npair, width=2 * tk, rc=rc_coarse, active=None):
        outs = []
        for ci, r0 in enumerate(range(0, tq, rc)):
            args = [a[r0:r0 + rc] for a in row_args]

            def body(j, acc, r0=r0, args=args):
                for c in range(width // LANES):
                    start = pl.multiple_of(j * width + c * LANES, LANES)
                    acc = acc + indicator(src[r0:r0 + rc, pl.ds(start, LANES)], start + lane_col, *args)
                return acc

            def sweep(body=body):
                acc = lax.fori_loop(0, nsteps, body, jnp.zeros((rc, LANES), F32))
                return jnp.broadcast_to(jnp.sum(acc, axis=1, keepdims=True), (rc, LANES))

            if active is None:
                outs.append(sweep())
            else:
                outs.append(lax.cond(active[ci], sweep, lambda: jnp.zeros((rc, LANES), F32)))
        return outs[0] if len(outs) == 1 else jnp.concatenate(outs, axis=0)

    ge = lambda kt, kp, c: jnp.where(kt >= c, 1.0, 0.0)
    n_adm = jnp.broadcast_to(jnp.minimum((qchunk + 1) * CHUNK, l_valid).astype(F32), (tq, LANES))

    def lb_body(t, ans):
        cand_u = ans | lax.shift_left(jnp.int32(1), 31 - t)
        cnt = count(ge, cand_u ^ INT_MIN, src=gmax_scr, nsteps=1, width=tk)
        return jnp.where(cnt >= ksel, cand_u, ans)

    lb_bits = 16
    lb = lax.fori_loop(0, lb_bits, lb_body, jnp.zeros((tq, LANES), I32)) ^ INT_MIN
    gm = gmax_scr[...]
    gbits = jnp.where(gm < 0, gm ^ INT_MAX, gm)
    gval = jnp.where(gm == INT_MIN, -jnp.inf, lax.bitcast_convert_type(gbits, F32))
    vmax = jnp.broadcast_to(jnp.max(gval, axis=1, keepdims=True), (tq, LANES))
    mbits = lax.bitcast_convert_type(vmax, I32)
    kmax = jnp.where(mbits < 0, mbits ^ INT_MAX, mbits)
    lo0 = jnp.maximum(lb, INT_MIN + 1)
    hi0 = jnp.minimum(kmax, INT_MAX - 1) + 1

    def open_rows(lo, hi, c_lo):
        return jnp.where(c_lo == ksel, 0.0, jnp.where(lo + 1 >= hi, 0.0, jnp.where(n_adm <= ksel, 0.0, 1.0)))

    def chunk_flags(rows01, rc):
        return tuple(jnp.max(rows01[r0:r0 + rc]) > 0.0 for r0 in range(0, tq, rc))

    def n_open(rows01):
        return jnp.sum(rows01[:, 0:1])

    def halve_twice(lo, hi, c_lo, rc, active):
        for _ in range(2):
            mid = (lo >> 1) + (hi >> 1) + (lo & hi & 1)
            cnt = count(ge, mid, rc=rc, active=active)
            take = cnt >= ksel
            lo = jnp.where(take, mid, lo)
            c_lo = jnp.where(take, cnt, c_lo)
            hi = jnp.where(take, hi, mid)
        return lo, hi, c_lo

    few_rows = 8.0

    def coarse_cond(st):
        return jnp.logical_and(st[0] < 34, st[4] > few_rows)

    def coarse_body(st):
        t, lo, hi, c_lo = st[:4]
        lo, hi, c_lo = halve_twice(lo, hi, c_lo, rc_coarse, st[5:])
        rows01 = open_rows(lo, hi, c_lo)
        return (t + 2, lo, hi, c_lo, n_open(rows01)) + chunk_flags(rows01, rc_coarse)

    def fine_cond(st):
        return jnp.logical_and(st[0] < 34, st[4] > 0.0)

    def fine_body(st):
        t, lo, hi, c_lo = st[:4]
        lo, hi, c_lo = halve_twice(lo, hi, c_lo, rc_fine, st[5:])
        rows01 = open_rows(lo, hi, c_lo)
        return (t + 2, lo, hi, c_lo, n_open(rows01)) + chunk_flags(rows01, rc_fine)

    c_lo0 = count(ge, lo0)
    rows01 = open_rows(lo0, hi0, c_lo0)
    st = lax.while_loop(coarse_cond, coarse_body,
                        (jnp.int32(0), lo0, hi0, c_lo0, n_open(rows01)) + chunk_flags(rows01, rc_coarse))
    rows01 = open_rows(*st[1:4])
    st = lax.while_loop(fine_cond, fine_body, st[:4] + (n_open(rows01),) + chunk_flags(rows01, rc_fine))
    tau, c_ge = st[1], st[3]

    tie_rows = jnp.where(c_ge > ksel, 1.0, 0.0)
    tie_chunks = chunk_flags(tie_rows, rc_fine)
    any_tie = jnp.max(tie_rows) > 0.0
    need = ksel - count(lambda kt, kp, c: jnp.where(kt > c, 1.0, 0.0), tau, rc=rc_fine, active=tie_chunks)

    def pos_body(t, p):
        cand = p | lax.shift_left(jnp.int32(1), pos_bits - 1 - t)
        cnt = count(lambda kt, kp, tv, cv: jnp.where(kt == tv, jnp.where(kp < cv, 1.0, 0.0), 0.0), tau, cand,
                    rc=rc_fine, active=tie_chunks)
        return jnp.where(cnt < need, cand, p)

    p_lim = lax.fori_loop(0, jnp.where(any_tie, pos_bits, 0), pos_body, jnp.zeros((tq, LANES), I32))
    p_lim = jnp.where(tie_rows > 0.0, p_lim, INT_MAX)

    _init_online(m_scr, acc_scr)
    q_stacks = _stack_maps(q_ref, 2)
    tau_t = _lane_tile(tau, tk)
    p_lim_t = _lane_tile(p_lim, tk)

    def att_body(j, carry):
        start = pl.multiple_of(j * tk, tk)
        kt = key_scr[:, pl.ds(start, tk)]
        bias = jnp.where(kt > tau_t, 0.0, jnp.where(kt == tau_t, jnp.where(start + kcol <= p_lim_t, 0.0, NEG), NEG))
        ss = _map_scores(q_stacks, k_ref, start, tk, tq)
        for h in range(HEADS):
            _online_update([h], [ss[h] + bias], v_ref[pl.ds(start, tk), h * LANES:(h + 1) * LANES], m_scr, acc_scr)
        return carry

    lax.fori_loop(0, nblk, att_body, 0)
    o_ref[...] = jnp.concatenate([_normalized(acc_scr[h]) for h in range(HEADS)], axis=1).astype(BF16)


def _dsa(qi, misc, q, ki, k, v, *, q_off, l_valid, tq):
    b, t, _ = q.shape
    lp = v.shape[1]
    tk = 512
    assert lp % (2 * tk) == 0 and t % tq == 0
    ksel = min(TOPK_MAX, l_valid // 4)
    qtok = lambda width: pl.BlockSpec((None, tq, width), lambda bi, i: (bi, i, 0))
    res = lambda width: pl.BlockSpec((None, lp, width), lambda bi, i: (bi, 0, 0), pipeline_mode=pl.Buffered(1))
    res_t = lambda rows: pl.BlockSpec((None, rows, lp), lambda bi, i: (bi, 0, 0), pipeline_mode=pl.Buffered(1))
    kern = functools.partial(_dsa_kernel, tq=tq, tk=tk, q_off=q_off, l_valid=l_valid, ksel=float(ksel),
                             pos_bits=int(lp - 1).bit_length())
    return pl.pallas_call(
        kern,
        grid=(b, t // tq),
        in_specs=[qtok(IDX_HEADS * IDX_DIM), qtok(MISC_W), qtok(HEADS * LANES), res_t(MISC_W), res_t(GROUP_W),
                  res(HEADS * LANES)],
        out_specs=qtok(GROUP_W),
        out_shape=jax.ShapeDtypeStruct((b, t, GROUP_W), BF16),
        scratch_shapes=[pltpu.VMEM((tq, lp), I32), pltpu.VMEM((tq, tk), I32)] + _attn_scratch(HEADS, tq),
        compiler_params=_cparams(2),
        name="dsa",
    )(qi, misc, q, ki, k, v)


def _fox_kernel(q_ref, cq_ref, k_ref, v_ref, ck_ref, o_ref, m_scr, acc_scr, *, tq, tk, q_off, l_valid):
    i = pl.program_id(1)
    q0 = q_off + i * tq
    qpos = q0 + lax.broadcasted_iota(I32, (tq, 1), 0)
    nblk = (jnp.minimum(q0 + tq, l_valid) + tk - 1) // tk
    nfull = jnp.minimum((q0 + 1) // tk, l_valid // tk)
    kcol = lax.broadcasted_iota(I32, (1, tk), 1)
    cq = cq_ref[...]
    cqs = [jnp.broadcast_to(cq[:, h:h + 1], (tq, LANES)) for h in range(HEADS)]
    _init_online(m_scr, acc_scr)
    q_stacks = _stack_maps(q_ref, 2)

    def body(j, carry, masked):
        start = pl.multiple_of(j * tk, tk)
        if masked:
            kpos = start + kcol
            valid = kpos <= qpos
        ss = _map_scores(q_stacks, k_ref, start, tk, tq)
        for h in range(HEADS):
            s = ss[h] - ck_ref[h:h + 1, pl.ds(start, tk)]
            if masked:
                s = jnp.where(valid, s, NEG)
            _online_update([h], [s], v_ref[pl.ds(start, tk), h * LANES:(h + 1) * LANES], m_scr, acc_scr,
                           shift=cqs[h])
        return carry

    lax.fori_loop(0, nfull, functools.partial(body, masked=False), 0)
    lax.fori_loop(nfull, nblk, functools.partial(body, masked=True), 0)
    o_ref[...] = jnp.concatenate([_normalized(acc_scr[h]) for h in range(HEADS)], axis=1).astype(BF16)


def _fox(q, cq, k, v, ck_t, *, q_off, l_valid, tq):
    b, t, _ = q.shape
    lp = v.shape[1]
    tk = 1024
    assert lp % tk == 0 and t % tq == 0
    qtok = lambda width: pl.BlockSpec((None, tq, width), lambda bi, i: (bi, i, 0))
    res = lambda rows, width: pl.BlockSpec((None, rows, width), lambda bi, i: (bi, 0, 0),
                                           pipeline_mode=pl.Buffered(1))
    kern = functools.partial(_fox_kernel, tq=tq, tk=tk, q_off=q_off, l_valid=l_valid)
    return pl.pallas_call(
        kern,
        grid=(b, t // tq),
        in_specs=[qtok(HEADS * LANES), qtok(HEADS), res(GROUP_W, lp), res(lp, HEADS * LANES), res(HEADS, lp)],
        out_specs=qtok(GROUP_W),
        out_shape=jax.ShapeDtypeStruct((b, t, GROUP_W), BF16),
        scratch_shapes=_attn_scratch(HEADS, tq),
        compiler_params=_cparams(2),
        name="fox",
    )(q, cq, k, v, ck_t)


def _diff_kernel(q_ref, k_ref, v_ref, lam_ref, gsub_ref, o_ref, m_scr, acc_scr,
                 *, tq, tk, q_off, l_valid, out_scale):
    i = pl.program_id(1)
    q0 = q_off + i * tq
    qchunk = (q0 + lax.broadcasted_iota(I32, (tq, 1), 0)) // CHUNK
    n_end = jnp.minimum(((q0 + tq - 1) // CHUNK + 1) * CHUNK, l_valid)
    nblk = (n_end + tk - 1) // tk
    nfull = jnp.minimum(((q0 // CHUNK + 1) * CHUNK) // tk, l_valid // tk)
    kcol = lax.broadcasted_iota(I32, (1, tk), 1)
    _init_online(m_scr, acc_scr)
    q_stacks = _stack_maps(q_ref, 4)

    def body(j, carry, masked):
        start = pl.multiple_of(j * tk, tk)
        if masked:
            kpos = start + kcol
            valid = jnp.where(kpos < l_valid, kpos // CHUNK, INT_MAX) <= qchunk
        ss = _map_scores(q_stacks, k_ref, start, tk, tq)
        if masked:
            ss = [jnp.where(valid, s, NEG) for s in ss]
        for h in range(HEADS):
            _online_update([2 * h, 2 * h + 1], ss[2 * h:2 * h + 2], v_ref[pl.ds(start, tk), h * LANES:(h + 1) * LANES],
                           m_scr, acc_scr)
        return carry

    lax.fori_loop(0, nfull, functools.partial(body, masked=False), 0)
    lax.fori_loop(nfull, nblk, functools.partial(body, masked=True), 0)

    lv = lam_ref[...]
    lam = (jnp.exp(jnp.sum(lv[0:1] * lv[1:2], axis=1, keepdims=True))
           - jnp.exp(jnp.sum(lv[2:3] * lv[3:4], axis=1, keepdims=True)) + (1.0 - out_scale))
    outs = []
    for h in range(HEADS):
        o = _normalized(acc_scr[2 * h]) - lam * _normalized(acc_scr[2 * h + 1])
        outs.append(_rms(o, gsub_ref[...]) * out_scale)
    o_ref[...] = jnp.concatenate(outs, axis=1).astype(BF16)


def _diff(q, k, v, lam_vecs, g_sub, *, q_off, l_valid, tq, lam_init):
    b, t, _ = q.shape
    lp = v.shape[1]
    tk = 512
    assert lp % tk == 0 and t % tq == 0
    qtok = lambda width: pl.BlockSpec((None, tq, width), lambda bi, i: (bi, i, 0))
    res = lambda rows, width: pl.BlockSpec((None, rows, width), lambda bi, i: (bi, 0, 0),
                                           pipeline_mode=pl.Buffered(1))
    const = lambda shape: pl.BlockSpec(shape, lambda bi, i: (0, 0))
    kern = functools.partial(_diff_kernel, tq=tq, tk=tk, q_off=q_off, l_valid=l_valid, out_scale=1.0 - lam_init)
    return pl.pallas_call(
        kern,
        grid=(b, t // tq),
        in_specs=[qtok(2 * HEADS * LANES), res(GROUP_W, lp), res(lp, HEADS * LANES), const((4, DIFF_DIM)),
                  const((1, HEAD_DIM))],
        out_specs=qtok(GROUP_W),
        out_shape=jax.ShapeDtypeStruct((b, t, GROUP_W), BF16),
        scratch_shapes=_attn_scratch(2 * HEADS, tq),
        compiler_params=_cparams(2),
        name="diff",
    )(q, k, v, lam_vecs, g_sub.reshape(1, HEAD_DIM))


def _relbias_kernel(rb_ref, o_ref, *, tq):
    w = BAND_PREV + tq
    pw = 2 * w
    j = lax.broadcasted_iota(I32, (8, pw), 1)
    off = jnp.where(j < w, j, j - pw)
    idx = jnp.clip(BAND_PREV - off, -REL_CLIP, REL_CLIP) + REL_CLIP

    def body(r, accs):
        return tuple(jnp.where(idx == r, rb_ref[h, r], accs[h]) for h in range(HEADS))

    profs = lax.fori_loop(0, 2 * REL_CLIP + 1, body, tuple(jnp.zeros((8, pw), F32) for _ in range(HEADS)))
    qc = BAND_CHUNKS + lax.broadcasted_iota(I32, (tq, w), 0) // CHUNK
    kc = lax.broadcasted_iota(I32, (tq, w), 1) // CHUNK
    for h in range(HEADS):
        full = jnp.broadcast_to(profs[h][0:1], (tq, pw))
        t = pltpu.roll(full, 0, 1, stride=1, stride_axis=0)[:, 0:w]
        t = jnp.where(kc <= qc, t, NEG)
        o_ref[h] = jnp.where(kc >= qc - BAND_CHUNKS, t, NEG)


def _relbias(rel_bias_l, tq):
    return pl.pallas_call(
        functools.partial(_relbias_kernel, tq=tq),
        in_specs=[pl.BlockSpec(memory_space=pltpu.SMEM)],
        out_specs=pl.BlockSpec(memory_space=pltpu.VMEM),
        out_shape=jax.ShapeDtypeStruct((HEADS, tq, BAND_PREV + tq), F32),
        compiler_params=pltpu.CompilerParams(vmem_limit_bytes=VMEM_LIMIT_BYTES),
        name="relbias",
    )(rel_bias_l.astype(F32))


def _band_kernel(q_ref, kp_ref, kc_ref, vp_ref, vc_ref, bias_ref, o_ref, k_scr, v_scr, *, tq, first_has_no_prev):
    i = pl.program_id(1)
    w = BAND_PREV + tq
    k_scr[0:BAND_PREV, :] = kp_ref[...]
    k_scr[BAND_PREV:w, :] = kc_ref[...]
    v_scr[0:BAND_PREV, :] = vp_ref[...]
    v_scr[BAND_PREV:w, :] = vc_ref[...]
    kcol = lax.broadcasted_iota(I32, (1, w), 1)
    first = jnp.where(i == 0, BAND_PREV, 0) if first_has_no_prev else 0
    outs = []
    for h in range(HEADS):
        hs = slice(h * HEAD_DIM, (h + 1) * HEAD_DIM)
        s = _dot_nt(q_ref[:, hs], k_scr[:, hs]) + bias_ref[h]
        if first_has_no_prev:
            s = jnp.where(kcol >= first, s, NEG)
        e = jnp.exp(s - jnp.max(s, axis=1, keepdims=True))
        o = _dot(e.astype(BF16), v_scr[:, hs])
        outs.append(o / jnp.sum(e, axis=1, keepdims=True))
    o_ref[...] = jnp.concatenate(outs, axis=1).astype(BF16)


def _band(q, k_prev, k_cur, v_prev, v_cur, bias, *, tq, prev_from_cur):
    b, t, _ = q.shape
    assert t % tq == 0 and (not prev_from_cur or tq == BAND_PREV)
    qtok = pl.BlockSpec((None, tq, GROUP_W), lambda bi, i: (bi, i, 0))
    if prev_from_cur:
        prev = pl.BlockSpec((None, BAND_PREV, GROUP_W), lambda bi, i: (bi, jnp.maximum(i - 1, 0), 0))
    else:
        prev = pl.BlockSpec((None, BAND_PREV, GROUP_W), lambda bi, i: (bi, 0, 0))
    bspec = pl.BlockSpec((HEADS, tq, BAND_PREV + tq), lambda bi, i: (0, 0, 0))
    kern = functools.partial(_band_kernel, tq=tq, first_has_no_prev=prev_from_cur)
    return pl.pallas_call(
        kern,
        grid=(b, t // tq),
        in_specs=[qtok, prev, qtok, prev, qtok, bspec],
        out_specs=qtok,
        out_shape=jax.ShapeDtypeStruct((b, t, GROUP_W), BF16),
        scratch_shapes=[pltpu.VMEM((BAND_PREV + tq, GROUP_W), BF16), pltpu.VMEM((BAND_PREV + tq, GROUP_W), BF16)],
        compiler_params=_cparams(2),
        name="band",
    )(q, k_prev, k_cur, v_prev, v_cur, bias)


def _memkv_kernel(m_ref, g_ref, wk_ref, wv_ref, kf_o, vf_o, kb_o, vb_o):
    hb = _rms(m_ref[...], g_ref[...]).astype(BF16)
    k = _dot(hb, wk_ref[...])
    v = _dot(hb, wv_ref[...])
    kf_o[...] = k
    vf_o[...] = v
    kb_o[...] = k.astype(BF16)
    vb_o[...] = v.astype(BF16)


def _memkv(mem, g, wk, wv):
    b, m, d = mem.shape
    xw = wk.shape[1]
    tok = lambda width: pl.BlockSpec((None, m, width), lambda bi: (bi, 0, 0))
    const = lambda shape: pl.BlockSpec(shape, lambda bi: (0, 0))
    return pl.pallas_call(
        _memkv_kernel,
        grid=(b,),
        in_specs=[tok(d), const((1, d)), const((d, xw)), const((d, xw))],
        out_specs=[tok(xw)] * 4,
        out_shape=[jax.ShapeDtypeStruct((b, m, xw), F32)] * 2 + [jax.ShapeDtypeStruct((b, m, xw), BF16)] * 2,
        compiler_params=_cparams(1),
        name="memkv",
    )(mem, g.reshape(1, d), wk, wv)


def _mix_cross_kernel(x_ref, oa_ref, ob_ref, oc_ref, od_ref, wout_ref, g_ref, wq_ref, mk_ref, mv_ref, wo_ref, o_ref):
    x = x_ref[...]
    for n, oref in enumerate((oa_ref, ob_ref, oc_ref, od_ref)):
        x = x + _dot(oref[...], wout_ref[n * GROUP_W:(n + 1) * GROUP_W, :])
    hb = _rms(x, g_ref[...]).astype(BF16)
    q = _dot(hb, wq_ref[...])
    outs = []
    for h in range(X_HEADS):
        hs = slice(h * X_HEAD_DIM, (h + 1) * X_HEAD_DIM)
        s = _dot_nt(q[:, hs].astype(BF16), mk_ref[:, hs]) * (X_HEAD_DIM ** -0.5)
        e = jnp.exp(s - jnp.max(s, axis=1, keepdims=True))
        p = e / jnp.sum(e, axis=1, keepdims=True)
        outs.append(_dot(p.astype(BF16), mv_ref[:, hs]))
    o = jnp.concatenate(outs, axis=1).astype(BF16)
    o_ref[...] = x + _dot(o, wo_ref[...])


def _mix_cross(x, oa, ob, oc, od, w_out, g, wq, mk, mv, wo):
    b, t, d = x.shape
    tm = min(512, t)
    m, xw = mk.shape[1], mk.shape[2]
    tok = lambda width: pl.BlockSpec((None, tm, width), lambda bi, i: (bi, i, 0))
    const = lambda shape: pl.BlockSpec(shape, lambda bi, i: (0, 0))
    mem = pl.BlockSpec((None, m, xw), lambda bi, i: (bi, 0, 0))
    return pl.pallas_call(
        _mix_cross_kernel,
        grid=(b, t // tm),
        in_specs=[tok(d)] + [tok(GROUP_W)] * 4 + [const(w_out.shape), const((1, d)), const(wq.shape), mem, mem,
                                                   const(wo.shape)],
        out_specs=tok(d),
        out_shape=jax.ShapeDtypeStruct((b, t, d), F32),
        compiler_params=_cparams(2),
        name="mix_cross",
    )(x, oa, ob, oc, od, w_out, g.reshape(1, d), wq, mk, mv, wo)


def _ffn_kernel(x_ref, g_ref, prev_ref, wup_ref, cw_ref, cb_ref, wdn_ref, o_ref, conv_ref, halo_scr, acc_scr,
                *, tm, d_ff, fc):
    i = pl.program_id(1)

    @pl.when(i == 0)
    def _():
        halo_scr[...] = prev_ref[...]

    x = x_ref[...]
    hb = _rms(x, g_ref[...]).astype(BF16)
    row = lax.broadcasted_iota(I32, (tm, fc), 0)
    for c in range(d_ff // fc):
        cs = slice(c * fc, (c + 1) * fc)
        g = _dot(hb, wup_ref[:, cs])
        u = _dot(hb, wup_ref[:, d_ff + c * fc:d_ff + (c + 1) * fc])
        h0 = halo_scr[0:1, cs]
        h1 = halo_scr[1:2, cs]
        gm1 = jnp.where(row == 0, h1, pltpu.roll(g, 1, 0))
        gm2 = jnp.where(row == 0, h0, jnp.where(row == 1, h1, pltpu.roll(g, 2, 0)))
        a = cb_ref[:, cs] + gm2 * cw_ref[0:1, cs]
        a = a + gm1 * cw_ref[1:2, cs]
        a = a + g * cw_ref[2:3, cs]
        y = (a * (1.0 / (1.0 + jnp.exp(-a)))) * u
        f = _dot(y.astype(BF16), wdn_ref[cs, :])
        if c == 0:
            acc_scr[...] = f
        else:
            acc_scr[...] += f
        tail = g[tm - (CONV_W - 1):tm, :]
        halo_scr[:, cs] = tail
        conv_ref[:, cs] = tail
    o_ref[...] = x + acc_scr[...]


def _ffn(x, g, prev, w_up, conv_w, conv_b, w_down):
    b, t, d = x.shape
    d_ff = w_down.shape[0]
    tm = min(512, t)
    fc = 256
    assert d_ff % fc == 0 and t % tm == 0 and tm >= CONV_W - 1
    tok = pl.BlockSpec((None, tm, d), lambda bi, i: (bi, i, 0))
    const = lambda shape: pl.BlockSpec(shape, lambda bi, i: (0, 0))
    st = pl.BlockSpec((None, CONV_W - 1, d_ff), lambda bi, i: (bi, 0, 0))
    kern = functools.partial(_ffn_kernel, tm=tm, d_ff=d_ff, fc=fc)
    return pl.pallas_call(
        kern,
        grid=(b, t // tm),
        in_specs=[tok, const((1, d)), st, const(w_up.shape), const(conv_w.shape), const((1, d_ff)),
                  const(w_down.shape)],
        out_specs=[tok, st],
        out_shape=[jax.ShapeDtypeStruct((b, t, d), F32), jax.ShapeDtypeStruct((b, CONV_W - 1, d_ff), F32)],
        scratch_shapes=[pltpu.VMEM((CONV_W - 1, d_ff), F32), pltpu.VMEM((tm, d), F32)],
        compiler_params=_cparams(2),
        name="ffn",
    )(x, g.reshape(1, d), prev, w_up, conv_w, conv_b.reshape(1, d_ff), w_down)


def _rmsnorm_kernel(x_ref, g_ref, o_ref):
    o_ref[...] = _rms(x_ref[...], g_ref[...])


def _rmsnorm(x, g):
    b, t, d = x.shape
    tm = min(1024, t)
    tok = pl.BlockSpec((None, tm, d), lambda bi, i: (bi, i, 0))
    return pl.pallas_call(
        _rmsnorm_kernel,
        grid=(b, t // tm),
        in_specs=[tok, pl.BlockSpec((1, d), lambda bi, i: (0, 0))],
        out_specs=tok,
        out_shape=jax.ShapeDtypeStruct((b, t, d), F32),
        compiler_params=_cparams(2),
        name="final_norm",
    )(x, g.reshape(1, d))


def _pad_rows(a, rows):
    return a if a.shape[1] == rows else jnp.pad(a, ((0, 0), (0, rows - a.shape[1]), (0, 0)))


def _round_up(n, m):
    return (n + m - 1) // m * m


def _value_slabs_host(v):
    b, l = v.shape[:2]
    one = jnp.zeros((b, l, HEADS, HEAD_DIM), BF16).at[..., 0].set(1.0)
    return jnp.concatenate([v.astype(BF16), one], axis=-1).reshape(b, l, HEADS * LANES)


def _mixers(pr, keys, *, q_off, l_valid, tq_a, tq_bd, band_args, band_tq, prev_from_cur, lam_vecs, g_sub,
            lam_init, logf_all):
    lp = keys["av"].shape[1]
    oa = _dsa(pr["aqi"], pr["misc"], pr["aq"], keys["aki"], keys["ak"], keys["av"],
              q_off=q_off, l_valid=l_valid, tq=tq_a)
    lc = _round_up(logf_all.shape[1], 1024)
    c_t = _cumsum_t(jnp.swapaxes(_pad_rows(logf_all, lc), 1, 2))
    t = pr["bq"].shape[1]
    cq = jnp.swapaxes(c_t[:, :, q_off:q_off + t], 1, 2)
    ck_t = c_t[:, :, :lp] if lc >= lp else jnp.pad(c_t, ((0, 0), (0, 0), (0, lp - lc)))
    ob = _fox(pr["bq"], cq, keys["bk"], keys["bv"], ck_t, q_off=q_off, l_valid=l_valid, tq=tq_bd)
    oc = _band(pr["cq"], *band_args, tq=band_tq, prev_from_cur=prev_from_cur)
    od = _diff(pr["dq"], keys["dk"], keys["dv"], lam_vecs, g_sub, q_off=q_off, l_valid=l_valid, tq=tq_bd,
               lam_init=lam_init)
    return oa, ob, oc, od


def kernel(x_prompt, x_sample, mem_prompt, cache_a_k, cache_a_v, cache_a_kidx, cache_b_k, cache_b_v,
           cache_b_logf, cache_c_k, cache_c_v, cache_d_k, cache_d_v, cache_mem_k, cache_mem_v,
           state_ffn_conv, g_mix, w_in, b_forget, rel_bias, lam_q1, lam_k1, lam_q2, lam_k2, g_sub, w_out,
           g_cross, g_mem, w_xq, w_xk, w_xv, w_xo, g_ffn, w_up, conv_w, conv_b, w_down, g_final):
    depth = w_in.shape[0]
    bp, s_len, d_model = x_prompt.shape
    bs, t_len, _ = x_sample.shape
    past = cache_a_k.shape[2]
    d_ff = w_down.shape[1]
    c_keep = min(BAND_PREV, s_len)
    l_s = past + t_len
    lp_s = _round_up(l_s, 1024)

    pos_p = jnp.arange(s_len, dtype=jnp.int32)
    pos_s = jnp.tile(past + jnp.arange(t_len, dtype=jnp.int32), bs)
    h64 = HEAD_DIM // ROT_FRACTION // 2
    h32 = DIFF_DIM // ROT_FRACTION // 2
    t64_p, t32_p = _rope_table(pos_p, h64, HEAD_DIM), _rope_table(pos_p, h32, DIFF_DIM)
    t64_s, t32_s = _rope_table(pos_s, h64, HEAD_DIM), _rope_table(pos_s, h32, DIFF_DIM)

    flat = lambda a: a.reshape(a.shape[0], a.shape[1], -1)
    heads = lambda a, hd=HEAD_DIM: a.reshape(a.shape[0], a.shape[1], a.shape[2] // hd, hd)

    xp, xs = x_prompt, x_sample
    p_states, s_states = [], []
    for l in range(depth):
        lam_init = 0.8 - 0.6 * math.exp(-0.3 * l)
        w_pack, bfp = _pack_w_in(w_in[l], b_forget[l])
        lam_vecs = jnp.stack([lam_q1[l], lam_k1[l], lam_q2[l], lam_k2[l]]).astype(F32)
        bias = _relbias(rel_bias[l], BAND_PREV)
        bias_s = bias[:, :t_len, :BAND_PREV + t_len]
        w_out_b, wq_b, wo_b = w_out[l].astype(BF16), w_xq[l].astype(BF16), w_xo[l].astype(BF16)
        w_up_b, w_dn_b = w_up[l].astype(BF16), w_down[l].astype(BF16)

        pr = _proj(xp, g_mix[l], w_pack, bfp, t64_p, t32_p, c_keep)
        logf = pr["misc"][:, :, MISC_BF:MISC_BF + HEADS]
        oa, ob, oc, od = _mixers(
            pr, pr, q_off=0, l_valid=s_len, tq_a=256, tq_bd=512,
            band_args=(pr["ck"], pr["ck"], pr["cv"], pr["cv"], bias), band_tq=BAND_PREV, prev_from_cur=True,
            lam_vecs=lam_vecs, g_sub=g_sub[l], lam_init=lam_init, logf_all=logf)
        mkf, mvf, mkb, mvb = _memkv(mem_prompt, g_mem[l], w_xk[l].astype(BF16), w_xv[l].astype(BF16))
        xp = _mix_cross(xp, oa, ob, oc, od, w_out_b, g_cross[l], wq_b, mkb, mvb, wo_b)
        xp, conv_p = _ffn(xp, g_ffn[l], jnp.zeros((bp, CONV_W - 1, d_ff), F32), w_up_b, conv_w[l], conv_b[l], w_dn_b)
        p_states.append((heads(pr["akf"]), heads(pr["avf"]), pr["misc"][:, :, :IDX_DIM],
                         heads(pr["bkf"]), heads(pr["bvf"]), logf,
                         heads(pr["ckf"]), heads(pr["cvf"]), heads(pr["dkf"]), heads(pr["dvf"]),
                         conv_p, heads(mkf, X_HEAD_DIM), heads(mvf, X_HEAD_DIM)))

        prs = _proj(xs.reshape(1, bs * t_len, d_model), g_mix[l], w_pack, bfp, t64_s, t32_s, bs * t_len)
        prs = {n: a.reshape(bs, t_len, a.shape[-1]) for n, a in prs.items() if n not in ("ak", "aki", "bk", "dk")}
        cat_t = lambda cache, new: jnp.swapaxes(
            _pad_rows(jnp.concatenate([flat(cache), new], axis=1), lp_s).astype(BF16), 1, 2)
        kidx = jnp.concatenate([cache_a_kidx[l], prs["misc"][:, :, :IDX_DIM]], axis=1)
        kidx = jnp.pad(kidx, ((0, 0), (0, lp_s - l_s), (0, MISC_W - IDX_DIM)))
        catv = lambda cache, new: _pad_rows(jnp.concatenate([_value_slabs_host(cache), new], axis=1), lp_s)
        keys = {"aki": jnp.swapaxes(kidx.astype(BF16), 1, 2),
                "ak": cat_t(cache_a_k[l], prs["akf"]), "av": catv(cache_a_v[l], prs["av"]),
                "bk": cat_t(cache_b_k[l], prs["bkf"]), "bv": catv(cache_b_v[l], prs["bv"]),
                "dk": cat_t(cache_d_k[l], prs["dkf"]), "dv": catv(cache_d_v[l], prs["dv"])}
        logf_s = prs["misc"][:, :, MISC_BF:MISC_BF + HEADS]
        logf_all = jnp.concatenate([cache_b_logf[l].astype(F32), logf_s], axis=1)
        oa, ob, oc, od = _mixers(
            prs, keys, q_off=past, l_valid=l_s, tq_a=t_len, tq_bd=t_len,
            band_args=(flat(cache_c_k[l]).astype(BF16), prs["ck"], flat(cache_c_v[l]).astype(BF16), prs["cv"],
                       bias_s),
            band_tq=t_len, prev_from_cur=False,
            lam_vecs=lam_vecs, g_sub=g_sub[l], lam_init=lam_init, logf_all=logf_all)
        xs = _mix_cross(xs, oa, ob, oc, od, w_out_b, g_cross[l], wq_b,
                        flat(cache_mem_k[l]).astype(BF16), flat(cache_mem_v[l]).astype(BF16), wo_b)
        xs, conv_s = _ffn(xs, g_ffn[l], state_ffn_conv[l], w_up_b, conv_w[l], conv_b[l], w_dn_b)
        s_states.append((heads(prs["akf"]), heads(prs["avf"]), prs["misc"][:, :, :IDX_DIM],
                         heads(prs["bkf"]), heads(prs["bvf"]), logf_s,
                         heads(prs["ckf"]), heads(prs["cvf"]), heads(prs["dkf"]), heads(prs["dvf"]), conv_s))

    y_prompt = _rmsnorm(xp, g_final)
    y_sample = _rmsnorm(xs, g_final)
    p_out = [jnp.stack(z, axis=0) for z in zip(*p_states)]
    s_out = [jnp.stack(z, axis=0) for z in zip(*s_states)]
    return (y_prompt, y_sample, *p_out, *s_out)
```

```python
import functools
import math

import numpy as np
import jax
import jax.numpy as jnp
from jax import lax
from jax.experimental import pallas as pl
from jax.experimental.pallas import tpu as pltpu

F32 = jnp.float32
BF16 = jnp.bfloat16
I32 = jnp.int32

CHUNK = 64
HEAD_DIM = 64
HEADS = 4
GROUP_W = HEADS * HEAD_DIM
ROT_FRACTION = 4
ROPE_THETA = 500000.0
IDX_HEADS = 8
IDX_DIM = 64
TOPK_MAX = 256
BAND_CHUNKS = 8
BAND_KEYS = (BAND_CHUNKS + 1) * CHUNK
BAND_PREV = BAND_CHUNKS * CHUNK
REL_CLIP = 128
DIFF_DIM = HEAD_DIM // 2
X_HEADS = 4
X_HEAD_DIM = 128
CONV_W = 3
EPS = 1e-6

LANES = 128
VMEM_LIMIT_BYTES = 56 * 1024 * 1024

NEG = -1e30
INT_MIN = -2147483648
INT_MAX = 2147483647

MISC_W = LANES
MISC_AW = IDX_DIM
MISC_BF = IDX_DIM + IDX_HEADS
PROJ_COLS = 3 * GROUP_W + IDX_HEADS * IDX_DIM + MISC_W + 9 * GROUP_W


def _cparams(n_axes):
    return pltpu.CompilerParams(dimension_semantics=("arbitrary",) * n_axes,
                                vmem_limit_bytes=VMEM_LIMIT_BYTES)


def _rms(x, g):
    return (x * lax.rsqrt(jnp.mean(x * x, axis=-1, keepdims=True) + EPS)) * g


def _dot(a, b):
    return jnp.dot(a, b, preferred_element_type=F32)


def _dot_nt(a, b):
    return lax.dot_general(a, b, (((1,), (1,)), ((), ())), preferred_element_type=F32)


def _rope(z, tab, half):
    w = z.shape[1]
    reps = w // LANES
    rep = lambda t: t if reps == 1 else jnp.concatenate([t] * reps, axis=1)
    c = rep(tab[:, 0:LANES])
    s1 = rep(tab[:, LANES:2 * LANES])
    s2 = rep(tab[:, 2 * LANES:3 * LANES])
    return z * c + pltpu.roll(z, w - half, 1) * s1 + pltpu.roll(z, half, 1) * s2


def _slab_masked(z, widths):
    lane = lax.broadcasted_iota(I32, (z.shape[0], LANES), 1)
    outs = []
    for j in range(GROUP_W // widths):
        lo = (j * widths) % LANES
        zs = z[:, (j * widths) // LANES * LANES:((j * widths) // LANES + 1) * LANES]
        outs.append(jnp.where(lane < lo, 0.0, jnp.where(lane < lo + widths, zs, 0.0)))
    return jnp.concatenate(outs, axis=1)


def _value_slabs(z):
    one = jnp.where(lax.broadcasted_iota(I32, (z.shape[0], HEAD_DIM), 1) == 0, 1.0, 0.0)
    return jnp.concatenate([t for h in range(HEADS) for t in (z[:, h * HEAD_DIM:(h + 1) * HEAD_DIM], one)], axis=1)


def _proj_kernel(x_ref, g_ref, w_ref, bf_ref, t64_ref, t32_ref,
                 aq_o, ak_o, av_o, aqi_o, aki_o, bq_o, bk_o, bv_o, cq_o, ck_o, cv_o, dq_o, dk_o, dv_o,
                 akf_o, avf_o, misc_o, bkf_o, bvf_o, ckf_o, cvf_o, dkf_o, dvf_o):
    hb = _rms(x_ref[...], g_ref[...]).astype(BF16)
    t64 = t64_ref[...]
    t32 = t32_ref[...]
    h64 = HEAD_DIM // ROT_FRACTION // 2
    h32 = DIFF_DIM // ROT_FRACTION // 2
    qscale = HEAD_DIM ** -0.5
    col = [0]

    def mm(width):
        c0 = col[0]
        col[0] = c0 + width
        return _dot(hb, w_ref[:, c0:c0 + width])

    z = _rope(mm(GROUP_W), t64, h64)
    aq_o[...] = _slab_masked(z * qscale, HEAD_DIM).astype(BF16)
    z = _rope(mm(GROUP_W), t64, h64)
    akf_o[...] = z
    ak_o[...] = z.T.astype(BF16)
    z = mm(GROUP_W)
    avf_o[...] = z
    av_o[...] = _value_slabs(z).astype(BF16)
    z = _rope(mm(IDX_HEADS * IDX_DIM), t64, h64)
    aqi_o[...] = (z * (IDX_DIM ** -0.5)).astype(BF16)
    z = mm(MISC_W)
    r = _rope(z, t64, h64)
    lane = lax.broadcasted_iota(I32, z.shape, 1)
    zf = z + bf_ref[...]
    logsig = jnp.minimum(zf, 0.0) - jnp.log1p(jnp.exp(-jnp.abs(zf)))
    misc_o[...] = jnp.where(lane < MISC_AW, r, jnp.where(lane < MISC_BF, z * (IDX_HEADS ** -0.5), logsig))
    aki_o[...] = jnp.where(lane < MISC_AW, r, 0.0).T.astype(BF16)
    z = mm(GROUP_W)
    bq_o[...] = _slab_masked(z * qscale, HEAD_DIM).astype(BF16)
    z = mm(GROUP_W)
    bkf_o[...] = z
    bk_o[...] = z.T.astype(BF16)
    z = mm(GROUP_W)
    bvf_o[...] = z
    bv_o[...] = _value_slabs(z).astype(BF16)
    z = mm(GROUP_W)
    cq_o[...] = (z * qscale).astype(BF16)
    z = mm(GROUP_W)
    ckf_o[...] = z
    ck_o[...] = z.astype(BF16)
    z = mm(GROUP_W)
    cvf_o[...] = z
    cv_o[...] = z.astype(BF16)
    z = _rope(mm(GROUP_W), t32, h32)
    dq_o[...] = _slab_masked(z * (DIFF_DIM ** -0.5), DIFF_DIM).astype(BF16)
    z = _rope(mm(GROUP_W), t32, h32)
    dkf_o[...] = z
    dk_o[...] = z.T.astype(BF16)
    z = mm(GROUP_W)
    dvf_o[...] = z
    dv_o[...] = _value_slabs(z).astype(BF16)


def _rope_table(pos, half, period):
    inv = ROPE_THETA ** (-jnp.arange(half, dtype=F32) / half)
    ang = pos.astype(F32)[:, None] * inv[None, :]
    cos, sin = jnp.cos(ang), jnp.sin(ang)
    t = pos.shape[0]
    rest = period - 2 * half
    zh = jnp.zeros((t, half), F32)
    zr = jnp.zeros((t, rest), F32)
    c = jnp.concatenate([cos, cos, jnp.ones((t, rest), F32)], axis=1)
    s1 = jnp.concatenate([-sin, zh, zr], axis=1)
    s2 = jnp.concatenate([zh, sin, zr], axis=1)
    rep = lambda a: jnp.tile(a, (1, LANES // period))
    return jnp.concatenate([rep(c), rep(s1), rep(s2)], axis=1)


def _pack_w_in(w_in_l, b_forget_l):
    sizes = (GROUP_W, GROUP_W, GROUP_W, IDX_HEADS * IDX_DIM, IDX_DIM, IDX_HEADS,
             GROUP_W, GROUP_W, GROUP_W, HEADS, GROUP_W, GROUP_W, GROUP_W, GROUP_W, GROUP_W, GROUP_W)
    pts = [int(p) for p in np.cumsum(sizes)[:-1]]
    (aq, ak, av, aqi, aki, aw, bq, bk, bv, bf, cq, ck, cv, dq, dk, dv) = jnp.split(w_in_l, pts, axis=1)
    d = w_in_l.shape[0]
    pad = jnp.zeros((d, MISC_W - IDX_DIM - IDX_HEADS - HEADS), w_in_l.dtype)
    misc = jnp.concatenate([aki, aw, bf, pad], axis=1)
    w = jnp.concatenate([aq, ak, av, aqi, misc, bq, bk, bv, cq, ck, cv, dq, dk, dv], axis=1).astype(BF16)
    bfp = jnp.zeros((1, MISC_W), F32).at[0, MISC_BF:MISC_BF + HEADS].set(b_forget_l.astype(F32))
    return w, bfp


def _proj(x, g, w, bfp, t64, t32, c_keep):
    b, t, d = x.shape
    tm = min(512, t)
    nt = t // tm
    nkeep = c_keep // tm
    tok = lambda width: pl.BlockSpec((None, tm, width), lambda bi, i: (bi, i, 0))
    ctok = pl.BlockSpec((None, tm, GROUP_W), lambda bi, i: (bi, jnp.maximum(i - (nt - nkeep), 0), 0))
    const = lambda shape: pl.BlockSpec(shape, lambda bi, i: (0, 0))
    tab = pl.BlockSpec((tm, 3 * LANES), lambda bi, i: (i, 0))
    sds = lambda width, dt, rows=t: jax.ShapeDtypeStruct((b, rows, width), dt)
    bf_names = ["aq", "ak", "av", "aqi", "aki", "bq", "bk", "bv", "cq", "ck", "cv", "dq", "dk", "dv"]
    qx_w, vx_w = HEADS * LANES, HEADS * LANES
    bf_w = [qx_w, GROUP_W, vx_w, IDX_HEADS * IDX_DIM, MISC_W, qx_w, GROUP_W, vx_w, GROUP_W, GROUP_W, GROUP_W,
            2 * qx_w, GROUP_W, vx_w]
    f_names = ["akf", "avf", "misc", "bkf", "bvf", "ckf", "cvf", "dkf", "dvf"]
    f_w = [GROUP_W, GROUP_W, MISC_W] + [GROUP_W] * 6
    out_shape = [sds(wd, BF16) for wd in bf_w]
    out_specs = [tok(wd) for wd in bf_w]
    for n, wd in zip(bf_names, bf_w):
        if n in ("ak", "aki", "bk", "dk"):
            out_shape[bf_names.index(n)] = jax.ShapeDtypeStruct((b, wd, t), BF16)
            out_specs[bf_names.index(n)] = pl.BlockSpec((None, wd, tm), lambda bi, i: (bi, 0, i))
    for n, wd in zip(f_names, f_w):
        if n in ("ckf", "cvf"):
            out_shape.append(sds(wd, F32, c_keep))
            out_specs.append(ctok)
        else:
            out_shape.append(sds(wd, F32))
            out_specs.append(tok(wd))
    outs = pl.pallas_call(
        _proj_kernel,
        grid=(b, nt),
        in_specs=[tok(d), const((1, d)), const((d, PROJ_COLS)), const((1, MISC_W)), tab, tab],
        out_specs=out_specs,
        out_shape=out_shape,
        compiler_params=_cparams(2),
        name="proj",
    )(x, g.reshape(1, d), w, bfp, t64, t32)
    return dict(zip(bf_names + f_names, outs))


def _cumsum_kernel(x_ref, o_ref, *, nb):
    x = x_ref[...]
    lane = lax.broadcasted_iota(I32, x.shape, 1)
    d = 1
    while d < LANES:
        x = x + jnp.where(lane >= d, pltpu.roll(x, d, 1), 0.0)
        d *= 2
    row = lax.broadcasted_iota(I32, x.shape, 0) % nb
    tot = jnp.broadcast_to(x[:, LANES - 1:LANES], x.shape)
    exc = jnp.where(row >= 1, pltpu.roll(tot, 1, 0), 0.0)
    d = 1
    while d < nb:
        exc = exc + jnp.where(row >= d, pltpu.roll(exc, d, 0), 0.0)
        d *= 2
    o_ref[...] = x + exc


def _cumsum_t(logf_t):
    b, h, l = logf_t.shape
    nb = l // LANES
    x = logf_t.reshape(b, h * nb, LANES)
    spec = pl.BlockSpec((None, h * nb, LANES), lambda bi: (bi, 0, 0))
    out = pl.pallas_call(
        functools.partial(_cumsum_kernel, nb=nb),
        grid=(b,),
        in_specs=[spec],
        out_specs=spec,
        out_shape=jax.ShapeDtypeStruct(x.shape, F32),
        compiler_params=_cparams(1),
        name="cumsum",
    )(x)
    return out.reshape(b, h, l)


def _lane_tile(a, width):
    return a if width == LANES else jnp.concatenate([a] * (width // LANES), axis=1)


def _online_update(js, ss, vx, m_scr, acc_scr, shift=None):
    tq = ss[0].shape[0]
    ps, alphas = [], []
    for j, s in zip(js, ss):
        m_prev = m_scr[j]
        rmax = jnp.max(s, axis=1, keepdims=True)
        if shift is not None:
            rmax = rmax + shift
        m_new = jnp.maximum(m_prev, rmax)
        sub = m_new if shift is None else m_new - shift
        ps.append(jnp.exp(s - _lane_tile(sub, s.shape[1])).astype(BF16))
        alphas.append(jnp.exp(m_prev - m_new))
        m_scr[j] = m_new
    pv = _dot(ps[0] if len(ps) == 1 else jnp.concatenate(ps, axis=0), vx)
    for i, j in enumerate(js):
        acc_scr[j] = alphas[i] * acc_scr[j] + pv[i * tq:(i + 1) * tq]


def _stack_maps(q_ref, per_pair):
    n_pairs = q_ref.shape[1] // LANES // per_pair
    return [jnp.concatenate([q_ref[:, m * LANES:(m + 1) * LANES] for m in range(p * per_pair, (p + 1) * per_pair)],
                            axis=0) for p in range(n_pairs)]


def _map_scores(q_stacks, kt_ref, start, tk, tq):
    out = []
    for p, qs in enumerate(q_stacks):
        sp = _dot(qs, kt_ref[p * LANES:(p + 1) * LANES, pl.ds(start, tk)])
        out += [sp[i * tq:(i + 1) * tq] for i in range(qs.shape[0] // tq)]
    return out


def _init_online(m_scr, acc_scr):
    m_scr[...] = jnp.full(m_scr.shape, NEG, F32)
    acc_scr[...] = jnp.zeros(acc_scr.shape, F32)


def _normalized(acc):
    return acc[:, 0:HEAD_DIM] / acc[:, HEAD_DIM:HEAD_DIM + 1]


def _attn_scratch(n_state, tq):
    return [pltpu.VMEM((n_state, tq, LANES), F32), pltpu.VMEM((n_state, tq, LANES), F32)]


def _dsa_kernel(qi_ref, w_ref, q_ref, ki_ref, k_ref, v_ref, o_ref, key_scr, gmax_scr, m_scr, acc_scr,
                *, tq, tk, q_off, l_valid, ksel, pos_bits):
    i = pl.program_id(1)
    q0 = q_off + i * tq
    qchunk = (q0 + lax.broadcasted_iota(I32, (tq, 1), 0)) // CHUNK
    n_end = jnp.minimum(((q0 + tq - 1) // CHUNK + 1) * CHUNK, l_valid)
    nblk = (n_end + tk - 1) // tk
    kcol = lax.broadcasted_iota(I32, (1, tk), 1)

    qi = qi_ref[...]
    w = w_ref[...]
    qi_stack = jnp.concatenate([qi[:, g * IDX_DIM:(g + 1) * IDX_DIM] for g in range(IDX_HEADS)], axis=0)
    ws = [jnp.broadcast_to(w[:, MISC_AW + g:MISC_AW + g + 1], (tq, LANES)) for g in range(IDX_HEADS)]

    def idx_body(j, carry):
        start = pl.multiple_of(j * tk, tk)
        s_all = _dot(qi_stack, ki_ref[0:IDX_DIM, pl.ds(start, tk)])
        idx = jnp.zeros((tq, tk), F32)
        for g in range(IDX_HEADS):
            idx = idx + jnp.maximum(s_all[g * tq:(g + 1) * tq], 0.0) * _lane_tile(ws[g], tk)
        bits = lax.bitcast_convert_type(idx, I32)
        key = jnp.where(bits < 0, bits ^ INT_MAX, bits)
        key = jnp.where(idx == 0.0, 0, key)
        kpos = start + kcol
        kchunk = jnp.where(kpos < l_valid, kpos // CHUNK, INT_MAX)
        key = jnp.where(kchunk <= qchunk, key, INT_MIN)
        key_scr[:, pl.ds(start, tk)] = key
        gmax_scr[...] = jnp.maximum(gmax_scr[...], key)
        return carry

    gmax_scr[...] = jnp.full((tq, tk), INT_MIN, I32)
    lax.fori_loop(0, nblk, idx_body, 0)

    @pl.when(nblk % 2 == 1)
    def _():
        key_scr[:, pl.ds(pl.multiple_of(nblk * tk, tk), tk)] = jnp.full((tq, tk), INT_MIN, I32)

    npair = (nblk + 1) // 2

    rc_coarse = min(tq, 128)
    rc_fine = min(tq, 32)

    lane_col = lax.broadcasted_iota(I32, (1, LANES), 1)

    def count(indicator, *row_args, src=key_scr, nsteps=npair, width=2 * tk, rc=rc_coarse, active=None):
        outs = []
        for ci, r0 in enumerate(range(0, tq, rc)):
            args = [a[r0:r0 + rc] for a in row_args]

            def body(j, acc, r0=r0, args=args):
                for c in range(width // LANES):
                    start = pl.multiple_of(j * width + c * LANES, LANES)
                    acc = acc + indicator(src[r0:r0 + rc, pl.ds(start, LANES)], start + lane_col, *args)
                return acc

            def sweep(body=body):
                acc = lax.fori_loop(0, nsteps, body, jnp.zeros((rc, LANES), F32))
                return jnp.broadcast_to(jnp.sum(acc, axis=1, keepdims=True), (rc, LANES))

            if active is None:
                outs.append(sweep())
            else:
                outs.append(lax.cond(active[ci], sweep, lambda: jnp.zeros((rc, LANES), F32)))
        return outs[0] if len(outs) == 1 else jnp.concatenate(outs, axis=0)

    ge = lambda kt, kp, c: jnp.where(kt >= c, 1.0, 0.0)
    n_adm = jnp.broadcast_to(jnp.minimum((qchunk + 1) * CHUNK, l_valid).astype(F32), (tq, LANES))

    def lb_body(t, ans):
        cand_u = ans | lax.shift_left(jnp.int32(1), 31 - t)
        cnt = count(ge, cand_u ^ INT_MIN, src=gmax_scr, nsteps=1, width=tk)
        return jnp.where(cnt >= ksel, cand_u, ans)

    lb_bits = 16
    lb = lax.fori_loop(0, lb_bits, lb_body, jnp.zeros((tq, LANES), I32)) ^ INT_MIN
    gm = gmax_scr[...]
    gbits = jnp.where(gm < 0, gm ^ INT_MAX, gm)
    gval = jnp.where(gm == INT_MIN, -jnp.inf, lax.bitcast_convert_type(gbits, F32))
    vmax = jnp.broadcast_to(jnp.max(gval, axis=1, keepdims=True), (tq, LANES))
    mbits = lax.bitcast_convert_type(vmax, I32)
    kmax = jnp.where(mbits < 0, mbits ^ INT_MAX, mbits)
    lo0 = jnp.maximum(lb, INT_MIN + 1)
    hi0 = jnp.minimum(kmax, INT_MAX - 1) + 1

    def open_rows(lo, hi, c_lo):
        return jnp.where(c_lo == ksel, 0.0, jnp.where(lo + 1 >= hi, 0.0, jnp.where(n_adm <= ksel, 0.0, 1.0)))

    def chunk_flags(rows01, rc):
        return tuple(jnp.max(rows01[r0:r0 + rc]) > 0.0 for r0 in range(0, tq, rc))

    def n_open(rows01):
        return jnp.sum(rows01[:, 0:1])

    def halve_twice(lo, hi, c_lo, rc, active):
        for _ in range(2):
            mid = (lo >> 1) + (hi >> 1) + (lo & hi & 1)
            cnt = count(ge, mid, rc=rc, active=active)
            take = cnt >= ksel
            lo = jnp.where(take, mid, lo)
            c_lo = jnp.where(take, cnt, c_lo)
            hi = jnp.where(take, hi, mid)
        return lo, hi, c_lo

    few_rows = 8.0

    def coarse_cond(st):
        return jnp.logical_and(st[0] < 34, st[4] > few_rows)

    def coarse_body(st):
        t, lo, hi, c_lo = st[:4]
        lo, hi, c_lo = halve_twice(lo, hi, c_lo, rc_coarse, st[5:])
        rows01 = open_rows(lo, hi, c_lo)
        return (t + 2, lo, hi, c_lo, n_open(rows01)) + chunk_flags(rows01, rc_coarse)

    def fine_cond(st):
        return jnp.logical_and(st[0] < 34, st[4] > 0.0)

    def fine_body(st):
        t, lo, hi, c_lo = st[:4]
        lo, hi, c_lo = halve_twice(lo, hi, c_lo, rc_fine, st[5:])
        rows01 = open_rows(lo, hi, c_lo)
        return (t + 2, lo, hi, c_lo, n_open(rows01)) + chunk_flags(rows01, rc_fine)

    c_lo0 = count(ge, lo0)
    rows01 = open_rows(lo0, hi0, c_lo0)
    st = lax.while_loop(coarse_cond, coarse_body,
                        (jnp.int32(0), lo0, hi0, c_lo0, n_open(rows01)) + chunk_flags(rows01, rc_coarse))
    rows01 = open_rows(*st[1:4])
    st = lax.while_loop(fine_cond, fine_body, st[:4] + (n_open(rows01),) + chunk_flags(rows01, rc_fine))
    tau, c_ge = st[1], st[3]

    tie_rows = jnp.where(c_ge > ksel, 1.0, 0.0)
    tie_chunks = chunk_flags(tie_rows, rc_fine)
    any_tie = jnp.max(tie_rows) > 0.0
    need = ksel - count(lambda kt, kp, c: jnp.where(kt > c, 1.0, 0.0), tau, rc=rc_fine, active=tie_chunks)

    def pos_body(t, p):
        cand = p | lax.shift_left(jnp.int32(1), pos_bits - 1 - t)
        cnt = count(lambda kt, kp, tv, cv: jnp.where(kt == tv, jnp.where(kp < cv, 1.0, 0.0), 0.0), tau, cand,
                    rc=rc_fine, active=tie_chunks)
        return jnp.where(cnt < need, cand, p)

    p_lim = lax.fori_loop(0, jnp.where(any_tie, pos_bits, 0), pos_body, jnp.zeros((tq, LANES), I32))
    p_lim = jnp.where(tie_rows > 0.0, p_lim, INT_MAX)

    _init_online(m_scr, acc_scr)
    q_stacks = _stack_maps(q_ref, 2)
    ta = 2 * tk
    tau_t = _lane_tile(tau, ta)
    p_lim_t = _lane_tile(p_lim, ta)
    acol = lax.broadcasted_iota(I32, (1, ta), 1)

    def att_body(j, carry):
        start = pl.multiple_of(j * ta, ta)
        kt = key_scr[:, pl.ds(start, ta)]
        bias = jnp.where(kt > tau_t, 0.0, jnp.where(kt == tau_t, jnp.where(start + acol <= p_lim_t, 0.0, NEG), NEG))
        ss = _map_scores(q_stacks, k_ref, start, ta, tq)
        for h in range(HEADS):
            _online_update([h], [ss[h] + bias], v_ref[pl.ds(start, ta), h * LANES:(h + 1) * LANES], m_scr, acc_scr)
        return carry

    lax.fori_loop(0, npair, att_body, 0)
    o_ref[...] = jnp.concatenate([_normalized(acc_scr[h]) for h in range(HEADS)], axis=1).astype(BF16)


def _dsa(qi, misc, q, ki, k, v, *, q_off, l_valid, tq):
    b, t, _ = q.shape
    lp = v.shape[1]
    tk = 512
    assert lp % (2 * tk) == 0 and t % tq == 0
    ksel = min(TOPK_MAX, l_valid // 4)
    qtok = lambda width: pl.BlockSpec((None, tq, width), lambda bi, i: (bi, i, 0))
    res = lambda width: pl.BlockSpec((None, lp, width), lambda bi, i: (bi, 0, 0), pipeline_mode=pl.Buffered(1))
    res_t = lambda rows: pl.BlockSpec((None, rows, lp), lambda bi, i: (bi, 0, 0), pipeline_mode=pl.Buffered(1))
    kern = functools.partial(_dsa_kernel, tq=tq, tk=tk, q_off=q_off, l_valid=l_valid, ksel=float(ksel),
                             pos_bits=int(lp - 1).bit_length())
    return pl.pallas_call(
        kern,
        grid=(b, t // tq),
        in_specs=[qtok(IDX_HEADS * IDX_DIM), qtok(MISC_W), qtok(HEADS * LANES), res_t(MISC_W), res_t(GROUP_W),
                  res(HEADS * LANES)],
        out_specs=qtok(GROUP_W),
        out_shape=jax.ShapeDtypeStruct((b, t, GROUP_W), BF16),
        scratch_shapes=[pltpu.VMEM((tq, lp), I32), pltpu.VMEM((tq, tk), I32)] + _attn_scratch(HEADS, tq),
        compiler_params=_cparams(2),
        name="dsa",
    )(qi, misc, q, ki, k, v)


def _fox_kernel(q_ref, cq_ref, k_ref, v_ref, ck_ref, o_ref, m_scr, acc_scr, *, tq, tk, q_off, l_valid):
    i = pl.program_id(1)
    q0 = q_off + i * tq
    qpos = q0 + lax.broadcasted_iota(I32, (tq, 1), 0)
    nblk = (jnp.minimum(q0 + tq, l_valid) + tk - 1) // tk
    nfull = jnp.minimum((q0 + 1) // tk, l_valid // tk)
    kcol = lax.broadcasted_iota(I32, (1, tk), 1)
    cq = cq_ref[...]
    cqs = [jnp.broadcast_to(cq[:, h:h + 1], (tq, LANES)) for h in range(HEADS)]
    _init_online(m_scr, acc_scr)
    q_stacks = _stack_maps(q_ref, 2)

    def body(j, carry, masked):
        start = pl.multiple_of(j * tk, tk)
        if masked:
            kpos = start + kcol
            valid = kpos <= qpos
        ss = _map_scores(q_stacks, k_ref, start, tk, tq)
        for h in range(HEADS):
            s = ss[h] - ck_ref[h:h + 1, pl.ds(start, tk)]
            if masked:
                s = jnp.where(valid, s, NEG)
            _online_update([h], [s], v_ref[pl.ds(start, tk), h * LANES:(h + 1) * LANES], m_scr, acc_scr,
                           shift=cqs[h])
        return carry

    lax.fori_loop(0, nfull, functools.partial(body, masked=False), 0)
    lax.fori_loop(nfull, nblk, functools.partial(body, masked=True), 0)
    o_ref[...] = jnp.concatenate([_normalized(acc_scr[h]) for h in range(HEADS)], axis=1).astype(BF16)


def _fox(q, cq, k, v, ck_t, *, q_off, l_valid, tq):
    b, t, _ = q.shape
    lp = v.shape[1]
    tk = 1024
    assert lp % tk == 0 and t % tq == 0
    qtok = lambda width: pl.BlockSpec((None, tq, width), lambda bi, i: (bi, i, 0))
    res = lambda rows, width: pl.BlockSpec((None, rows, width), lambda bi, i: (bi, 0, 0),
                                           pipeline_mode=pl.Buffered(1))
    kern = functools.partial(_fox_kernel, tq=tq, tk=tk, q_off=q_off, l_valid=l_valid)
    return pl.pallas_call(
        kern,
        grid=(b, t // tq),
        in_specs=[qtok(HEADS * LANES), qtok(HEADS), res(GROUP_W, lp), res(lp, HEADS * LANES), res(HEADS, lp)],
        out_specs=qtok(GROUP_W),
        out_shape=jax.ShapeDtypeStruct((b, t, GROUP_W), BF16),
        scratch_shapes=_attn_scratch(HEADS, tq),
        compiler_params=_cparams(2),
        name="fox",
    )(q, cq, k, v, ck_t)


def _diff_kernel(q_ref, k_ref, v_ref, lam_ref, gsub_ref, o_ref, m_scr, acc_scr,
                 *, tq, tk, q_off, l_valid, out_scale):
    i = pl.program_id(1)
    q0 = q_off + i * tq
    qchunk = (q0 + lax.broadcasted_iota(I32, (tq, 1), 0)) // CHUNK
    n_end = jnp.minimum(((q0 + tq - 1) // CHUNK + 1) * CHUNK, l_valid)
    nblk = (n_end + tk - 1) // tk
    nfull = jnp.minimum(((q0 // CHUNK + 1) * CHUNK) // tk, l_valid // tk)
    kcol = lax.broadcasted_iota(I32, (1, tk), 1)
    _init_online(m_scr, acc_scr)
    q_stacks = _stack_maps(q_ref, 4)

    def body(j, carry, masked):
        start = pl.multiple_of(j * tk, tk)
        if masked:
            kpos = start + kcol
            valid = jnp.where(kpos < l_valid, kpos // CHUNK, INT_MAX) <= qchunk
        ss = _map_scores(q_stacks, k_ref, start, tk, tq)
        if masked:
            ss = [jnp.where(valid, s, NEG) for s in ss]
        for h in range(HEADS):
            _online_update([2 * h, 2 * h + 1], ss[2 * h:2 * h + 2], v_ref[pl.ds(start, tk), h * LANES:(h + 1) * LANES],
                           m_scr, acc_scr)
        return carry

    lax.fori_loop(0, nfull, functools.partial(body, masked=False), 0)
    lax.fori_loop(nfull, nblk, functools.partial(body, masked=True), 0)

    lv = lam_ref[...]
    lam = (jnp.exp(jnp.sum(lv[0:1] * lv[1:2], axis=1, keepdims=True))
           - jnp.exp(jnp.sum(lv[2:3] * lv[3:4], axis=1, keepdims=True)) + (1.0 - out_scale))
    outs = []
    for h in range(HEADS):
        o = _normalized(acc_scr[2 * h]) - lam * _normalized(acc_scr[2 * h + 1])
        outs.append(_rms(o, gsub_ref[...]) * out_scale)
    o_ref[...] = jnp.concatenate(outs, axis=1).astype(BF16)


def _diff(q, k, v, lam_vecs, g_sub, *, q_off, l_valid, tq, lam_init):
    b, t, _ = q.shape
    lp = v.shape[1]
    tk = 512
    assert lp % tk == 0 and t % tq == 0
    qtok = lambda width: pl.BlockSpec((None, tq, width), lambda bi, i: (bi, i, 0))
    res = lambda rows, width: pl.BlockSpec((None, rows, width), lambda bi, i: (bi, 0, 0),
                                           pipeline_mode=pl.Buffered(1))
    const = lambda shape: pl.BlockSpec(shape, lambda bi, i: (0, 0))
    kern = functools.partial(_diff_kernel, tq=tq, tk=tk, q_off=q_off, l_valid=l_valid, out_scale=1.0 - lam_init)
    return pl.pallas_call(
        kern,
        grid=(b, t // tq),
        in_specs=[qtok(2 * HEADS * LANES), res(GROUP_W, lp), res(lp, HEADS * LANES), const((4, DIFF_DIM)),
                  const((1, HEAD_DIM))],
        out_specs=qtok(GROUP_W),
        out_shape=jax.ShapeDtypeStruct((b, t, GROUP_W), BF16),
        scratch_shapes=_attn_scratch(2 * HEADS, tq),
        compiler_params=_cparams(2),
        name="diff",
    )(q, k, v, lam_vecs, g_sub.reshape(1, HEAD_DIM))


def _relbias_kernel(rb_ref, o_ref, *, tq):
    w = BAND_PREV + tq
    pw = 2 * w
    j = lax.broadcasted_iota(I32, (8, pw), 1)
    off = jnp.where(j < w, j, j - pw)
    idx = jnp.clip(BAND_PREV - off, -REL_CLIP, REL_CLIP) + REL_CLIP

    def body(r, accs):
        return tuple(jnp.where(idx == r, rb_ref[h, r], accs[h]) for h in range(HEADS))

    profs = lax.fori_loop(0, 2 * REL_CLIP + 1, body, tuple(jnp.zeros((8, pw), F32) for _ in range(HEADS)))
    qc = BAND_CHUNKS + lax.broadcasted_iota(I32, (tq, w), 0) // CHUNK
    kc = lax.broadcasted_iota(I32, (tq, w), 1) // CHUNK
    for h in range(HEADS):
        full = jnp.broadcast_to(profs[h][0:1], (tq, pw))
        t = pltpu.roll(full, 0, 1, stride=1, stride_axis=0)[:, 0:w]
        t = jnp.where(kc <= qc, t, NEG)
        o_ref[h] = jnp.where(kc >= qc - BAND_CHUNKS, t, NEG)


def _relbias(rel_bias_l, tq):
    return pl.pallas_call(
        functools.partial(_relbias_kernel, tq=tq),
        in_specs=[pl.BlockSpec(memory_space=pltpu.SMEM)],
        out_specs=pl.BlockSpec(memory_space=pltpu.VMEM),
        out_shape=jax.ShapeDtypeStruct((HEADS, tq, BAND_PREV + tq), F32),
        compiler_params=pltpu.CompilerParams(vmem_limit_bytes=VMEM_LIMIT_BYTES),
        name="relbias",
    )(rel_bias_l.astype(F32))


def _band_kernel(q_ref, kp_ref, kc_ref, vp_ref, vc_ref, bias_ref, o_ref, k_scr, v_scr, *, tq, first_has_no_prev):
    i = pl.program_id(1)
    w = BAND_PREV + tq
    k_scr[0:BAND_PREV, :] = kp_ref[...]
    k_scr[BAND_PREV:w, :] = kc_ref[...]
    v_scr[0:BAND_PREV, :] = vp_ref[...]
    v_scr[BAND_PREV:w, :] = vc_ref[...]
    kcol = lax.broadcasted_iota(I32, (1, w), 1)
    first = jnp.where(i == 0, BAND_PREV, 0) if first_has_no_prev else 0
    outs = []
    for h in range(HEADS):
        hs = slice(h * HEAD_DIM, (h + 1) * HEAD_DIM)
        s = _dot_nt(q_ref[:, hs], k_scr[:, hs]) + bias_ref[h]
        if first_has_no_prev:
            s = jnp.where(kcol >= first, s, NEG)
        e = jnp.exp(s - jnp.max(s, axis=1, keepdims=True))
        o = _dot(e.astype(BF16), v_scr[:, hs])
        outs.append(o / jnp.sum(e, axis=1, keepdims=True))
    o_ref[...] = jnp.concatenate(outs, axis=1).astype(BF16)


def _band(q, k_prev, k_cur, v_prev, v_cur, bias, *, tq, prev_from_cur):
    b, t, _ = q.shape
    assert t % tq == 0 and (not prev_from_cur or tq == BAND_PREV)
    qtok = pl.BlockSpec((None, tq, GROUP_W), lambda bi, i: (bi, i, 0))
    if prev_from_cur:
        prev = pl.BlockSpec((None, BAND_PREV, GROUP_W), lambda bi, i: (bi, jnp.maximum(i - 1, 0), 0))
    else:
        prev = pl.BlockSpec((None, BAND_PREV, GROUP_W), lambda bi, i: (bi, 0, 0))
    bspec = pl.BlockSpec((HEADS, tq, BAND_PREV + tq), lambda bi, i: (0, 0, 0))
    kern = functools.partial(_band_kernel, tq=tq, first_has_no_prev=prev_from_cur)
    return pl.pallas_call(
        kern,
        grid=(b, t // tq),
        in_specs=[qtok, prev, qtok, prev, qtok, bspec],
        out_specs=qtok,
        out_shape=jax.ShapeDtypeStruct((b, t, GROUP_W), BF16),
        scratch_shapes=[pltpu.VMEM((BAND_PREV + tq, GROUP_W), BF16), pltpu.VMEM((BAND_PREV + tq, GROUP_W), BF16)],
        compiler_params=_cparams(2),
        name="band",
    )(q, k_prev, k_cur, v_prev, v_cur, bias)


def _memkv_kernel(m_ref, g_ref, wk_ref, wv_ref, kf_o, vf_o, kb_o, vb_o):
    hb = _rms(m_ref[...], g_ref[...]).astype(BF16)
    k = _dot(hb, wk_ref[...])
    v = _dot(hb, wv_ref[...])
    kf_o[...] = k
    vf_o[...] = v
    kb_o[...] = k.astype(BF16)
    vb_o[...] = v.astype(BF16)


def _memkv(mem, g, wk, wv):
    b, m, d = mem.shape
    xw = wk.shape[1]
    tok = lambda width: pl.BlockSpec((None, m, width), lambda bi: (bi, 0, 0))
    const = lambda shape: pl.BlockSpec(shape, lambda bi: (0, 0))
    return pl.pallas_call(
        _memkv_kernel,
        grid=(b,),
        in_specs=[tok(d), const((1, d)), const((d, xw)), const((d, xw))],
        out_specs=[tok(xw)] * 4,
        out_shape=[jax.ShapeDtypeStruct((b, m, xw), F32)] * 2 + [jax.ShapeDtypeStruct((b, m, xw), BF16)] * 2,
        compiler_params=_cparams(1),
        name="memkv",
    )(mem, g.reshape(1, d), wk, wv)


def _mix_cross_kernel(x_ref, oa_ref, ob_ref, oc_ref, od_ref, wout_ref, g_ref, wq_ref, mk_ref, mv_ref, wo_ref, o_ref):
    x = x_ref[...]
    for n, oref in enumerate((oa_ref, ob_ref, oc_ref, od_ref)):
        x = x + _dot(oref[...], wout_ref[n * GROUP_W:(n + 1) * GROUP_W, :])
    hb = _rms(x, g_ref[...]).astype(BF16)
    q = _dot(hb, wq_ref[...])
    outs = []
    for h in range(X_HEADS):
        hs = slice(h * X_HEAD_DIM, (h + 1) * X_HEAD_DIM)
        s = _dot_nt(q[:, hs].astype(BF16), mk_ref[:, hs]) * (X_HEAD_DIM ** -0.5)
        e = jnp.exp(s - jnp.max(s, axis=1, keepdims=True))
        p = e / jnp.sum(e, axis=1, keepdims=True)
        outs.append(_dot(p.astype(BF16), mv_ref[:, hs]))
    o = jnp.concatenate(outs, axis=1).astype(BF16)
    o_ref[...] = x + _dot(o, wo_ref[...])


def _mix_cross(x, oa, ob, oc, od, w_out, g, wq, mk, mv, wo):
    b, t, d = x.shape
    tm = min(512, t)
    m, xw = mk.shape[1], mk.shape[2]
    tok = lambda width: pl.BlockSpec((None, tm, width), lambda bi, i: (bi, i, 0))
    const = lambda shape: pl.BlockSpec(shape, lambda bi, i: (0, 0))
    mem = pl.BlockSpec((None, m, xw), lambda bi, i: (bi, 0, 0))
    return pl.pallas_call(
        _mix_cross_kernel,
        grid=(b, t // tm),
        in_specs=[tok(d)] + [tok(GROUP_W)] * 4 + [const(w_out.shape), const((1, d)), const(wq.shape), mem, mem,
                                                   const(wo.shape)],
        out_specs=tok(d),
        out_shape=jax.ShapeDtypeStruct((b, t, d), F32),
        compiler_params=_cparams(2),
        name="mix_cross",
    )(x, oa, ob, oc, od, w_out, g.reshape(1, d), wq, mk, mv, wo)


def _ffn_kernel(x_ref, g_ref, prev_ref, wup_ref, cw_ref, cb_ref, wdn_ref, o_ref, conv_ref, halo_scr, acc_scr,
                *, tm, d_ff, fc):
    i = pl.program_id(1)

    @pl.when(i == 0)
    def _():
        halo_scr[...] = prev_ref[...]

    x = x_ref[...]
    hb = _rms(x, g_ref[...]).astype(BF16)
    row = lax.broadcasted_iota(I32, (tm, fc), 0)
    for c in range(d_ff // fc):
        cs = slice(c * fc, (c + 1) * fc)
        g = _dot(hb, wup_ref[:, cs])
        u = _dot(hb, wup_ref[:, d_ff + c * fc:d_ff + (c + 1) * fc])
        h0 = halo_scr[0:1, cs]
        h1 = halo_scr[1:2, cs]
        gm1 = jnp.where(row == 0, h1, pltpu.roll(g, 1, 0))
        gm2 = jnp.where(row == 0, h0, jnp.where(row == 1, h1, pltpu.roll(g, 2, 0)))
        a = cb_ref[:, cs] + gm2 * cw_ref[0:1, cs]
        a = a + gm1 * cw_ref[1:2, cs]
        a = a + g * cw_ref[2:3, cs]
        y = (a * (1.0 / (1.0 + jnp.exp(-a)))) * u
        f = _dot(y.astype(BF16), wdn_ref[cs, :])
        if c == 0:
            acc_scr[...] = f
        else:
            acc_scr[...] += f
        tail = g[tm - (CONV_W - 1):tm, :]
        halo_scr[:, cs] = tail
        conv_ref[:, cs] = tail
    o_ref[...] = x + acc_scr[...]


def _ffn(x, g, prev, w_up, conv_w, conv_b, w_down):
    b, t, d = x.shape
    d_ff = w_down.shape[0]
    tm = min(512, t)
    fc = 256
    assert d_ff % fc == 0 and t % tm == 0 and tm >= CONV_W - 1
    tok = pl.BlockSpec((None, tm, d), lambda bi, i: (bi, i, 0))
    const = lambda shape: pl.BlockSpec(shape, lambda bi, i: (0, 0))
    st = pl.BlockSpec((None, CONV_W - 1, d_ff), lambda bi, i: (bi, 0, 0))
    kern = functools.partial(_ffn_kernel, tm=tm, d_ff=d_ff, fc=fc)
    return pl.pallas_call(
        kern,
        grid=(b, t // tm),
        in_specs=[tok, const((1, d)), st, const(w_up.shape), const(conv_w.shape), const((1, d_ff)),
                  const(w_down.shape)],
        out_specs=[tok, st],
        out_shape=[jax.ShapeDtypeStruct((b, t, d), F32), jax.ShapeDtypeStruct((b, CONV_W - 1, d_ff), F32)],
        scratch_shapes=[pltpu.VMEM((CONV_W - 1, d_ff), F32), pltpu.VMEM((tm, d), F32)],
        compiler_params=_cparams(2),
        name="ffn",
    )(x, g.reshape(1, d), prev, w_up, conv_w, conv_b.reshape(1, d_ff), w_down)


def _rmsnorm_kernel(x_ref, g_ref, o_ref):
    o_ref[...] = _rms(x_ref[...], g_ref[...])


def _rmsnorm(x, g):
    b, t, d = x.shape
    tm = min(1024, t)
    tok = pl.BlockSpec((None, tm, d), lambda bi, i: (bi, i, 0))
    return pl.pallas_call(
        _rmsnorm_kernel,
        grid=(b, t // tm),
        in_specs=[tok, pl.BlockSpec((1, d), lambda bi, i: (0, 0))],
        out_specs=tok,
        out_shape=jax.ShapeDtypeStruct((b, t, d), F32),
        compiler_params=_cparams(2),
        name="final_norm",
    )(x, g.reshape(1, d))


def _pad_rows(a, rows):
    return a if a.shape[1] == rows else jnp.pad(a, ((0, 0), (0, rows - a.shape[1]), (0, 0)))


def _round_up(n, m):
    return (n + m - 1) // m * m


def _value_slabs_host(v):
    b, l = v.shape[:2]
    one = jnp.zeros((b, l, HEADS, HEAD_DIM), BF16).at[..., 0].set(1.0)
    return jnp.concatenate([v.astype(BF16), one], axis=-1).reshape(b, l, HEADS * LANES)


def _mixers(pr, keys, *, q_off, l_valid, tq_a, tq_bd, band_args, band_tq, prev_from_cur, lam_vecs, g_sub,
            lam_init, logf_all):
    lp = keys["av"].shape[1]
    oa = _dsa(pr["aqi"], pr["misc"], pr["aq"], keys["aki"], keys["ak"], keys["av"],
              q_off=q_off, l_valid=l_valid, tq=tq_a)
    lc = _round_up(logf_all.shape[1], 1024)
    c_t = _cumsum_t(jnp.swapaxes(_pad_rows(logf_all, lc), 1, 2))
    t = pr["bq"].shape[1]
    cq = jnp.swapaxes(c_t[:, :, q_off:q_off + t], 1, 2)
    ck_t = c_t[:, :, :lp] if lc >= lp else jnp.pad(c_t, ((0, 0), (0, 0), (0, lp - lc)))
    ob = _fox(pr["bq"], cq, keys["bk"], keys["bv"], ck_t, q_off=q_off, l_valid=l_valid, tq=tq_bd)
    oc = _band(pr["cq"], *band_args, tq=band_tq, prev_from_cur=prev_from_cur)
    od = _diff(pr["dq"], keys["dk"], keys["dv"], lam_vecs, g_sub, q_off=q_off, l_valid=l_valid, tq=tq_bd,
               lam_init=lam_init)
    return oa, ob, oc, od


def kernel(x_prompt, x_sample, mem_prompt, cache_a_k, cache_a_v, cache_a_kidx, cache_b_k, cache_b_v,
           cache_b_logf, cache_c_k, cache_c_v, cache_d_k, cache_d_v, cache_mem_k, cache_mem_v,
           state_ffn_conv, g_mix, w_in, b_forget, rel_bias, lam_q1, lam_k1, lam_q2, lam_k2, g_sub, w_out,
           g_cross, g_mem, w_xq, w_xk, w_xv, w_xo, g_ffn, w_up, conv_w, conv_b, w_down, g_final):
    depth = w_in.shape[0]
    bp, s_len, d_model = x_prompt.shape
    bs, t_len, _ = x_sample.shape
    past = cache_a_k.shape[2]
    d_ff = w_down.shape[1]
    c_keep = min(BAND_PREV, s_len)
    l_s = past + t_len
    lp_s = _round_up(l_s, 1024)

    pos_p = jnp.arange(s_len, dtype=jnp.int32)
    pos_s = jnp.tile(past + jnp.arange(t_len, dtype=jnp.int32), bs)
    h64 = HEAD_DIM // ROT_FRACTION // 2
    h32 = DIFF_DIM // ROT_FRACTION // 2
    t64_p, t32_p = _rope_table(pos_p, h64, HEAD_DIM), _rope_table(pos_p, h32, DIFF_DIM)
    t64_s, t32_s = _rope_table(pos_s, h64, HEAD_DIM), _rope_table(pos_s, h32, DIFF_DIM)

    flat = lambda a: a.reshape(a.shape[0], a.shape[1], -1)
    heads = lambda a, hd=HEAD_DIM: a.reshape(a.shape[0], a.shape[1], a.shape[2] // hd, hd)

    xp, xs = x_prompt, x_sample
    p_states, s_states = [], []
    for l in range(depth):
        lam_init = 0.8 - 0.6 * math.exp(-0.3 * l)
        w_pack, bfp = _pack_w_in(w_in[l], b_forget[l])
        lam_vecs = jnp.stack([lam_q1[l], lam_k1[l], lam_q2[l], lam_k2[l]]).astype(F32)
        bias = _relbias(rel_bias[l], BAND_PREV)
        bias_s = bias[:, :t_len, :BAND_PREV + t_len]
        w_out_b, wq_b, wo_b = w_out[l].astype(BF16), w_xq[l].astype(BF16), w_xo[l].astype(BF16)
        w_up_b, w_dn_b = w_up[l].astype(BF16), w_down[l].astype(BF16)

        pr = _proj(xp, g_mix[l], w_pack, bfp, t64_p, t32_p, c_keep)
        logf = pr["misc"][:, :, MISC_BF:MISC_BF + HEADS]
        oa, ob, oc, od = _mixers(
            pr, pr, q_off=0, l_valid=s_len, tq_a=256, tq_bd=512,
            band_args=(pr["ck"], pr["ck"], pr["cv"], pr["cv"], bias), band_tq=BAND_PREV, prev_from_cur=True,
            lam_vecs=lam_vecs, g_sub=g_sub[l], lam_init=lam_init, logf_all=logf)
        mkf, mvf, mkb, mvb = _memkv(mem_prompt, g_mem[l], w_xk[l].astype(BF16), w_xv[l].astype(BF16))
        xp = _mix_cross(xp, oa, ob, oc, od, w_out_b, g_cross[l], wq_b, mkb, mvb, wo_b)
        xp, conv_p = _ffn(xp, g_ffn[l], jnp.zeros((bp, CONV_W - 1, d_ff), F32), w_up_b, conv_w[l], conv_b[l], w_dn_b)
        p_states.append((heads(pr["akf"]), heads(pr["avf"]), pr["misc"][:, :, :IDX_DIM],
                         heads(pr["bkf"]), heads(pr["bvf"]), logf,
                         heads(pr["ckf"]), heads(pr["cvf"]), heads(pr["dkf"]), heads(pr["dvf"]),
                         conv_p, heads(mkf, X_HEAD_DIM), heads(mvf, X_HEAD_DIM)))

        prs = _proj(xs.reshape(1, bs * t_len, d_model), g_mix[l], w_pack, bfp, t64_s, t32_s, bs * t_len)
        prs = {n: a.reshape(bs, t_len, a.shape[-1]) for n, a in prs.items() if n not in ("ak", "aki", "bk", "dk")}
        cat_t = lambda cache, new: jnp.swapaxes(
            _pad_rows(jnp.concatenate([flat(cache), new], axis=1), lp_s).astype(BF16), 1, 2)
        kidx = jnp.concatenate([cache_a_kidx[l], prs["misc"][:, :, :IDX_DIM]], axis=1)
        kidx = jnp.pad(kidx, ((0, 0), (0, lp_s - l_s), (0, MISC_W - IDX_DIM)))
        catv = lambda cache, new: _pad_rows(jnp.concatenate([_value_slabs_host(cache), new], axis=1), lp_s)
        keys = {"aki": jnp.swapaxes(kidx.astype(BF16), 1, 2),
                "ak": cat_t(cache_a_k[l], prs["akf"]), "av": catv(cache_a_v[l], prs["av"]),
                "bk": cat_t(cache_b_k[l], prs["bkf"]), "bv": catv(cache_b_v[l], prs["bv"]),
                "dk": cat_t(cache_d_k[l], prs["dkf"]), "dv": catv(cache_d_v[l], prs["dv"])}
        logf_s = prs["misc"][:, :, MISC_BF:MISC_BF + HEADS]
        logf_all = jnp.concatenate([cache_b_logf[l].astype(F32), logf_s], axis=1)
        oa, ob, oc, od = _mixers(
            prs, keys, q_off=past, l_valid=l_s, tq_a=t_len, tq_bd=t_len,
            band_args=(flat(cache_c_k[l]).astype(BF16), prs["ck"], flat(cache_c_v[l]).astype(BF16), prs["cv"],
                       bias_s),
            band_tq=t_len, prev_from_cur=False,
            lam_vecs=lam_vecs, g_sub=g_sub[l], lam_init=lam_init, logf_all=logf_all)
        xs = _mix_cross(xs, oa, ob, oc, od, w_out_b, g_cross[l], wq_b,
                        flat(cache_mem_k[l]).astype(BF16), flat(cache_mem_v[l]).astype(BF16), wo_b)
        xs, conv_s = _ffn(xs, g_ffn[l], state_ffn_conv[l], w_up_b, conv_w[l], conv_b[l], w_dn_b)
        s_states.append((heads(prs["akf"]), heads(prs["avf"]), prs["misc"][:, :, :IDX_DIM],
                         heads(prs["bkf"]), heads(prs["bvf"]), logf_s,
                         heads(prs["ckf"]), heads(prs["cvf"]), heads(prs["dkf"]), heads(prs["dvf"]), conv_s))

    y_prompt = _rmsnorm(xp, g_final)
    y_sample = _rmsnorm(xs, g_final)
    p_out = [jnp.stack(z, axis=0) for z in zip(*p_states)]
    s_out = [jnp.stack(z, axis=0) for z in zip(*s_states)]
    return (y_prompt, y_sample, *p_out, *s_out)
```

```python
import functools
import math

import numpy as np
import jax
import jax.numpy as jnp
from jax import lax
from jax.experimental import pallas as pl
from jax.experimental.pallas import tpu as pltpu

F32 = jnp.float32
BF16 = jnp.bfloat16
I32 = jnp.int32

CHUNK = 64
HEAD_DIM = 64
HEADS = 4
GROUP_W = HEADS * HEAD_DIM
ROT_FRACTION = 4
ROPE_THETA = 500000.0
IDX_HEADS = 8
IDX_DIM = 64
TOPK_MAX = 256
BAND_CHUNKS = 8
BAND_KEYS = (BAND_CHUNKS + 1) * CHUNK
BAND_PREV = BAND_CHUNKS * CHUNK
REL_CLIP = 128
DIFF_DIM = HEAD_DIM // 2
X_HEADS = 4
X_HEAD_DIM = 128
CONV_W = 3
EPS = 1e-6

LANES = 128
VMEM_LIMIT_BYTES = 56 * 1024 * 1024

NEG = -1e30
INT_MIN = -2147483648
INT_MAX = 2147483647

MISC_W = LANES
MISC_AW = IDX_DIM
MISC_BF = IDX_DIM + IDX_HEADS
PROJ_COLS = 3 * GROUP_W + IDX_HEADS * IDX_DIM + MISC_W + 9 * GROUP_W


def _cparams(n_axes):
    return pltpu.CompilerParams(dimension_semantics=("arbitrary",) * n_axes,
                                vmem_limit_bytes=VMEM_LIMIT_BYTES)


def _rms(x, g):
    return (x * lax.rsqrt(jnp.mean(x * x, axis=-1, keepdims=True) + EPS)) * g


def _dot(a, b):
    return jnp.dot(a, b, preferred_element_type=F32)


def _dot_nt(a, b):
    return lax.dot_general(a, b, (((1,), (1,)), ((), ())), preferred_element_type=F32)


def _rope(z, tab, half):
    w = z.shape[1]
    reps = w // LANES
    rep = lambda t: t if reps == 1 else jnp.concatenate([t] * reps, axis=1)
    c = rep(tab[:, 0:LANES])
    s1 = rep(tab[:, LANES:2 * LANES])
    s2 = rep(tab[:, 2 * LANES:3 * LANES])
    return z * c + pltpu.roll(z, w - half, 1) * s1 + pltpu.roll(z, half, 1) * s2


def _slab_masked(z, widths):
    lane = lax.broadcasted_iota(I32, (z.shape[0], LANES), 1)
    outs = []
    for j in range(GROUP_W // widths):
        lo = (j * widths) % LANES
        zs = z[:, (j * widths) // LANES * LANES:((j * widths) // LANES + 1) * LANES]
        outs.append(jnp.where(lane < lo, 0.0, jnp.where(lane < lo + widths, zs, 0.0)))
    return jnp.concatenate(outs, axis=1)


def _value_slabs(z):
    one = jnp.where(lax.broadcasted_iota(I32, (z.shape[0], HEAD_DIM), 1) == 0, 1.0, 0.0)
    return jnp.concatenate([t for h in range(HEADS) for t in (z[:, h * HEAD_DIM:(h + 1) * HEAD_DIM], one)], axis=1)


def _proj_kernel(x_ref, g_ref, w_ref, bf_ref, t64_ref, t32_ref,
                 aq_o, ak_o, av_o, aqi_o, aki_o, bq_o, bk_o, bv_o, cq_o, ck_o, cv_o, dq_o, dk_o, dv_o,
                 akf_o, avf_o, misc_o, bkf_o, bvf_o, ckf_o, cvf_o, dkf_o, dvf_o):
    hb = _rms(x_ref[...], g_ref[...]).astype(BF16)
    t64 = t64_ref[...]
    t32 = t32_ref[...]
    h64 = HEAD_DIM // ROT_FRACTION // 2
    h32 = DIFF_DIM // ROT_FRACTION // 2
    qscale = HEAD_DIM ** -0.5
    col = [0]

    def mm(width):
        c0 = col[0]
        col[0] = c0 + width
        return _dot(hb, w_ref[:, c0:c0 + width])

    z = _rope(mm(GROUP_W), t64, h64)
    aq_o[...] = _slab_masked(z * qscale, HEAD_DIM).astype(BF16)
    z = _rope(mm(GROUP_W), t64, h64)
    akf_o[...] = z
    ak_o[...] = z.T.astype(BF16)
    z = mm(GROUP_W)
    avf_o[...] = z
    av_o[...] = _value_slabs(z).astype(BF16)
    z = _rope(mm(IDX_HEADS * IDX_DIM), t64, h64)
    aqi_o[...] = (z * (IDX_DIM ** -0.5)).astype(BF16)
    z = mm(MISC_W)
    r = _rope(z, t64, h64)
    lane = lax.broadcasted_iota(I32, z.shape, 1)
    zf = z + bf_ref[...]
    logsig = jnp.minimum(zf, 0.0) - jnp.log1p(jnp.exp(-jnp.abs(zf)))
    misc_o[...] = jnp.where(lane < MISC_AW, r, jnp.where(lane < MISC_BF, z * (IDX_HEADS ** -0.5), logsig))
    aki_o[...] = jnp.where(lane < MISC_AW, r, 0.0).T.astype(BF16)
    z = mm(GROUP_W)
    bq_o[...] = _slab_masked(z * qscale, HEAD_DIM).astype(BF16)
    z = mm(GROUP_W)
    bkf_o[...] = z
    bk_o[...] = z.T.astype(BF16)
    z = mm(GROUP_W)
    bvf_o[...] = z
    bv_o[...] = _value_slabs(z).astype(BF16)
    z = mm(GROUP_W)
    cq_o[...] = (z * qscale).astype(BF16)
    z = mm(GROUP_W)
    ckf_o[...] = z
    ck_o[...] = z.astype(BF16)
    z = mm(GROUP_W)
    cvf_o[...] = z
    cv_o[...] = z.astype(BF16)
    z = _rope(mm(GROUP_W), t32, h32)
    dq_o[...] = _slab_masked(z * (DIFF_DIM ** -0.5), DIFF_DIM).astype(BF16)
    z = _rope(mm(GROUP_W), t32, h32)
    dkf_o[...] = z
    dk_o[...] = z.T.astype(BF16)
    z = mm(GROUP_W)
    dvf_o[...] = z
    dv_o[...] = _value_slabs(z).astype(BF16)


def _rope_table(pos, half, period):
    inv = ROPE_THETA ** (-jnp.arange(half, dtype=F32) / half)
    ang = pos.astype(F32)[:, None] * inv[None, :]
    cos, sin = jnp.cos(ang), jnp.sin(ang)
    t = pos.shape[0]
    rest = period - 2 * half
    zh = jnp.zeros((t, half), F32)
    zr = jnp.zeros((t, rest), F32)
    c = jnp.concatenate([cos, cos, jnp.ones((t, rest), F32)], axis=1)
    s1 = jnp.concatenate([-sin, zh, zr], axis=1)
    s2 = jnp.concatenate([zh, sin, zr], axis=1)
    rep = lambda a: jnp.tile(a, (1, LANES // period))
    return jnp.concatenate([rep(c), rep(s1), rep(s2)], axis=1)


def _pack_w_in(w_in_l, b_forget_l):
    sizes = (GROUP_W, GROUP_W, GROUP_W, IDX_HEADS * IDX_DIM, IDX_DIM, IDX_HEADS,
             GROUP_W, GROUP_W, GROUP_W, HEADS, GROUP_W, GROUP_W, GROUP_W, GROUP_W, GROUP_W, GROUP_W)
    pts = [int(p) for p in np.cumsum(sizes)[:-1]]
    (aq, ak, av, aqi, aki, aw, bq, bk, bv, bf, cq, ck, cv, dq, dk, dv) = jnp.split(w_in_l, pts, axis=1)
    d = w_in_l.shape[0]
    pad = jnp.zeros((d, MISC_W - IDX_DIM - IDX_HEADS - HEADS), w_in_l.dtype)
    misc = jnp.concatenate([aki, aw, bf, pad], axis=1)
    w = jnp.concatenate([aq, ak, av, aqi, misc, bq, bk, bv, cq, ck, cv, dq, dk, dv], axis=1).astype(BF16)
    bfp = jnp.zeros((1, MISC_W), F32).at[0, MISC_BF:MISC_BF + HEADS].set(b_forget_l.astype(F32))
    return w, bfp


def _proj(x, g, w, bfp, t64, t32, c_keep):
    b, t, d = x.shape
    tm = min(512, t)
    nt = t // tm
    nkeep = c_keep // tm
    tok = lambda width: pl.BlockSpec((None, tm, width), lambda bi, i: (bi, i, 0))
    ctok = pl.BlockSpec((None, tm, GROUP_W), lambda bi, i: (bi, jnp.maximum(i - (nt - nkeep), 0), 0))
    const = lambda shape: pl.BlockSpec(shape, lambda bi, i: (0, 0))
    tab = pl.BlockSpec((tm, 3 * LANES), lambda bi, i: (i, 0))
    sds = lambda width, dt, rows=t: jax.ShapeDtypeStruct((b, rows, width), dt)
    bf_names = ["aq", "ak", "av", "aqi", "aki", "bq", "bk", "bv", "cq", "ck", "cv", "dq", "dk", "dv"]
    qx_w, vx_w = HEADS * LANES, HEADS * LANES
    bf_w = [qx_w, GROUP_W, vx_w, IDX_HEADS * IDX_DIM, MISC_W, qx_w, GROUP_W, vx_w, GROUP_W, GROUP_W, GROUP_W,
            2 * qx_w, GROUP_W, vx_w]
    f_names = ["akf", "avf", "misc", "bkf", "bvf", "ckf", "cvf", "dkf", "dvf"]
    f_w = [GROUP_W, GROUP_W, MISC_W] + [GROUP_W] * 6
    out_shape = [sds(wd, BF16) for wd in bf_w]
    out_specs = [tok(wd) for wd in bf_w]
    for n, wd in zip(bf_names, bf_w):
        if n in ("ak", "aki", "bk", "dk"):
            out_shape[bf_names.index(n)] = jax.ShapeDtypeStruct((b, wd, t), BF16)
            out_specs[bf_names.index(n)] = pl.BlockSpec((None, wd, tm), lambda bi, i: (bi, 0, i))
    for n, wd in zip(f_names, f_w):
        if n in ("ckf", "cvf"):
            out_shape.append(sds(wd, F32, c_keep))
            out_specs.append(ctok)
        else:
            out_shape.append(sds(wd, F32))
            out_specs.append(tok(wd))
    outs = pl.pallas_call(
        _proj_kernel,
        grid=(b, nt),
        in_specs=[tok(d), const((1, d)), const((d, PROJ_COLS)), const((1, MISC_W)), tab, tab],
        out_specs=out_specs,
        out_shape=out_shape,
        compiler_params=_cparams(2),
        name="proj",
    )(x, g.reshape(1, d), w, bfp, t64, t32)
    return dict(zip(bf_names + f_names, outs))


def _cumsum_kernel(x_ref, o_ref, *, nb):
    x = x_ref[...]
    lane = lax.broadcasted_iota(I32, x.shape, 1)
    d = 1
    while d < LANES:
        x = x + jnp.where(lane >= d, pltpu.roll(x, d, 1), 0.0)
        d *= 2
    row = lax.broadcasted_iota(I32, x.shape, 0) % nb
    tot = jnp.broadcast_to(x[:, LANES - 1:LANES], x.shape)
    exc = jnp.where(row >= 1, pltpu.roll(tot, 1, 0), 0.0)
    d = 1
    while d < nb:
        exc = exc + jnp.where(row >= d, pltpu.roll(exc, d, 0), 0.0)
        d *= 2
    o_ref[...] = x + exc


def _cumsum_t(logf_t):
    b, h, l = logf_t.shape
    nb = l // LANES
    x = logf_t.reshape(b, h * nb, LANES)
    spec = pl.BlockSpec((None, h * nb, LANES), lambda bi: (bi, 0, 0))
    out = pl.pallas_call(
        functools.partial(_cumsum_kernel, nb=nb),
        grid=(b,),
        in_specs=[spec],
        out_specs=spec,
        out_shape=jax.ShapeDtypeStruct(x.shape, F32),
        compiler_params=_cparams(1),
        name="cumsum",
    )(x)
    return out.reshape(b, h, l)


def _lane_tile(a, width):
    return a if width == LANES else jnp.concatenate([a] * (width // LANES), axis=1)


def _online_update(js, ss, vx, m_scr, acc_scr, shift=None):
    tq = ss[0].shape[0]
    ps, alphas = [], []
    for j, s in zip(js, ss):
        m_prev = m_scr[j]
        rmax = jnp.max(s, axis=1, keepdims=True)
        if shift is not None:
            rmax = rmax + shift
        m_new = jnp.maximum(m_prev, rmax)
        sub = m_new if shift is None else m_new - shift
        ps.append(jnp.exp(s - _lane_tile(sub, s.shape[1])).astype(BF16))
        alphas.append(jnp.exp(m_prev - m_new))
        m_scr[j] = m_new
    pv = _dot(ps[0] if len(ps) == 1 else jnp.concatenate(ps, axis=0), vx)
    for i, j in enumerate(js):
        acc_scr[j] = alphas[i] * acc_scr[j] + pv[i * tq:(i + 1) * tq]


def _stack_maps(q_ref, per_pair):
    n_pairs = q_ref.shape[1] // LANES // per_pair
    return [jnp.concatenate([q_ref[:, m * LANES:(m + 1) * LANES] for m in range(p * per_pair, (p + 1) * per_pair)],
                            axis=0) for p in range(n_pairs)]


def _map_scores(q_stacks, kt_ref, start, tk, tq):
    out = []
    for p, qs in enumerate(q_stacks):
        sp = _dot(qs, kt_ref[p * LANES:(p + 1) * LANES, pl.ds(start, tk)])
        out += [sp[i * tq:(i + 1) * tq] for i in range(qs.shape[0] // tq)]
    return out


def _init_online(m_scr, acc_scr):
    m_scr[...] = jnp.full(m_scr.shape, NEG, F32)
    acc_scr[...] = jnp.zeros(acc_scr.shape, F32)


def _normalized(acc):
    return acc[:, 0:HEAD_DIM] / acc[:, HEAD_DIM:HEAD_DIM + 1]


def _attn_scratch(n_state, tq):
    return [pltpu.VMEM((n_state, tq, LANES), F32), pltpu.VMEM((n_state, tq, LANES), F32)]


def _dsa_kernel(qi_ref, w_ref, q_ref, ki_ref, k_ref, v_ref, o_ref, key_scr, gmax_scr, m_scr, acc_scr,
                *, tq, tk, q_off, l_valid, ksel, pos_bits):
    i = pl.program_id(1)
    q0 = q_off + i * tq
    qchunk = (q0 + lax.broadcasted_iota(I32, (tq, 1), 0)) // CHUNK
    n_end = jnp.minimum(((q0 + tq - 1) // CHUNK + 1) * CHUNK, l_valid)
    nblk = (n_end + tk - 1) // tk
    kcol = lax.broadcasted_iota(I32, (1, tk), 1)

    qi = qi_ref[...]
    w = w_ref[...]
    qi_stack = jnp.concatenate([qi[:, g * IDX_DIM:(g + 1) * IDX_DIM] for g in range(IDX_HEADS)], axis=0)
    ws = [jnp.broadcast_to(w[:, MISC_AW + g:MISC_AW + g + 1], (tq, LANES)) for g in range(IDX_HEADS)]

    def idx_body(j, carry):
        start = pl.multiple_of(j * tk, tk)
        s_all = _dot(qi_stack, ki_ref[0:IDX_DIM, pl.ds(start, tk)])
        idx = jnp.zeros((tq, tk), F32)
        for g in range(IDX_HEADS):
            idx = idx + jnp.maximum(s_all[g * tq:(g + 1) * tq], 0.0) * _lane_tile(ws[g], tk)
        bits = lax.bitcast_convert_type(idx, I32)
        key = jnp.where(bits < 0, bits ^ INT_MAX, bits)
        key = jnp.where(idx == 0.0, 0, key)
        kpos = start + kcol
        kchunk = jnp.where(kpos < l_valid, kpos // CHUNK, INT_MAX)
        key = jnp.where(kchunk <= qchunk, key, INT_MIN)
        key_scr[:, pl.ds(start, tk)] = key
        gmax_scr[...] = jnp.maximum(gmax_scr[...], key)
        return carry

    gmax_scr[...] = jnp.full((tq, tk), INT_MIN, I32)
    lax.fori_loop(0, nblk, idx_body, 0)

    @pl.when(nblk % 2 == 1)
    def _():
        key_scr[:, pl.ds(pl.multiple_of(nblk * tk, tk), tk)] = jnp.full((tq, tk), INT_MIN, I32)

    npair = (nblk + 1) // 2

    rc_coarse = min(tq, 128)
    rc_fine = min(tq, 16)

    lane_col = lax.broadcasted_iota(I32, (1, LANES), 1)

    def count(indicator, *row_args, src=key_scr, nsteps=npair, width=2 * tk, rc=rc_coarse, active=None):
        outs = []
        for ci, r0 in enumerate(range(0, tq, rc)):
            args = [a[r0:r0 + rc] for a in row_args]

            def body(j, acc, r0=r0, args=args):
                for c in range(width // LANES):
                    start = pl.multiple_of(j * width + c * LANES, LANES)
                    acc = acc + indicator(src[r0:r0 + rc, pl.ds(start, LANES)], start + lane_col, *args)
                return acc

            def sweep(body=body):
                acc = lax.fori_loop(0, nsteps, body, jnp.zeros((rc, LANES), F32))
                return jnp.broadcast_to(jnp.sum(acc, axis=1, keepdims=True), (rc, LANES))

            if active is None:
                outs.append(sweep())
            else:
                outs.append(lax.cond(active[ci], sweep, lambda: jnp.zeros((rc, LANES), F32)))
        return outs[0] if len(outs) == 1 else jnp.concatenate(outs, axis=0)

    ge = lambda kt, kp, c: jnp.where(kt >= c, 1.0, 0.0)
    n_adm = jnp.broadcast_to(jnp.minimum((qchunk + 1) * CHUNK, l_valid).astype(F32), (tq, LANES))

    def lb_body(t, ans):
        cand_u = ans | lax.shift_left(jnp.int32(1), 31 - t)
        cnt = count(ge, cand_u ^ INT_MIN, src=gmax_scr, nsteps=1, width=tk)
        return jnp.where(cnt >= ksel, cand_u, ans)

    lb_bits = 16
    lb = lax.fori_loop(0, lb_bits, lb_body, jnp.zeros((tq, LANES), I32)) ^ INT_MIN
    gm = gmax_scr[...]
    gbits = jnp.where(gm < 0, gm ^ INT_MAX, gm)
    gval = jnp.where(gm == INT_MIN, -jnp.inf, lax.bitcast_convert_type(gbits, F32))
    vmax = jnp.broadcast_to(jnp.max(gval, axis=1, keepdims=True), (tq, LANES))
    mbits = lax.bitcast_convert_type(vmax, I32)
    kmax = jnp.where(mbits < 0, mbits ^ INT_MAX, mbits)
    lo0 = jnp.maximum(lb, INT_MIN + 1)
    hi0 = jnp.minimum(kmax, INT_MAX - 1) + 1

    def open_rows(lo, hi, c_lo):
        return jnp.where(c_lo == ksel, 0.0, jnp.where(lo + 1 >= hi, 0.0, jnp.where(n_adm <= ksel, 0.0, 1.0)))

    def chunk_flags(rows01, rc):
        return tuple(jnp.max(rows01[r0:r0 + rc]) > 0.0 for r0 in range(0, tq, rc))

    def n_open(rows01):
        return jnp.sum(rows01[:, 0:1])

    def halve_twice(lo, hi, c_lo, rc, active):
        for _ in range(2):
            mid = (lo >> 1) + (hi >> 1) + (lo & hi & 1)
            cnt = count(ge, mid, rc=rc, active=active)
            take = cnt >= ksel
            lo = jnp.where(take, mid, lo)
            c_lo = jnp.where(take, cnt, c_lo)
            hi = jnp.where(take, hi, mid)
        return lo, hi, c_lo

    few_rows = 16.0

    def coarse_cond(st):
        return jnp.logical_and(st[0] < 34, st[4] > few_rows)

    def coarse_body(st):
        t, lo, hi, c_lo = st[:4]
        lo, hi, c_lo = halve_twice(lo, hi, c_lo, rc_coarse, st[5:])
        rows01 = open_rows(lo, hi, c_lo)
        return (t + 2, lo, hi, c_lo, n_open(rows01)) + chunk_flags(rows01, rc_coarse)

    def fine_cond(st):
        return jnp.logical_and(st[0] < 34, st[4] > 0.0)

    def fine_body(st):
        t, lo, hi, c_lo = st[:4]
        lo, hi, c_lo = halve_twice(lo, hi, c_lo, rc_fine, st[5:])
        rows01 = open_rows(lo, hi, c_lo)
        return (t + 2, lo, hi, c_lo, n_open(rows01)) + chunk_flags(rows01, rc_fine)

    c_lo0 = count(ge, lo0)
    rows01 = open_rows(lo0, hi0, c_lo0)
    st = lax.while_loop(coarse_cond, coarse_body,
                        (jnp.int32(0), lo0, hi0, c_lo0, n_open(rows01)) + chunk_flags(rows01, rc_coarse))
    rows01 = open_rows(*st[1:4])
    st = lax.while_loop(fine_cond, fine_body, st[:4] + (n_open(rows01),) + chunk_flags(rows01, rc_fine))
    tau, c_ge = st[1], st[3]

    tie_rows = jnp.where(c_ge > ksel, 1.0, 0.0)
    tie_chunks = chunk_flags(tie_rows, rc_fine)
    any_tie = jnp.max(tie_rows) > 0.0
    need = ksel - count(lambda kt, kp, c: jnp.where(kt > c, 1.0, 0.0), tau, rc=rc_fine, active=tie_chunks)

    def pos_body(t, p):
        cand = p | lax.shift_left(jnp.int32(1), pos_bits - 1 - t)
        cnt = count(lambda kt, kp, tv, cv: jnp.where(kt == tv, jnp.where(kp < cv, 1.0, 0.0), 0.0), tau, cand,
                    rc=rc_fine, active=tie_chunks)
        return jnp.where(cnt < need, cand, p)

    p_lim = lax.fori_loop(0, jnp.where(any_tie, pos_bits, 0), pos_body, jnp.zeros((tq, LANES), I32))
    p_lim = jnp.where(tie_rows > 0.0, p_lim, INT_MAX)

    _init_online(m_scr, acc_scr)
    q_stacks = _stack_maps(q_ref, 2)
    ta = 2 * tk
    tau_t = _lane_tile(tau, ta)
    p_lim_t = _lane_tile(p_lim, ta)
    acol = lax.broadcasted_iota(I32, (1, ta), 1)

    def att_body(j, carry):
        start = pl.multiple_of(j * ta, ta)
        kt = key_scr[:, pl.ds(start, ta)]
        bias = jnp.where(kt > tau_t, 0.0, jnp.where(kt == tau_t, jnp.where(start + acol <= p_lim_t, 0.0, NEG), NEG))
        ss = _map_scores(q_stacks, k_ref, start, ta, tq)
        for h in range(HEADS):
            _online_update([h], [ss[h] + bias], v_ref[pl.ds(start, ta), h * LANES:(h + 1) * LANES], m_scr, acc_scr)
        return carry

    lax.fori_loop(0, npair, att_body, 0)
    o_ref[...] = jnp.concatenate([_normalized(acc_scr[h]) for h in range(HEADS)], axis=1).astype(BF16)


def _dsa(qi, misc, q, ki, k, v, *, q_off, l_valid, tq):
    b, t, _ = q.shape
    lp = v.shape[1]
    tk = 512
    assert lp % (2 * tk) == 0 and t % tq == 0
    ksel = min(TOPK_MAX, l_valid // 4)
    qtok = lambda width: pl.BlockSpec((None, tq, width), lambda bi, i: (bi, i, 0))
    res = lambda width: pl.BlockSpec((None, lp, width), lambda bi, i: (bi, 0, 0), pipeline_mode=pl.Buffered(1))
    res_t = lambda rows: pl.BlockSpec((None, rows, lp), lambda bi, i: (bi, 0, 0), pipeline_mode=pl.Buffered(1))
    kern = functools.partial(_dsa_kernel, tq=tq, tk=tk, q_off=q_off, l_valid=l_valid, ksel=float(ksel),
                             pos_bits=int(lp - 1).bit_length())
    return pl.pallas_call(
        kern,
        grid=(b, t // tq),
        in_specs=[qtok(IDX_HEADS * IDX_DIM), qtok(MISC_W), qtok(HEADS * LANES), res_t(MISC_W), res_t(GROUP_W),
                  res(HEADS * LANES)],
        out_specs=qtok(GROUP_W),
        out_shape=jax.ShapeDtypeStruct((b, t, GROUP_W), BF16),
        scratch_shapes=[pltpu.VMEM((tq, lp), I32), pltpu.VMEM((tq, tk), I32)] + _attn_scratch(HEADS, tq),
        compiler_params=_cparams(2),
        name="dsa",
    )(qi, misc, q, ki, k, v)


def _fox_kernel(q_ref, cq_ref, k_ref, v_ref, ck_ref, o_ref, m_scr, acc_scr, *, tq, tk, q_off, l_valid):
    i = pl.program_id(1)
    q0 = q_off + i * tq
    qpos = q0 + lax.broadcasted_iota(I32, (tq, 1), 0)
    nblk = (jnp.minimum(q0 + tq, l_valid) + tk - 1) // tk
    nfull = jnp.minimum((q0 + 1) // tk, l_valid // tk)
    kcol = lax.broadcasted_iota(I32, (1, tk), 1)
    cq = cq_ref[...]
    cqs = [jnp.broadcast_to(cq[:, h:h + 1], (tq, LANES)) for h in range(HEADS)]
    _init_online(m_scr, acc_scr)
    q_stacks = _stack_maps(q_ref, 2)

    def body(j, carry, masked):
        start = pl.multiple_of(j * tk, tk)
        if masked:
            kpos = start + kcol
            valid = kpos <= qpos
        ss = _map_scores(q_stacks, k_ref, start, tk, tq)
        for h in range(HEADS):
            s = ss[h] - ck_ref[h:h + 1, pl.ds(start, tk)]
            if masked:
                s = jnp.where(valid, s, NEG)
            _online_update([h], [s], v_ref[pl.ds(start, tk), h * LANES:(h + 1) * LANES], m_scr, acc_scr,
                           shift=cqs[h])
        return carry

    lax.fori_loop(0, nfull, functools.partial(body, masked=False), 0)
    lax.fori_loop(nfull, nblk, functools.partial(body, masked=True), 0)
    o_ref[...] = jnp.concatenate([_normalized(acc_scr[h]) for h in range(HEADS)], axis=1).astype(BF16)


def _fox(q, cq, k, v, ck_t, *, q_off, l_valid, tq):
    b, t, _ = q.shape
    lp = v.shape[1]
    tk = 1024
    assert lp % tk == 0 and t % tq == 0
    qtok = lambda width: pl.BlockSpec((None, tq, width), lambda bi, i: (bi, i, 0))
    res = lambda rows, width: pl.BlockSpec((None, rows, width), lambda bi, i: (bi, 0, 0),
                                           pipeline_mode=pl.Buffered(1))
    kern = functools.partial(_fox_kernel, tq=tq, tk=tk, q_off=q_off, l_valid=l_valid)
    return pl.pallas_call(
        kern,
        grid=(b, t // tq),
        in_specs=[qtok(HEADS * LANES), qtok(HEADS), res(GROUP_W, lp), res(lp, HEADS * LANES), res(HEADS, lp)],
        out_specs=qtok(GROUP_W),
        out_shape=jax.ShapeDtypeStruct((b, t, GROUP_W), BF16),
        scratch_shapes=_attn_scratch(HEADS, tq),
        compiler_params=_cparams(2),
        name="fox",
    )(q, cq, k, v, ck_t)


def _diff_kernel(q_ref, k_ref, v_ref, lam_ref, gsub_ref, o_ref, m_scr, acc_scr,
                 *, tq, tk, q_off, l_valid, out_scale):
    i = pl.program_id(1)
    q0 = q_off + i * tq
    qchunk = (q0 + lax.broadcasted_iota(I32, (tq, 1), 0)) // CHUNK
    n_end = jnp.minimum(((q0 + tq - 1) // CHUNK + 1) * CHUNK, l_valid)
    nblk = (n_end + tk - 1) // tk
    nfull = jnp.minimum(((q0 // CHUNK + 1) * CHUNK) // tk, l_valid // tk)
    kcol = lax.broadcasted_iota(I32, (1, tk), 1)
    _init_online(m_scr, acc_scr)
    q_stacks = _stack_maps(q_ref, 4)

    def body(j, carry, masked):
        start = pl.multiple_of(j * tk, tk)
        if masked:
            kpos = start + kcol
            valid = jnp.where(kpos < l_valid, kpos // CHUNK, INT_MAX) <= qchunk
        ss = _map_scores(q_stacks, k_ref, start, tk, tq)
        if masked:
            ss = [jnp.where(valid, s, NEG) for s in ss]
        for h in range(HEADS):
            _online_update([2 * h, 2 * h + 1], ss[2 * h:2 * h + 2], v_ref[pl.ds(start, tk), h * LANES:(h + 1) * LANES],
                           m_scr, acc_scr)
        return carry

    lax.fori_loop(0, nfull, functools.partial(body, masked=False), 0)
    lax.fori_loop(nfull, nblk, functools.partial(body, masked=True), 0)

    lv = lam_ref[...]
    lam = (jnp.exp(jnp.sum(lv[0:1] * lv[1:2], axis=1, keepdims=True))
           - jnp.exp(jnp.sum(lv[2:3] * lv[3:4], axis=1, keepdims=True)) + (1.0 - out_scale))
    outs = []
    for h in range(HEADS):
        o = _normalized(acc_scr[2 * h]) - lam * _normalized(acc_scr[2 * h + 1])
        outs.append(_rms(o, gsub_ref[...]) * out_scale)
    o_ref[...] = jnp.concatenate(outs, axis=1).astype(BF16)


def _diff(q, k, v, lam_vecs, g_sub, *, q_off, l_valid, tq, lam_init):
    b, t, _ = q.shape
    lp = v.shape[1]
    tk = 512
    assert lp % tk == 0 and t % tq == 0
    qtok = lambda width: pl.BlockSpec((None, tq, width), lambda bi, i: (bi, i, 0))
    res = lambda rows, width: pl.BlockSpec((None, rows, width), lambda bi, i: (bi, 0, 0),
                                           pipeline_mode=pl.Buffered(1))
    const = lambda shape: pl.BlockSpec(shape, lambda bi, i: (0, 0))
    kern = functools.partial(_diff_kernel, tq=tq, tk=tk, q_off=q_off, l_valid=l_valid, out_scale=1.0 - lam_init)
    return pl.pallas_call(
        kern,
        grid=(b, t // tq),
        in_specs=[qtok(2 * HEADS * LANES), res(GROUP_W, lp), res(lp, HEADS * LANES), const((4, DIFF_DIM)),
                  const((1, HEAD_DIM))],
        out_specs=qtok(GROUP_W),
        out_shape=jax.ShapeDtypeStruct((b, t, GROUP_W), BF16),
        scratch_shapes=_attn_scratch(2 * HEADS, tq),
        compiler_params=_cparams(2),
        name="diff",
    )(q, k, v, lam_vecs, g_sub.reshape(1, HEAD_DIM))


def _relbias_kernel(rb_ref, o_ref, *, tq):
    w = BAND_PREV + tq
    pw = 2 * w
    j = lax.broadcasted_iota(I32, (8, pw), 1)
    off = jnp.where(j < w, j, j - pw)
    idx = jnp.clip(BAND_PREV - off, -REL_CLIP, REL_CLIP) + REL_CLIP

    def body(r, accs):
        return tuple(jnp.where(idx == r, rb_ref[h, r], accs[h]) for h in range(HEADS))

    profs = lax.fori_loop(0, 2 * REL_CLIP + 1, body, tuple(jnp.zeros((8, pw), F32) for _ in range(HEADS)))
    qc = BAND_CHUNKS + lax.broadcasted_iota(I32, (tq, w), 0) // CHUNK
    kc = lax.broadcasted_iota(I32, (tq, w), 1) // CHUNK
    for h in range(HEADS):
        full = jnp.broadcast_to(profs[h][0:1], (tq, pw))
        t = pltpu.roll(full, 0, 1, stride=1, stride_axis=0)[:, 0:w]
        t = jnp.where(kc <= qc, t, NEG)
        o_ref[h] = jnp.where(kc >= qc - BAND_CHUNKS, t, NEG)


def _relbias(rel_bias_l, tq):
    return pl.pallas_call(
        functools.partial(_relbias_kernel, tq=tq),
        in_specs=[pl.BlockSpec(memory_space=pltpu.SMEM)],
        out_specs=pl.BlockSpec(memory_space=pltpu.VMEM),
        out_shape=jax.ShapeDtypeStruct((HEADS, tq, BAND_PREV + tq), F32),
        compiler_params=pltpu.CompilerParams(vmem_limit_bytes=VMEM_LIMIT_BYTES),
        name="relbias",
    )(rel_bias_l.astype(F32))


def _band_kernel(q_ref, kp_ref, kc_ref, vp_ref, vc_ref, bias_ref, o_ref, k_scr, v_scr, *, tq, first_has_no_prev):
    i = pl.program_id(1)
    w = BAND_PREV + tq
    k_scr[0:BAND_PREV, :] = kp_ref[...]
    k_scr[BAND_PREV:w, :] = kc_ref[...]
    v_scr[0:BAND_PREV, :] = vp_ref[...]
    v_scr[BAND_PREV:w, :] = vc_ref[...]
    kcol = lax.broadcasted_iota(I32, (1, w), 1)
    first = jnp.where(i == 0, BAND_PREV, 0) if first_has_no_prev else 0
    outs = []
    for h in range(HEADS):
        hs = slice(h * HEAD_DIM, (h + 1) * HEAD_DIM)
        s = _dot_nt(q_ref[:, hs], k_scr[:, hs]) + bias_ref[h]
        if first_has_no_prev:
            s = jnp.where(kcol >= first, s, NEG)
        e = jnp.exp(s - jnp.max(s, axis=1, keepdims=True))
        o = _dot(e.astype(BF16), v_scr[:, hs])
        outs.append(o / jnp.sum(e, axis=1, keepdims=True))
    o_ref[...] = jnp.concatenate(outs, axis=1).astype(BF16)


def _band(q, k_prev, k_cur, v_prev, v_cur, bias, *, tq, prev_from_cur):
    b, t, _ = q.shape
    assert t % tq == 0 and (not prev_from_cur or tq == BAND_PREV)
    qtok = pl.BlockSpec((None, tq, GROUP_W), lambda bi, i: (bi, i, 0))
    if prev_from_cur:
        prev = pl.BlockSpec((None, BAND_PREV, GROUP_W), lambda bi, i: (bi, jnp.maximum(i - 1, 0), 0))
    else:
        prev = pl.BlockSpec((None, BAND_PREV, GROUP_W), lambda bi, i: (bi, 0, 0))
    bspec = pl.BlockSpec((HEADS, tq, BAND_PREV + tq), lambda bi, i: (0, 0, 0))
    kern = functools.partial(_band_kernel, tq=tq, first_has_no_prev=prev_from_cur)
    return pl.pallas_call(
        kern,
        grid=(b, t // tq),
        in_specs=[qtok, prev, qtok, prev, qtok, bspec],
        out_specs=qtok,
        out_shape=jax.ShapeDtypeStruct((b, t, GROUP_W), BF16),
        scratch_shapes=[pltpu.VMEM((BAND_PREV + tq, GROUP_W), BF16), pltpu.VMEM((BAND_PREV + tq, GROUP_W), BF16)],
        compiler_params=_cparams(2),
        name="band",
    )(q, k_prev, k_cur, v_prev, v_cur, bias)


def _memkv_kernel(m_ref, g_ref, wk_ref, wv_ref, kf_o, vf_o, kb_o, vb_o):
    hb = _rms(m_ref[...], g_ref[...]).astype(BF16)
    k = _dot(hb, wk_ref[...])
    v = _dot(hb, wv_ref[...])
    kf_o[...] = k
    vf_o[...] = v
    kb_o[...] = k.astype(BF16)
    vb_o[...] = v.astype(BF16)


def _memkv(mem, g, wk, wv):
    b, m, d = mem.shape
    xw = wk.shape[1]
    tok = lambda width: pl.BlockSpec((None, m, width), lambda bi: (bi, 0, 0))
    const = lambda shape: pl.BlockSpec(shape, lambda bi: (0, 0))
    return pl.pallas_call(
        _memkv_kernel,
        grid=(b,),
        in_specs=[tok(d), const((1, d)), const((d, xw)), const((d, xw))],
        out_specs=[tok(xw)] * 4,
        out_shape=[jax.ShapeDtypeStruct((b, m, xw), F32)] * 2 + [jax.ShapeDtypeStruct((b, m, xw), BF16)] * 2,
        compiler_params=_cparams(1),
        name="memkv",
    )(mem, g.reshape(1, d), wk, wv)


def _mix_cross_kernel(x_ref, oa_ref, ob_ref, oc_ref, od_ref, wout_ref, g_ref, wq_ref, mk_ref, mv_ref, wo_ref, o_ref):
    x = x_ref[...]
    for n, oref in enumerate((oa_ref, ob_ref, oc_ref, od_ref)):
        x = x + _dot(oref[...], wout_ref[n * GROUP_W:(n + 1) * GROUP_W, :])
    hb = _rms(x, g_ref[...]).astype(BF16)
    q = _dot(hb, wq_ref[...])
    outs = []
    for h in range(X_HEADS):
        hs = slice(h * X_HEAD_DIM, (h + 1) * X_HEAD_DIM)
        s = _dot_nt(q[:, hs].astype(BF16), mk_ref[:, hs]) * (X_HEAD_DIM ** -0.5)
        e = jnp.exp(s - jnp.max(s, axis=1, keepdims=True))
        p = e / jnp.sum(e, axis=1, keepdims=True)
        outs.append(_dot(p.astype(BF16), mv_ref[:, hs]))
    o = jnp.concatenate(outs, axis=1).astype(BF16)
    o_ref[...] = x + _dot(o, wo_ref[...])


def _mix_cross(x, oa, ob, oc, od, w_out, g, wq, mk, mv, wo):
    b, t, d = x.shape
    tm = min(512, t)
    m, xw = mk.shape[1], mk.shape[2]
    tok = lambda width: pl.BlockSpec((None, tm, width), lambda bi, i: (bi, i, 0))
    const = lambda shape: pl.BlockSpec(shape, lambda bi, i: (0, 0))
    mem = pl.BlockSpec((None, m, xw), lambda bi, i: (bi, 0, 0))
    return pl.pallas_call(
        _mix_cross_kernel,
        grid=(b, t // tm),
        in_specs=[tok(d)] + [tok(GROUP_W)] * 4 + [const(w_out.shape), const((1, d)), const(wq.shape), mem, mem,
                                                   const(wo.shape)],
        out_specs=tok(d),
        out_shape=jax.ShapeDtypeStruct((b, t, d), F32),
        compiler_params=_cparams(2),
        name="mix_cross",
    )(x, oa, ob, oc, od, w_out, g.reshape(1, d), wq, mk, mv, wo)


def _ffn_kernel(x_ref, g_ref, prev_ref, wup_ref, cw_ref, cb_ref, wdn_ref, o_ref, conv_ref, halo_scr, acc_scr,
                *, tm, d_ff, fc):
    i = pl.program_id(1)

    @pl.when(i == 0)
    def _():
        halo_scr[...] = prev_ref[...]

    x = x_ref[...]
    hb = _rms(x, g_ref[...]).astype(BF16)
    row = lax.broadcasted_iota(I32, (tm, fc), 0)
    for c in range(d_ff // fc):
        cs = slice(c * fc, (c + 1) * fc)
        g = _dot(hb, wup_ref[:, cs])
        u = _dot(hb, wup_ref[:, d_ff + c * fc:d_ff + (c + 1) * fc])
        h0 = halo_scr[0:1, cs]
        h1 = halo_scr[1:2, cs]
        gm1 = jnp.where(row == 0, h1, pltpu.roll(g, 1, 0))
        gm2 = jnp.where(row == 0, h0, jnp.where(row == 1, h1, pltpu.roll(g, 2, 0)))
        a = cb_ref[:, cs] + gm2 * cw_ref[0:1, cs]
        a = a + gm1 * cw_ref[1:2, cs]
        a = a + g * cw_ref[2:3, cs]
        y = (a * (1.0 / (1.0 + jnp.exp(-a)))) * u
        f = _dot(y.astype(BF16), wdn_ref[cs, :])
        if c == 0:
            acc_scr[...] = f
        else:
            acc_scr[...] += f
        tail = g[tm - (CONV_W - 1):tm, :]
        halo_scr[:, cs] = tail
        conv_ref[:, cs] = tail
    o_ref[...] = x + acc_scr[...]


def _ffn(x, g, prev, w_up, conv_w, conv_b, w_down):
    b, t, d = x.shape
    d_ff = w_down.shape[0]
    tm = min(512, t)
    fc = 256
    assert d_ff % fc == 0 and t % tm == 0 and tm >= CONV_W - 1
    tok = pl.BlockSpec((None, tm, d), lambda bi, i: (bi, i, 0))
    const = lambda shape: pl.BlockSpec(shape, lambda bi, i: (0, 0))
    st = pl.BlockSpec((None, CONV_W - 1, d_ff), lambda bi, i: (bi, 0, 0))
    kern = functools.partial(_ffn_kernel, tm=tm, d_ff=d_ff, fc=fc)
    return pl.pallas_call(
        kern,
        grid=(b, t // tm),
        in_specs=[tok, const((1, d)), st, const(w_up.shape), const(conv_w.shape), const((1, d_ff)),
                  const(w_down.shape)],
        out_specs=[tok, st],
        out_shape=[jax.ShapeDtypeStruct((b, t, d), F32), jax.ShapeDtypeStruct((b, CONV_W - 1, d_ff), F32)],
        scratch_shapes=[pltpu.VMEM((CONV_W - 1, d_ff), F32), pltpu.VMEM((tm, d), F32)],
        compiler_params=_cparams(2),
        name="ffn",
    )(x, g.reshape(1, d), prev, w_up, conv_w, conv_b.reshape(1, d_ff), w_down)


def _rmsnorm_kernel(x_ref, g_ref, o_ref):
    o_ref[...] = _rms(x_ref[...], g_ref[...])


def _rmsnorm(x, g):
    b, t, d = x.shape
    tm = min(1024, t)
    tok = pl.BlockSpec((None, tm, d), lambda bi, i: (bi, i, 0))
    return pl.pallas_call(
        _rmsnorm_kernel,
        grid=(b, t // tm),
        in_specs=[tok, pl.BlockSpec((1, d), lambda bi, i: (0, 0))],
        out_specs=tok,
        out_shape=jax.ShapeDtypeStruct((b, t, d), F32),
        compiler_params=_cparams(2),
        name="final_norm",
    )(x, g.reshape(1, d))


def _pad_rows(a, rows):
    return a if a.shape[1] == rows else jnp.pad(a, ((0, 0), (0, rows - a.shape[1]), (0, 0)))


def _round_up(n, m):
    return (n + m - 1) // m * m


def _value_slabs_host(v):
    b, l = v.shape[:2]
    one = jnp.zeros((b, l, HEADS, HEAD_DIM), BF16).at[..., 0].set(1.0)
    return jnp.concatenate([v.astype(BF16), one], axis=-1).reshape(b, l, HEADS * LANES)


def _mixers(pr, keys, *, q_off, l_valid, tq_a, tq_bd, band_args, band_tq, prev_from_cur, lam_vecs, g_sub,
            lam_init, logf_all):
    lp = keys["av"].shape[1]
    oa = _dsa(pr["aqi"], pr["misc"], pr["aq"], keys["aki"], keys["ak"], keys["av"],
              q_off=q_off, l_valid=l_valid, tq=tq_a)
    lc = _round_up(logf_all.shape[1], 1024)
    c_t = _cumsum_t(jnp.swapaxes(_pad_rows(logf_all, lc), 1, 2))
    t = pr["bq"].shape[1]
    cq = jnp.swapaxes(c_t[:, :, q_off:q_off + t], 1, 2)
    ck_t = c_t[:, :, :lp] if lc >= lp else jnp.pad(c_t, ((0, 0), (0, 0), (0, lp - lc)))
    ob = _fox(pr["bq"], cq, keys["bk"], keys["bv"], ck_t, q_off=q_off, l_valid=l_valid, tq=tq_bd)
    oc = _band(pr["cq"], *band_args, tq=band_tq, prev_from_cur=prev_from_cur)
    od = _diff(pr["dq"], keys["dk"], keys["dv"], lam_vecs, g_sub, q_off=q_off, l_valid=l_valid, tq=tq_bd,
               lam_init=lam_init)
    return oa, ob, oc, od


def kernel(x_prompt, x_sample, mem_prompt, cache_a_k, cache_a_v, cache_a_kidx, cache_b_k, cache_b_v,
           cache_b_logf, cache_c_k, cache_c_v, cache_d_k, cache_d_v, cache_mem_k, cache_mem_v,
           state_ffn_conv, g_mix, w_in, b_forget, rel_bias, lam_q1, lam_k1, lam_q2, lam_k2, g_sub, w_out,
           g_cross, g_mem, w_xq, w_xk, w_xv, w_xo, g_ffn, w_up, conv_w, conv_b, w_down, g_final):
    depth = w_in.shape[0]
    bp, s_len, d_model = x_prompt.shape
    bs, t_len, _ = x_sample.shape
    past = cache_a_k.shape[2]
    d_ff = w_down.shape[1]
    c_keep = min(BAND_PREV, s_len)
    l_s = past + t_len
    lp_s = _round_up(l_s, 1024)

    pos_p = jnp.arange(s_len, dtype=jnp.int32)
    pos_s = jnp.tile(past + jnp.arange(t_len, dtype=jnp.int32), bs)
    h64 = HEAD_DIM // ROT_FRACTION // 2
    h32 = DIFF_DIM // ROT_FRACTION // 2
    t64_p, t32_p = _rope_table(pos_p, h64, HEAD_DIM), _rope_table(pos_p, h32, DIFF_DIM)
    t64_s, t32_s = _rope_table(pos_s, h64, HEAD_DIM), _rope_table(pos_s, h32, DIFF_DIM)

    flat = lambda a: a.reshape(a.shape[0], a.shape[1], -1)
    heads = lambda a, hd=HEAD_DIM: a.reshape(a.shape[0], a.shape[1], a.shape[2] // hd, hd)

    xp, xs = x_prompt, x_sample
    p_states, s_states = [], []
    for l in range(depth):
        lam_init = 0.8 - 0.6 * math.exp(-0.3 * l)
        w_pack, bfp = _pack_w_in(w_in[l], b_forget[l])
        lam_vecs = jnp.stack([lam_q1[l], lam_k1[l], lam_q2[l], lam_k2[l]]).astype(F32)
        bias = _relbias(rel_bias[l], BAND_PREV)
        bias_s = bias[:, :t_len, :BAND_PREV + t_len]
        w_out_b, wq_b, wo_b = w_out[l].astype(BF16), w_xq[l].astype(BF16), w_xo[l].astype(BF16)
        w_up_b, w_dn_b = w_up[l].astype(BF16), w_down[l].astype(BF16)

        pr = _proj(xp, g_mix[l], w_pack, bfp, t64_p, t32_p, c_keep)
        logf = pr["misc"][:, :, MISC_BF:MISC_BF + HEADS]
        oa, ob, oc, od = _mixers(
            pr, pr, q_off=0, l_valid=s_len, tq_a=256, tq_bd=512,
            band_args=(pr["ck"], pr["ck"], pr["cv"], pr["cv"], bias), band_tq=BAND_PREV, prev_from_cur=True,
            lam_vecs=lam_vecs, g_sub=g_sub[l], lam_init=lam_init, logf_all=logf)
        mkf, mvf, mkb, mvb = _memkv(mem_prompt, g_mem[l], w_xk[l].astype(BF16), w_xv[l].astype(BF16))
        xp = _mix_cross(xp, oa, ob, oc, od, w_out_b, g_cross[l], wq_b, mkb, mvb, wo_b)
        xp, conv_p = _ffn(xp, g_ffn[l], jnp.zeros((bp, CONV_W - 1, d_ff), F32), w_up_b, conv_w[l], conv_b[l], w_dn_b)
        p_states.append((heads(pr["akf"]), heads(pr["avf"]), pr["misc"][:, :, :IDX_DIM],
                         heads(pr["bkf"]), heads(pr["bvf"]), logf,
                         heads(pr["ckf"]), heads(pr["cvf"]), heads(pr["dkf"]), heads(pr["dvf"]),
                         conv_p, heads(mkf, X_HEAD_DIM), heads(mvf, X_HEAD_DIM)))

        prs = _proj(xs.reshape(1, bs * t_len, d_model), g_mix[l], w_pack, bfp, t64_s, t32_s, bs * t_len)
        prs = {n: a.reshape(bs, t_len, a.shape[-1]) for n, a in prs.items() if n not in ("ak", "aki", "bk", "dk")}
        cat_t = lambda cache, new: jnp.swapaxes(
            _pad_rows(jnp.concatenate([flat(cache), new], axis=1), lp_s).astype(BF16), 1, 2)
        kidx = jnp.concatenate([cache_a_kidx[l], prs["misc"][:, :, :IDX_DIM]], axis=1)
        kidx = jnp.pad(kidx, ((0, 0), (0, lp_s - l_s), (0, MISC_W - IDX_DIM)))
        catv = lambda cache, new: _pad_rows(jnp.concatenate([_value_slabs_host(cache), new], axis=1), lp_s)
        keys = {"aki": jnp.swapaxes(kidx.astype(BF16), 1, 2),
                "ak": cat_t(cache_a_k[l], prs["akf"]), "av": catv(cache_a_v[l], prs["av"]),
                "bk": cat_t(cache_b_k[l], prs["bkf"]), "bv": catv(cache_b_v[l], prs["bv"]),
                "dk": cat_t(cache_d_k[l], prs["dkf"]), "dv": catv(cache_d_v[l], prs["dv"])}
        logf_s = prs["misc"][:, :, MISC_BF:MISC_BF + HEADS]
        logf_all = jnp.concatenate([cache_b_logf[l].astype(F32), logf_s], axis=1)
        oa, ob, oc, od = _mixers(
            prs, keys, q_off=past, l_valid=l_s, tq_a=t_len, tq_bd=t_len,
            band_args=(flat(cache_c_k[l]).astype(BF16), prs["ck"], flat(cache_c_v[l]).astype(BF16), prs["cv"],
                       bias_s),
            band_tq=t_len, prev_from_cur=False,
            lam_vecs=lam_vecs, g_sub=g_sub[l], lam_init=lam_init, logf_all=logf_all)
        xs = _mix_cross(xs, oa, ob, oc, od, w_out_b, g_cross[l], wq_b,
                        flat(cache_mem_k[l]).astype(BF16), flat(cache_mem_v[l]).astype(BF16), wo_b)
        xs, conv_s = _ffn(xs, g_ffn[l], state_ffn_conv[l], w_up_b, conv_w[l], conv_b[l], w_dn_b)
        s_states.append((heads(prs["akf"]), heads(prs["avf"]), prs["misc"][:, :, :IDX_DIM],
                         heads(prs["bkf"]), heads(prs["bvf"]), logf_s,
                         heads(prs["ckf"]), heads(prs["cvf"]), heads(prs["dkf"]), heads(prs["dvf"]), conv_s))

    y_prompt = _rmsnorm(xp, g_final)
    y_sample = _rmsnorm(xs, g_final)
    p_out = [jnp.stack(z, axis=0) for z in zip(*p_states)]
    s_out = [jnp.stack(z, axis=0) for z in zip(*s_states)]
    return (y_prompt, y_sample, *p_out, *s_out)
```

```python
import functools
import math

import numpy as np
import jax
import jax.numpy as jnp
from jax import lax
from jax.experimental import pallas as pl
from jax.experimental.pallas import tpu as pltpu

F32 = jnp.float32
BF16 = jnp.bfloat16
I32 = jnp.int32

CHUNK = 64
HEAD_DIM = 64
HEADS = 4
GROUP_W = HEADS * HEAD_DIM
ROT_FRACTION = 4
ROPE_THETA = 500000.0
IDX_HEADS = 8
IDX_DIM = 64
TOPK_MAX = 256
BAND_CHUNKS = 8
BAND_KEYS = (BAND_CHUNKS + 1) * CHUNK
BAND_PREV = BAND_CHUNKS * CHUNK
REL_CLIP = 128
DIFF_DIM = HEAD_DIM // 2
X_HEADS = 4
X_HEAD_DIM = 128
CONV_W = 3
EPS = 1e-6

LANES = 128
VMEM_LIMIT_BYTES = 56 * 1024 * 1024

NEG = -1e30
INT_MIN = -2147483648
INT_MAX = 2147483647

MISC_W = LANES
MISC_AW = IDX_DIM
MISC_BF = IDX_DIM + IDX_HEADS
PROJ_COLS = 3 * GROUP_W + IDX_HEADS * IDX_DIM + MISC_W + 9 * GROUP_W


def _cparams(n_axes):
    return pltpu.CompilerParams(dimension_semantics=("arbitrary",) * n_axes,
                                vmem_limit_bytes=VMEM_LIMIT_BYTES)


def _rms(x, g):
    return (x * lax.rsqrt(jnp.mean(x * x, axis=-1, keepdims=True) + EPS)) * g


def _dot(a, b):
    return jnp.dot(a, b, preferred_element_type=F32)


def _dot_nt(a, b):
    return lax.dot_general(a, b, (((1,), (1,)), ((), ())), preferred_element_type=F32)


def _rope(z, tab, half):
    w = z.shape[1]
    reps = w // LANES
    rep = lambda t: t if reps == 1 else jnp.concatenate([t] * reps, axis=1)
    c = rep(tab[:, 0:LANES])
    s1 = rep(tab[:, LANES:2 * LANES])
    s2 = rep(tab[:, 2 * LANES:3 * LANES])
    return z * c + pltpu.roll(z, w - half, 1) * s1 + pltpu.roll(z, half, 1) * s2


def _slab_masked(z, widths):
    lane = lax.broadcasted_iota(I32, (z.shape[0], LANES), 1)
    outs = []
    for j in range(GROUP_W // widths):
        lo = (j * widths) % LANES
        zs = z[:, (j * widths) // LANES * LANES:((j * widths) // LANES + 1) * LANES]
        outs.append(jnp.where(lane < lo, 0.0, jnp.where(lane < lo + widths, zs, 0.0)))
    return jnp.concatenate(outs, axis=1)


def _value_slabs(z):
    one = jnp.where(lax.broadcasted_iota(I32, (z.shape[0], HEAD_DIM), 1) == 0, 1.0, 0.0)
    return jnp.concatenate([t for h in range(HEADS) for t in (z[:, h * HEAD_DIM:(h + 1) * HEAD_DIM], one)], axis=1)


def _proj_kernel(x_ref, g_ref, w_ref, bf_ref, t64_ref, t32_ref,
                 aq_o, ak_o, av_o, aqi_o, aki_o, bq_o, bk_o, bv_o, cq_o, ck_o, cv_o, dq_o, dk_o, dv_o,
                 akf_o, avf_o, misc_o, bkf_o, bvf_o, ckf_o, cvf_o, dkf_o, dvf_o):
    hb = _rms(x_ref[...], g_ref[...]).astype(BF16)
    t64 = t64_ref[...]
    t32 = t32_ref[...]
    h64 = HEAD_DIM // ROT_FRACTION // 2
    h32 = DIFF_DIM // ROT_FRACTION // 2
    qscale = HEAD_DIM ** -0.5
    col = [0]

    def mm(width):
        c0 = col[0]
        col[0] = c0 + width
        return _dot(hb, w_ref[:, c0:c0 + width])

    z = _rope(mm(GROUP_W), t64, h64)
    aq_o[...] = _slab_masked(z * qscale, HEAD_DIM).astype(BF16)
    z = _rope(mm(GROUP_W), t64, h64)
    akf_o[...] = z
    ak_o[...] = z.T.astype(BF16)
    z = mm(GROUP_W)
    avf_o[...] = z
    av_o[...] = _value_slabs(z).astype(BF16)
    z = _rope(mm(IDX_HEADS * IDX_DIM), t64, h64)
    aqi_o[...] = (z * (IDX_DIM ** -0.5)).astype(BF16)
    z = mm(MISC_W)
    r = _rope(z, t64, h64)
    lane = lax.broadcasted_iota(I32, z.shape, 1)
    zf = z + bf_ref[...]
    logsig = jnp.minimum(zf, 0.0) - jnp.log1p(jnp.exp(-jnp.abs(zf)))
    misc_o[...] = jnp.where(lane < MISC_AW, r, jnp.where(lane < MISC_BF, z * (IDX_HEADS ** -0.5), logsig))
    aki_o[...] = jnp.where(lane < MISC_AW, r, 0.0).T.astype(BF16)
    z = mm(GROUP_W)
    bq_o[...] = _slab_masked(z * qscale, HEAD_DIM).astype(BF16)
    z = mm(GROUP_W)
    bkf_o[...] = z
    bk_o[...] = z.T.astype(BF16)
    z = mm(GROUP_W)
    bvf_o[...] = z
    bv_o[...] = _value_slabs(z).astype(BF16)
    z = mm(GROUP_W)
    cq_o[...] = (z * qscale).astype(BF16)
    z = mm(GROUP_W)
    ckf_o[...] = z
    ck_o[...] = z.astype(BF16)
    z = mm(GROUP_W)
    cvf_o[...] = z
    cv_o[...] = z.astype(BF16)
    z = _rope(mm(GROUP_W), t32, h32)
    dq_o[...] = _slab_masked(z * (DIFF_DIM ** -0.5), DIFF_DIM).astype(BF16)
    z = _rope(mm(GROUP_W), t32, h32)
    dkf_o[...] = z
    dk_o[...] = z.T.astype(BF16)
    z = mm(GROUP_W)
    dvf_o[...] = z
    dv_o[...] = _value_slabs(z).astype(BF16)


def _rope_table(pos, half, period):
    inv = ROPE_THETA ** (-jnp.arange(half, dtype=F32) / half)
    ang = pos.astype(F32)[:, None] * inv[None, :]
    cos, sin = jnp.cos(ang), jnp.sin(ang)
    t = pos.shape[0]
    rest = period - 2 * half
    zh = jnp.zeros((t, half), F32)
    zr = jnp.zeros((t, rest), F32)
    c = jnp.concatenate([cos, cos, jnp.ones((t, rest), F32)], axis=1)
    s1 = jnp.concatenate([-sin, zh, zr], axis=1)
    s2 = jnp.concatenate([zh, sin, zr], axis=1)
    rep = lambda a: jnp.tile(a, (1, LANES // period))
    return jnp.concatenate([rep(c), rep(s1), rep(s2)], axis=1)


def _pack_w_in(w_in_l, b_forget_l):
    sizes = (GROUP_W, GROUP_W, GROUP_W, IDX_HEADS * IDX_DIM, IDX_DIM, IDX_HEADS,
             GROUP_W, GROUP_W, GROUP_W, HEADS, GROUP_W, GROUP_W, GROUP_W, GROUP_W, GROUP_W, GROUP_W)
    pts = [int(p) for p in np.cumsum(sizes)[:-1]]
    (aq, ak, av, aqi, aki, aw, bq, bk, bv, bf, cq, ck, cv, dq, dk, dv) = jnp.split(w_in_l, pts, axis=1)
    d = w_in_l.shape[0]
    pad = jnp.zeros((d, MISC_W - IDX_DIM - IDX_HEADS - HEADS), w_in_l.dtype)
    misc = jnp.concatenate([aki, aw, bf, pad], axis=1)
    w = jnp.concatenate([aq, ak, av, aqi, misc, bq, bk, bv, cq, ck, cv, dq, dk, dv], axis=1).astype(BF16)
    bfp = jnp.zeros((1, MISC_W), F32).at[0, MISC_BF:MISC_BF + HEADS].set(b_forget_l.astype(F32))
    return w, bfp


def _proj(x, g, w, bfp, t64, t32, c_keep):
    b, t, d = x.shape
    tm = min(512, t)
    nt = t // tm
    nkeep = c_keep // tm
    tok = lambda width: pl.BlockSpec((None, tm, width), lambda bi, i: (bi, i, 0))
    ctok = pl.BlockSpec((None, tm, GROUP_W), lambda bi, i: (bi, jnp.maximum(i - (nt - nkeep), 0), 0))
    const = lambda shape: pl.BlockSpec(shape, lambda bi, i: (0, 0))
    tab = pl.BlockSpec((tm, 3 * LANES), lambda bi, i: (i, 0))
    sds = lambda width, dt, rows=t: jax.ShapeDtypeStruct((b, rows, width), dt)
    bf_names = ["aq", "ak", "av", "aqi", "aki", "bq", "bk", "bv", "cq", "ck", "cv", "dq", "dk", "dv"]
    qx_w, vx_w = HEADS * LANES, HEADS * LANES
    bf_w = [qx_w, GROUP_W, vx_w, IDX_HEADS * IDX_DIM, MISC_W, qx_w, GROUP_W, vx_w, GROUP_W, GROUP_W, GROUP_W,
            2 * qx_w, GROUP_W, vx_w]
    f_names = ["akf", "avf", "misc", "bkf", "bvf", "ckf", "cvf", "dkf", "dvf"]
    f_w = [GROUP_W, GROUP_W, MISC_W] + [GROUP_W] * 6
    out_shape = [sds(wd, BF16) for wd in bf_w]
    out_specs = [tok(wd) for wd in bf_w]
    for n, wd in zip(bf_names, bf_w):
        if n in ("ak", "aki", "bk", "dk"):
            out_shape[bf_names.index(n)] = jax.ShapeDtypeStruct((b, wd, t), BF16)
            out_specs[bf_names.index(n)] = pl.BlockSpec((None, wd, tm), lambda bi, i: (bi, 0, i))
    for n, wd in zip(f_names, f_w):
        if n in ("ckf", "cvf"):
            out_shape.append(sds(wd, F32, c_keep))
            out_specs.append(ctok)
        else:
            out_shape.append(sds(wd, F32))
            out_specs.append(tok(wd))
    outs = pl.pallas_call(
        _proj_kernel,
        grid=(b, nt),
        in_specs=[tok(d), const((1, d)), const((d, PROJ_COLS)), const((1, MISC_W)), tab, tab],
        out_specs=out_specs,
        out_shape=out_shape,
        compiler_params=_cparams(2),
        name="proj",
    )(x, g.reshape(1, d), w, bfp, t64, t32)
    return dict(zip(bf_names + f_names, outs))


def _cumsum_kernel(x_ref, o_ref, *, nb):
    x = x_ref[...]
    lane = lax.broadcasted_iota(I32, x.shape, 1)
    d = 1
    while d < LANES:
        x = x + jnp.where(lane >= d, pltpu.roll(x, d, 1), 0.0)
        d *= 2
    row = lax.broadcasted_iota(I32, x.shape, 0) % nb
    tot = jnp.broadcast_to(x[:, LANES - 1:LANES], x.shape)
    exc = jnp.where(row >= 1, pltpu.roll(tot, 1, 0), 0.0)
    d = 1
    while d < nb:
        exc = exc + jnp.where(row >= d, pltpu.roll(exc, d, 0), 0.0)
        d *= 2
    o_ref[...] = x + exc


def _cumsum_t(logf_t):
    b, h, l = logf_t.shape
    nb = l // LANES
    x = logf_t.reshape(b, h * nb, LANES)
    spec = pl.BlockSpec((None, h * nb, LANES), lambda bi: (bi, 0, 0))
    out = pl.pallas_call(
        functools.partial(_cumsum_kernel, nb=nb),
        grid=(b,),
        in_specs=[spec],
        out_specs=spec,
        out_shape=jax.ShapeDtypeStruct(x.shape, F32),
        compiler_params=_cparams(1),
        name="cumsum",
    )(x)
    return out.reshape(b, h, l)


def _lane_tile(a, width):
    return a if width == LANES else jnp.concatenate([a] * (width // LANES), axis=1)


def _online_update(js, ss, vx, m_scr, acc_scr, shift=None):
    tq = ss[0].shape[0]
    ps, alphas = [], []
    for j, s in zip(js, ss):
        m_prev = m_scr[j]
        rmax = jnp.max(s, axis=1, keepdims=True)
        if shift is not None:
            rmax = rmax + shift
        m_new = jnp.maximum(m_prev, rmax)
        sub = m_new if shift is None else m_new - shift
        ps.append(jnp.exp(s - _lane_tile(sub, s.shape[1])).astype(BF16))
        alphas.append(jnp.exp(m_prev - m_new))
        m_scr[j] = m_new
    pv = _dot(ps[0] if len(ps) == 1 else jnp.concatenate(ps, axis=0), vx)
    for i, j in enumerate(js):
        acc_scr[j] = alphas[i] * acc_scr[j] + pv[i * tq:(i + 1) * tq]


def _stack_maps(q_ref, per_pair):
    n_pairs = q_ref.shape[1] // LANES // per_pair
    return [jnp.concatenate([q_ref[:, m * LANES:(m + 1) * LANES] for m in range(p * per_pair, (p + 1) * per_pair)],
                            axis=0) for p in range(n_pairs)]


def _map_scores(q_stacks, kt_ref, start, tk, tq):
    out = []
    for p, qs in enumerate(q_stacks):
        sp = _dot(qs, kt_ref[p * LANES:(p + 1) * LANES, pl.ds(start, tk)])
        out += [sp[i * tq:(i + 1) * tq] for i in range(qs.shape[0] // tq)]
    return out


def _init_online(m_scr, acc_scr):
    m_scr[...] = jnp.full(m_scr.shape, NEG, F32)
    acc_scr[...] = jnp.zeros(acc_scr.shape, F32)


def _normalized(acc):
    return acc[:, 0:HEAD_DIM] / acc[:, HEAD_DIM:HEAD_DIM + 1]


def _attn_scratch(n_state, tq):
    return [pltpu.VMEM((n_state, tq, LANES), F32), pltpu.VMEM((n_state, tq, LANES), F32)]


def _dsa_kernel(qi_ref, w_ref, q_ref, ki_ref, k_ref, v_ref, o_ref, key_scr, gmax_scr, m_scr, acc_scr,
                *, tq, tk, q_off, l_valid, ksel, pos_bits):
    i = pl.program_id(1)
    q0 = q_off + i * tq
    qchunk = (q0 + lax.broadcasted_iota(I32, (tq, 1), 0)) // CHUNK
    n_end = jnp.minimum(((q0 + tq - 1) // CHUNK + 1) * CHUNK, l_valid)
    nblk = (n_end + tk - 1) // tk
    kcol = lax.broadcasted_iota(I32, (1, tk), 1)

    qi = qi_ref[...]
    w = w_ref[...]
    qi_stack = jnp.concatenate([qi[:, g * IDX_DIM:(g + 1) * IDX_DIM] for g in range(IDX_HEADS)], axis=0)
    ws = [jnp.broadcast_to(w[:, MISC_AW + g:MISC_AW + g + 1], (tq, LANES)) for g in range(IDX_HEADS)]

    def idx_body(j, carry, masked):
        start = pl.multiple_of(j * tk, tk)
        s_all = _dot(qi_stack, ki_ref[0:IDX_DIM, pl.ds(start, tk)])
        idx = jnp.zeros((tq, tk), F32)
        for g in range(IDX_HEADS):
            idx = idx + jnp.maximum(s_all[g * tq:(g + 1) * tq], 0.0) * _lane_tile(ws[g], tk)
        bits = lax.bitcast_convert_type(idx, I32)
        key = jnp.where(bits < 0, bits ^ INT_MAX, bits)
        key = jnp.where(idx == 0.0, 0, key)
        if masked:
            kpos = start + kcol
            kchunk = jnp.where(kpos < l_valid, kpos // CHUNK, INT_MAX)
            key = jnp.where(kchunk <= qchunk, key, INT_MIN)
        key_scr[:, pl.ds(start, tk)] = key
        gmax_scr[...] = jnp.maximum(gmax_scr[...], key)
        return carry

    gmax_scr[...] = jnp.full((tq, tk), INT_MIN, I32)
    nfree = jnp.minimum((q0 // CHUNK + 1) * CHUNK, l_valid) // tk
    lax.fori_loop(0, nfree, functools.partial(idx_body, masked=False), 0)
    lax.fori_loop(nfree, nblk, functools.partial(idx_body, masked=True), 0)

    @pl.when(nblk % 2 == 1)
    def _():
        key_scr[:, pl.ds(pl.multiple_of(nblk * tk, tk), tk)] = jnp.full((tq, tk), INT_MIN, I32)

    npair = (nblk + 1) // 2

    rc_coarse = min(tq, 128)
    rc_fine = min(tq, 16)

    lane_col = lax.broadcasted_iota(I32, (1, LANES), 1)

    def count(indicator, *row_args, src=key_scr, nsteps=npair, width=2 * tk, rc=rc_coarse, active=None):
        outs = []
        for ci, r0 in enumerate(range(0, tq, rc)):
            args = [a[r0:r0 + rc] for a in row_args]

            def body(j, acc, r0=r0, args=args):
                for c in range(width // LANES):
                    start = pl.multiple_of(j * width + c * LANES, LANES)
                    acc = acc + indicator(src[r0:r0 + rc, pl.ds(start, LANES)], start + lane_col, *args)
                return acc

            def sweep(body=body):
                acc = lax.fori_loop(0, nsteps, body, jnp.zeros((rc, LANES), F32))
                return jnp.broadcast_to(jnp.sum(acc, axis=1, keepdims=True), (rc, LANES))

            if active is None:
                outs.append(sweep())
            else:
                outs.append(lax.cond(active[ci], sweep, lambda: jnp.zeros((rc, LANES), F32)))
        return outs[0] if len(outs) == 1 else jnp.concatenate(outs, axis=0)

    ge = lambda kt, kp, c: jnp.where(kt >= c, 1.0, 0.0)
    n_adm = jnp.broadcast_to(jnp.minimum((qchunk + 1) * CHUNK, l_valid).astype(F32), (tq, LANES))

    def lb_body(t, ans):
        cand_u = ans | lax.shift_left(jnp.int32(1), 31 - t)
        cnt = count(ge, cand_u ^ INT_MIN, src=gmax_scr, nsteps=1, width=tk)
        return jnp.where(cnt >= ksel, cand_u, ans)

    lb_bits = 16
    lb = lax.fori_loop(0, lb_bits, lb_body, jnp.zeros((tq, LANES), I32)) ^ INT_MIN
    gm = gmax_scr[...]
    gbits = jnp.where(gm < 0, gm ^ INT_MAX, gm)
    gval = jnp.where(gm == INT_MIN, -jnp.inf, lax.bitcast_convert_type(gbits, F32))
    vmax = jnp.broadcast_to(jnp.max(gval, axis=1, keepdims=True), (tq, LANES))
    mbits = lax.bitcast_convert_type(vmax, I32)
    kmax = jnp.where(mbits < 0, mbits ^ INT_MAX, mbits)
    lo0 = jnp.maximum(lb, INT_MIN + 1)
    hi0 = jnp.minimum(kmax, INT_MAX - 1) + 1

    def open_rows(lo, hi, c_lo):
        return jnp.where(c_lo == ksel, 0.0, jnp.where(lo + 1 >= hi, 0.0, jnp.where(n_adm <= ksel, 0.0, 1.0)))

    def chunk_flags(rows01, rc):
        return tuple(jnp.max(rows01[r0:r0 + rc]) > 0.0 for r0 in range(0, tq, rc))

    def n_open(rows01):
        return jnp.sum(rows01[:, 0:1])

    def halve_twice(lo, hi, c_lo, rc, active):
        for _ in range(2):
            mid = (lo >> 1) + (hi >> 1) + (lo & hi & 1)
            cnt = count(ge, mid, rc=rc, active=active)
            take = cnt >= ksel
            lo = jnp.where(take, mid, lo)
            c_lo = jnp.where(take, cnt, c_lo)
            hi = jnp.where(take, hi, mid)
        return lo, hi, c_lo

    few_rows = 16.0

    def coarse_cond(st):
        return jnp.logical_and(st[0] < 34, st[4] > few_rows)

    def coarse_body(st):
        t, lo, hi, c_lo = st[:4]
        lo, hi, c_lo = halve_twice(lo, hi, c_lo, rc_coarse, st[5:])
        rows01 = open_rows(lo, hi, c_lo)
        return (t + 2, lo, hi, c_lo, n_open(rows01)) + chunk_flags(rows01, rc_coarse)

    def fine_cond(st):
        return jnp.logical_and(st[0] < 34, st[4] > 0.0)

    def fine_body(st):
        t, lo, hi, c_lo = st[:4]
        lo, hi, c_lo = halve_twice(lo, hi, c_lo, rc_fine, st[5:])
        rows01 = open_rows(lo, hi, c_lo)
        return (t + 2, lo, hi, c_lo, n_open(rows01)) + chunk_flags(rows01, rc_fine)

    c_lo0 = count(ge, lo0)
    rows01 = open_rows(lo0, hi0, c_lo0)
    st = lax.while_loop(coarse_cond, coarse_body,
                        (jnp.int32(0), lo0, hi0, c_lo0, n_open(rows01)) + chunk_flags(rows01, rc_coarse))
    rows01 = open_rows(*st[1:4])
    st = lax.while_loop(fine_cond, fine_body, st[:4] + (n_open(rows01),) + chunk_flags(rows01, rc_fine))
    tau, c_ge = st[1], st[3]

    tie_rows = jnp.where(c_ge > ksel, 1.0, 0.0)
    tie_chunks = chunk_flags(tie_rows, rc_fine)
    any_tie = jnp.max(tie_rows) > 0.0
    need = ksel - count(lambda kt, kp, c: jnp.where(kt > c, 1.0, 0.0), tau, rc=rc_fine, active=tie_chunks)

    def pos_body(t, p):
        cand = p | lax.shift_left(jnp.int32(1), pos_bits - 1 - t)
        cnt = count(lambda kt, kp, tv, cv: jnp.where(kt == tv, jnp.where(kp < cv, 1.0, 0.0), 0.0), tau, cand,
                    rc=rc_fine, active=tie_chunks)
        return jnp.where(cnt < need, cand, p)

    p_lim = lax.fori_loop(0, jnp.where(any_tie, pos_bits, 0), pos_body, jnp.zeros((tq, LANES), I32))
    p_lim = jnp.where(tie_rows > 0.0, p_lim, INT_MAX)

    _init_online(m_scr, acc_scr)
    q_stacks = _stack_maps(q_ref, 2)
    ta = 2 * tk
    tau_t = _lane_tile(tau, ta)
    p_lim_t = _lane_tile(p_lim, ta)
    acol = lax.broadcasted_iota(I32, (1, ta), 1)

    def att_body(j, carry):
        start = pl.multiple_of(j * ta, ta)
        kt = key_scr[:, pl.ds(start, ta)]
        bias = jnp.where(kt > tau_t, 0.0, jnp.where(kt == tau_t, jnp.where(start + acol <= p_lim_t, 0.0, NEG), NEG))
        ss = _map_scores(q_stacks, k_ref, start, ta, tq)
        for h in range(HEADS):
            _online_update([h], [ss[h] + bias], v_ref[pl.ds(start, ta), h * LANES:(h + 1) * LANES], m_scr, acc_scr)
        return carry

    lax.fori_loop(0, npair, att_body, 0)
    o_ref[...] = jnp.concatenate([_normalized(acc_scr[h]) for h in range(HEADS)], axis=1).astype(BF16)


def _dsa(qi, misc, q, ki, k, v, *, q_off, l_valid, tq):
    b, t, _ = q.shape
    lp = v.shape[1]
    tk = 512
    assert lp % (2 * tk) == 0 and t % tq == 0
    ksel = min(TOPK_MAX, l_valid // 4)
    qtok = lambda width: pl.BlockSpec((None, tq, width), lambda bi, i: (bi, i, 0))
    res = lambda width: pl.BlockSpec((None, lp, width), lambda bi, i: (bi, 0, 0), pipeline_mode=pl.Buffered(1))
    res_t = lambda rows: pl.BlockSpec((None, rows, lp), lambda bi, i: (bi, 0, 0), pipeline_mode=pl.Buffered(1))
    kern = functools.partial(_dsa_kernel, tq=tq, tk=tk, q_off=q_off, l_valid=l_valid, ksel=float(ksel),
                             pos_bits=int(lp - 1).bit_length())
    return pl.pallas_call(
        kern,
        grid=(b, t // tq),
        in_specs=[qtok(IDX_HEADS * IDX_DIM), qtok(MISC_W), qtok(HEADS * LANES), res_t(MISC_W), res_t(GROUP_W),
                  res(HEADS * LANES)],
        out_specs=qtok(GROUP_W),
        out_shape=jax.ShapeDtypeStruct((b, t, GROUP_W), BF16),
        scratch_shapes=[pltpu.VMEM((tq, lp), I32), pltpu.VMEM((tq, tk), I32)] + _attn_scratch(HEADS, tq),
        compiler_params=_cparams(2),
        name="dsa",
    )(qi, misc, q, ki, k, v)


def _fox_kernel(q_ref, cq_ref, k_ref, v_ref, ck_ref, o_ref, m_scr, acc_scr, *, tq, tk, q_off, l_valid):
    i = pl.program_id(1)
    q0 = q_off + i * tq
    qpos = q0 + lax.broadcasted_iota(I32, (tq, 1), 0)
    nblk = (jnp.minimum(q0 + tq, l_valid) + tk - 1) // tk
    nfull = jnp.minimum((q0 + 1) // tk, l_valid // tk)
    kcol = lax.broadcasted_iota(I32, (1, tk), 1)
    cq = cq_ref[...]
    cqs = [jnp.broadcast_to(cq[:, h:h + 1], (tq, LANES)) for h in range(HEADS)]
    _init_online(m_scr, acc_scr)
    q_stacks = _stack_maps(q_ref, 2)

    def body(j, carry, masked):
        start = pl.multiple_of(j * tk, tk)
        if masked:
            kpos = start + kcol
            valid = kpos <= qpos
        ss = _map_scores(q_stacks, k_ref, start, tk, tq)
        for h in range(HEADS):
            s = ss[h] - ck_ref[h:h + 1, pl.ds(start, tk)]
            if masked:
                s = jnp.where(valid, s, NEG)
            _online_update([h], [s], v_ref[pl.ds(start, tk), h * LANES:(h + 1) * LANES], m_scr, acc_scr,
                           shift=cqs[h])
        return carry

    lax.fori_loop(0, nfull, functools.partial(body, masked=False), 0)
    lax.fori_loop(nfull, nblk, functools.partial(body, masked=True), 0)
    o_ref[...] = jnp.concatenate([_normalized(acc_scr[h]) for h in range(HEADS)], axis=1).astype(BF16)


def _fox(q, cq, k, v, ck_t, *, q_off, l_valid, tq):
    b, t, _ = q.shape
    lp = v.shape[1]
    tk = 1024
    assert lp % tk == 0 and t % tq == 0
    qtok = lambda width: pl.BlockSpec((None, tq, width), lambda bi, i: (bi, i, 0))
    res = lambda rows, width: pl.BlockSpec((None, rows, width), lambda bi, i: (bi, 0, 0),
                                           pipeline_mode=pl.Buffered(1))
    kern = functools.partial(_fox_kernel, tq=tq, tk=tk, q_off=q_off, l_valid=l_valid)
    return pl.pallas_call(
        kern,
        grid=(b, t // tq),
        in_specs=[qtok(HEADS * LANES), qtok(HEADS), res(GROUP_W, lp), res(lp, HEADS * LANES), res(HEADS, lp)],
        out_specs=qtok(GROUP_W),
        out_shape=jax.ShapeDtypeStruct((b, t, GROUP_W), BF16),
        scratch_shapes=_attn_scratch(HEADS, tq),
        compiler_params=_cparams(2),
        name="fox",
    )(q, cq, k, v, ck_t)


def _diff_kernel(q_ref, k_ref, v_ref, lam_ref, gsub_ref, o_ref, m_scr, acc_scr,
                 *, tq, tk, q_off, l_valid, out_scale):
    i = pl.program_id(1)
    q0 = q_off + i * tq
    qchunk = (q0 + lax.broadcasted_iota(I32, (tq, 1), 0)) // CHUNK
    n_end = jnp.minimum(((q0 + tq - 1) // CHUNK + 1) * CHUNK, l_valid)
    nblk = (n_end + tk - 1) // tk
    nfull = jnp.minimum(((q0 // CHUNK + 1) * CHUNK) // tk, l_valid // tk)
    kcol = lax.broadcasted_iota(I32, (1, tk), 1)
    _init_online(m_scr, acc_scr)
    q_stacks = _stack_maps(q_ref, 4)

    def body(j, carry, masked):
        start = pl.multiple_of(j * tk, tk)
        if masked:
            kpos = start + kcol
            valid = jnp.where(kpos < l_valid, kpos // CHUNK, INT_MAX) <= qchunk
        ss = _map_scores(q_stacks, k_ref, start, tk, tq)
        if masked:
            ss = [jnp.where(valid, s, NEG) for s in ss]
        for h in range(HEADS):
            _online_update([2 * h, 2 * h + 1], ss[2 * h:2 * h + 2], v_ref[pl.ds(start, tk), h * LANES:(h + 1) * LANES],
                           m_scr, acc_scr)
        return carry

    lax.fori_loop(0, nfull, functools.partial(body, masked=False), 0)
    lax.fori_loop(nfull, nblk, functools.partial(body, masked=True), 0)

    lv = lam_ref[...]
    lam = (jnp.exp(jnp.sum(lv[0:1] * lv[1:2], axis=1, keepdims=True))
           - jnp.exp(jnp.sum(lv[2:3] * lv[3:4], axis=1, keepdims=True)) + (1.0 - out_scale))
    outs = []
    for h in range(HEADS):
        o = _normalized(acc_scr[2 * h]) - lam * _normalized(acc_scr[2 * h + 1])
        outs.append(_rms(o, gsub_ref[...]) * out_scale)
    o_ref[...] = jnp.concatenate(outs, axis=1).astype(BF16)


def _diff(q, k, v, lam_vecs, g_sub, *, q_off, l_valid, tq, lam_init):
    b, t, _ = q.shape
    lp = v.shape[1]
    tk = 512
    assert lp % tk == 0 and t % tq == 0
    qtok = lambda width: pl.BlockSpec((None, tq, width), lambda bi, i: (bi, i, 0))
    res = lambda rows, width: pl.BlockSpec((None, rows, width), lambda bi, i: (bi, 0, 0),
                                           pipeline_mode=pl.Buffered(1))
    const = lambda shape: pl.BlockSpec(shape, lambda bi, i: (0, 0))
    kern = functools.partial(_diff_kernel, tq=tq, tk=tk, q_off=q_off, l_valid=l_valid, out_scale=1.0 - lam_init)
    return pl.pallas_call(
        kern,
        grid=(b, t // tq),
        in_specs=[qtok(2 * HEADS * LANES), res(GROUP_W, lp), res(lp, HEADS * LANES), const((4, DIFF_DIM)),
                  const((1, HEAD_DIM))],
        out_specs=qtok(GROUP_W),
        out_shape=jax.ShapeDtypeStruct((b, t, GROUP_W), BF16),
        scratch_shapes=_attn_scratch(2 * HEADS, tq),
        compiler_params=_cparams(2),
        name="diff",
    )(q, k, v, lam_vecs, g_sub.reshape(1, HEAD_DIM))


def _relbias_kernel(rb_ref, o_ref, *, tq):
    w = BAND_PREV + tq
    pw = 2 * w
    j = lax.broadcasted_iota(I32, (8, pw), 1)
    off = jnp.where(j < w, j, j - pw)
    idx = jnp.clip(BAND_PREV - off, -REL_CLIP, REL_CLIP) + REL_CLIP

    def body(r, accs):
        return tuple(jnp.where(idx == r, rb_ref[h, r], accs[h]) for h in range(HEADS))

    profs = lax.fori_loop(0, 2 * REL_CLIP + 1, body, tuple(jnp.zeros((8, pw), F32) for _ in range(HEADS)))
    qc = BAND_CHUNKS + lax.broadcasted_iota(I32, (tq, w), 0) // CHUNK
    kc = lax.broadcasted_iota(I32, (tq, w), 1) // CHUNK
    for h in range(HEADS):
        full = jnp.broadcast_to(profs[h][0:1], (tq, pw))
        t = pltpu.roll(full, 0, 1, stride=1, stride_axis=0)[:, 0:w]
        t = jnp.where(kc <= qc, t, NEG)
        o_ref[h] = jnp.where(kc >= qc - BAND_CHUNKS, t, NEG)


def _relbias(rel_bias_l, tq):
    return pl.pallas_call(
        functools.partial(_relbias_kernel, tq=tq),
        in_specs=[pl.BlockSpec(memory_space=pltpu.SMEM)],
        out_specs=pl.BlockSpec(memory_space=pltpu.VMEM),
        out_shape=jax.ShapeDtypeStruct((HEADS, tq, BAND_PREV + tq), F32),
        compiler_params=pltpu.CompilerParams(vmem_limit_bytes=VMEM_LIMIT_BYTES),
        name="relbias",
    )(rel_bias_l.astype(F32))


def _band_kernel(q_ref, kp_ref, kc_ref, vp_ref, vc_ref, bias_ref, o_ref, k_scr, v_scr, *, tq, first_has_no_prev):
    i = pl.program_id(1)
    w = BAND_PREV + tq
    k_scr[0:BAND_PREV, :] = kp_ref[...]
    k_scr[BAND_PREV:w, :] = kc_ref[...]
    v_scr[0:BAND_PREV, :] = vp_ref[...]
    v_scr[BAND_PREV:w, :] = vc_ref[...]
    kcol = lax.broadcasted_iota(I32, (1, w), 1)
    first = jnp.where(i == 0, BAND_PREV, 0) if first_has_no_prev else 0
    outs = []
    for h in range(HEADS):
        hs = slice(h * HEAD_DIM, (h + 1) * HEAD_DIM)
        s = _dot_nt(q_ref[:, hs], k_scr[:, hs]) + bias_ref[h]
        if first_has_no_prev:
            s = jnp.where(kcol >= first, s, NEG)
        e = jnp.exp(s - jnp.max(s, axis=1, keepdims=True))
        o = _dot(e.astype(BF16), v_scr[:, hs])
        outs.append(o / jnp.sum(e, axis=1, keepdims=True))
    o_ref[...] = jnp.concatenate(outs, axis=1).astype(BF16)


def _band(q, k_prev, k_cur, v_prev, v_cur, bias, *, tq, prev_from_cur):
    b, t, _ = q.shape
    assert t % tq == 0 and (not prev_from_cur or tq == BAND_PREV)
    qtok = pl.BlockSpec((None, tq, GROUP_W), lambda bi, i: (bi, i, 0))
    if prev_from_cur:
        prev = pl.BlockSpec((None, BAND_PREV, GROUP_W), lambda bi, i: (bi, jnp.maximum(i - 1, 0), 0))
    else:
        prev = pl.BlockSpec((None, BAND_PREV, GROUP_W), lambda bi, i: (bi, 0, 0))
    bspec = pl.BlockSpec((HEADS, tq, BAND_PREV + tq), lambda bi, i: (0, 0, 0))
    kern = functools.partial(_band_kernel, tq=tq, first_has_no_prev=prev_from_cur)
    return pl.pallas_call(
        kern,
        grid=(b, t // tq),
        in_specs=[qtok, prev, qtok, prev, qtok, bspec],
        out_specs=qtok,
        out_shape=jax.ShapeDtypeStruct((b, t, GROUP_W), BF16),
        scratch_shapes=[pltpu.VMEM((BAND_PREV + tq, GROUP_W), BF16), pltpu.VMEM((BAND_PREV + tq, GROUP_W), BF16)],
        compiler_params=_cparams(2),
        name="band",
    )(q, k_prev, k_cur, v_prev, v_cur, bias)


def _memkv_kernel(m_ref, g_ref, wk_ref, wv_ref, kf_o, vf_o, kb_o, vb_o):
    hb = _rms(m_ref[...], g_ref[...]).astype(BF16)
    k = _dot(hb, wk_ref[...])
    v = _dot(hb, wv_ref[...])
    kf_o[...] = k
    vf_o[...] = v
    kb_o[...] = k.astype(BF16)
    vb_o[...] = v.astype(BF16)


def _memkv(mem, g, wk, wv):
    b, m, d = mem.shape
    xw = wk.shape[1]
    tok = lambda width: pl.BlockSpec((None, m, width), lambda bi: (bi, 0, 0))
    const = lambda shape: pl.BlockSpec(shape, lambda bi: (0, 0))
    return pl.pallas_call(
        _memkv_kernel,
        grid=(b,),
        in_specs=[tok(d), const((1, d)), const((d, xw)), const((d, xw))],
        out_specs=[tok(xw)] * 4,
        out_shape=[jax.ShapeDtypeStruct((b, m, xw), F32)] * 2 + [jax.ShapeDtypeStruct((b, m, xw), BF16)] * 2,
        compiler_params=_cparams(1),
        name="memkv",
    )(mem, g.reshape(1, d), wk, wv)


def _mix_cross_kernel(x_ref, oa_ref, ob_ref, oc_ref, od_ref, wout_ref, g_ref, wq_ref, mk_ref, mv_ref, wo_ref, o_ref):
    x = x_ref[...]
    for n, oref in enumerate((oa_ref, ob_ref, oc_ref, od_ref)):
        x = x + _dot(oref[...], wout_ref[n * GROUP_W:(n + 1) * GROUP_W, :])
    hb = _rms(x, g_ref[...]).astype(BF16)
    q = _dot(hb, wq_ref[...])
    outs = []
    for h in range(X_HEADS):
        hs = slice(h * X_HEAD_DIM, (h + 1) * X_HEAD_DIM)
        s = _dot_nt(q[:, hs].astype(BF16), mk_ref[:, hs]) * (X_HEAD_DIM ** -0.5)
        e = jnp.exp(s - jnp.max(s, axis=1, keepdims=True))
        p = e / jnp.sum(e, axis=1, keepdims=True)
        outs.append(_dot(p.astype(BF16), mv_ref[:, hs]))
    o = jnp.concatenate(outs, axis=1).astype(BF16)
    o_ref[...] = x + _dot(o, wo_ref[...])


def _mix_cross(x, oa, ob, oc, od, w_out, g, wq, mk, mv, wo):
    b, t, d = x.shape
    tm = min(512, t)
    m, xw = mk.shape[1], mk.shape[2]
    tok = lambda width: pl.BlockSpec((None, tm, width), lambda bi, i: (bi, i, 0))
    const = lambda shape: pl.BlockSpec(shape, lambda bi, i: (0, 0))
    mem = pl.BlockSpec((None, m, xw), lambda bi, i: (bi, 0, 0))
    return pl.pallas_call(
        _mix_cross_kernel,
        grid=(b, t // tm),
        in_specs=[tok(d)] + [tok(GROUP_W)] * 4 + [const(w_out.shape), const((1, d)), const(wq.shape), mem, mem,
                                                   const(wo.shape)],
        out_specs=tok(d),
        out_shape=jax.ShapeDtypeStruct((b, t, d), F32),
        compiler_params=_cparams(2),
        name="mix_cross",
    )(x, oa, ob, oc, od, w_out, g.reshape(1, d), wq, mk, mv, wo)


def _ffn_kernel(x_ref, g_ref, prev_ref, wup_ref, cw_ref, cb_ref, wdn_ref, o_ref, conv_ref, halo_scr, acc_scr,
                *, tm, d_ff, fc):
    i = pl.program_id(1)

    @pl.when(i == 0)
    def _():
        halo_scr[...] = prev_ref[...]

    x = x_ref[...]
    hb = _rms(x, g_ref[...]).astype(BF16)
    row = lax.broadcasted_iota(I32, (tm, fc), 0)
    for c in range(d_ff // fc):
        cs = slice(c * fc, (c + 1) * fc)
        g = _dot(hb, wup_ref[:, cs])
        u = _dot(hb, wup_ref[:, d_ff + c * fc:d_ff + (c + 1) * fc])
        h0 = halo_scr[0:1, cs]
        h1 = halo_scr[1:2, cs]
        gm1 = jnp.where(row == 0, h1, pltpu.roll(g, 1, 0))
        gm2 = jnp.where(row == 0, h0, jnp.where(row == 1, h1, pltpu.roll(g, 2, 0)))
        a = cb_ref[:, cs] + gm2 * cw_ref[0:1, cs]
        a = a + gm1 * cw_ref[1:2, cs]
        a = a + g * cw_ref[2:3, cs]
        y = (a * (1.0 / (1.0 + jnp.exp(-a)))) * u
        f = _dot(y.astype(BF16), wdn_ref[cs, :])
        if c == 0:
            acc_scr[...] = f
        else:
            acc_scr[...] += f
        tail = g[tm - (CONV_W - 1):tm, :]
        halo_scr[:, cs] = tail
        conv_ref[:, cs] = tail
    o_ref[...] = x + acc_scr[...]


def _ffn(x, g, prev, w_up, conv_w, conv_b, w_down):
    b, t, d = x.shape
    d_ff = w_down.shape[0]
    tm = min(1024, t)
    fc = 256
    assert d_ff % fc == 0 and t % tm == 0 and tm >= CONV_W - 1
    tok = pl.BlockSpec((None, tm, d), lambda bi, i: (bi, i, 0))
    const = lambda shape: pl.BlockSpec(shape, lambda bi, i: (0, 0))
    st = pl.BlockSpec((None, CONV_W - 1, d_ff), lambda bi, i: (bi, 0, 0))
    kern = functools.partial(_ffn_kernel, tm=tm, d_ff=d_ff, fc=fc)
    return pl.pallas_call(
        kern,
        grid=(b, t // tm),
        in_specs=[tok, const((1, d)), st, const(w_up.shape), const(conv_w.shape), const((1, d_ff)),
                  const(w_down.shape)],
        out_specs=[tok, st],
        out_shape=[jax.ShapeDtypeStruct((b, t, d), F32), jax.ShapeDtypeStruct((b, CONV_W - 1, d_ff), F32)],
        scratch_shapes=[pltpu.VMEM((CONV_W - 1, d_ff), F32), pltpu.VMEM((tm, d), F32)],
        compiler_params=_cparams(2),
        name="ffn",
    )(x, g.reshape(1, d), prev, w_up, conv_w, conv_b.reshape(1, d_ff), w_down)


def _rmsnorm_kernel(x_ref, g_ref, o_ref):
    o_ref[...] = _rms(x_ref[...], g_ref[...])


def _rmsnorm(x, g):
    b, t, d = x.shape
    tm = min(1024, t)
    tok = pl.BlockSpec((None, tm, d), lambda bi, i: (bi, i, 0))
    return pl.pallas_call(
        _rmsnorm_kernel,
        grid=(b, t // tm),
        in_specs=[tok, pl.BlockSpec((1, d), lambda bi, i: (0, 0))],
        out_specs=tok,
        out_shape=jax.ShapeDtypeStruct((b, t, d), F32),
        compiler_params=_cparams(2),
        name="final_norm",
    )(x, g.reshape(1, d))


def _pad_rows(a, rows):
    return a if a.shape[1] == rows else jnp.pad(a, ((0, 0), (0, rows - a.shape[1]), (0, 0)))


def _round_up(n, m):
    return (n + m - 1) // m * m


def _value_slabs_host(v):
    b, l = v.shape[:2]
    one = jnp.zeros((b, l, HEADS, HEAD_DIM), BF16).at[..., 0].set(1.0)
    return jnp.concatenate([v.astype(BF16), one], axis=-1).reshape(b, l, HEADS * LANES)


def _mixers(pr, keys, *, q_off, l_valid, tq_a, tq_bd, band_args, band_tq, prev_from_cur, lam_vecs, g_sub,
            lam_init, logf_all):
    lp = keys["av"].shape[1]
    oa = _dsa(pr["aqi"], pr["misc"], pr["aq"], keys["aki"], keys["ak"], keys["av"],
              q_off=q_off, l_valid=l_valid, tq=tq_a)
    lc = _round_up(logf_all.shape[1], 1024)
    c_t = _cumsum_t(jnp.swapaxes(_pad_rows(logf_all, lc), 1, 2))
    t = pr["bq"].shape[1]
    cq = jnp.swapaxes(c_t[:, :, q_off:q_off + t], 1, 2)
    ck_t = c_t[:, :, :lp] if lc >= lp else jnp.pad(c_t, ((0, 0), (0, 0), (0, lp - lc)))
    ob = _fox(pr["bq"], cq, keys["bk"], keys["bv"], ck_t, q_off=q_off, l_valid=l_valid, tq=tq_bd)
    oc = _band(pr["cq"], *band_args, tq=band_tq, prev_from_cur=prev_from_cur)
    od = _diff(pr["dq"], keys["dk"], keys["dv"], lam_vecs, g_sub, q_off=q_off, l_valid=l_valid, tq=tq_bd,
               lam_init=lam_init)
    return oa, ob, oc, od


def kernel(x_prompt, x_sample, mem_prompt, cache_a_k, cache_a_v, cache_a_kidx, cache_b_k, cache_b_v,
           cache_b_logf, cache_c_k, cache_c_v, cache_d_k, cache_d_v, cache_mem_k, cache_mem_v,
           state_ffn_conv, g_mix, w_in, b_forget, rel_bias, lam_q1, lam_k1, lam_q2, lam_k2, g_sub, w_out,
           g_cross, g_mem, w_xq, w_xk, w_xv, w_xo, g_ffn, w_up, conv_w, conv_b, w_down, g_final):
    depth = w_in.shape[0]
    bp, s_len, d_model = x_prompt.shape
    bs, t_len, _ = x_sample.shape
    past = cache_a_k.shape[2]
    d_ff = w_down.shape[1]
    c_keep = min(BAND_PREV, s_len)
    l_s = past + t_len
    lp_s = _round_up(l_s, 1024)

    pos_p = jnp.arange(s_len, dtype=jnp.int32)
    pos_s = jnp.tile(past + jnp.arange(t_len, dtype=jnp.int32), bs)
    h64 = HEAD_DIM // ROT_FRACTION // 2
    h32 = DIFF_DIM // ROT_FRACTION // 2
    t64_p, t32_p = _rope_table(pos_p, h64, HEAD_DIM), _rope_table(pos_p, h32, DIFF_DIM)
    t64_s, t32_s = _rope_table(pos_s, h64, HEAD_DIM), _rope_table(pos_s, h32, DIFF_DIM)

    flat = lambda a: a.reshape(a.shape[0], a.shape[1], -1)
    heads = lambda a, hd=HEAD_DIM: a.reshape(a.shape[0], a.shape[1], a.shape[2] // hd, hd)

    xp, xs = x_prompt, x_sample
    p_states, s_states = [], []
    for l in range(depth):
        lam_init = 0.8 - 0.6 * math.exp(-0.3 * l)
        w_pack, bfp = _pack_w_in(w_in[l], b_forget[l])
        lam_vecs = jnp.stack([lam_q1[l], lam_k1[l], lam_q2[l], lam_k2[l]]).astype(F32)
        bias = _relbias(rel_bias[l], BAND_PREV)
        bias_s = bias[:, :t_len, :BAND_PREV + t_len]
        w_out_b, wq_b, wo_b = w_out[l].astype(BF16), w_xq[l].astype(BF16), w_xo[l].astype(BF16)
        w_up_b, w_dn_b = w_up[l].astype(BF16), w_down[l].astype(BF16)

        pr = _proj(xp, g_mix[l], w_pack, bfp, t64_p, t32_p, c_keep)
        logf = pr["misc"][:, :, MISC_BF:MISC_BF + HEADS]
        oa, ob, oc, od = _mixers(
            pr, pr, q_off=0, l_valid=s_len, tq_a=256, tq_bd=512,
            band_args=(pr["ck"], pr["ck"], pr["cv"], pr["cv"], bias), band_tq=BAND_PREV, prev_from_cur=True,
            lam_vecs=lam_vecs, g_sub=g_sub[l], lam_init=lam_init, logf_all=logf)
        mkf, mvf, mkb, mvb = _memkv(mem_prompt, g_mem[l], w_xk[l].astype(BF16), w_xv[l].astype(BF16))
        xp = _mix_cross(xp, oa, ob, oc, od, w_out_b, g_cross[l], wq_b, mkb, mvb, wo_b)
        xp, conv_p = _ffn(xp, g_ffn[l], jnp.zeros((bp, CONV_W - 1, d_ff), F32), w_up_b, conv_w[l], conv_b[l], w_dn_b)
        p_states.append((heads(pr["akf"]), heads(pr["avf"]), pr["misc"][:, :, :IDX_DIM],
                         heads(pr["bkf"]), heads(pr["bvf"]), logf,
                         heads(pr["ckf"]), heads(pr["cvf"]), heads(pr["dkf"]), heads(pr["dvf"]),
                         conv_p, heads(mkf, X_HEAD_DIM), heads(mvf, X_HEAD_DIM)))

        prs = _proj(xs.reshape(1, bs * t_len, d_model), g_mix[l], w_pack, bfp, t64_s, t32_s, bs * t_len)
        prs = {n: a.reshape(bs, t_len, a.shape[-1]) for n, a in prs.items() if n not in ("ak", "aki", "bk", "dk")}
        cat_t = lambda cache, new: jnp.swapaxes(
            _pad_rows(jnp.concatenate([flat(cache), new], axis=1), lp_s).astype(BF16), 1, 2)
        kidx = jnp.concatenate([cache_a_kidx[l], prs["misc"][:, :, :IDX_DIM]], axis=1)
        kidx = jnp.pad(kidx, ((0, 0), (0, lp_s - l_s), (0, MISC_W - IDX_DIM)))
        catv = lambda cache, new: _pad_rows(jnp.concatenate([_value_slabs_host(cache), new], axis=1), lp_s)
        keys = {"aki": jnp.swapaxes(kidx.astype(BF16), 1, 2),
                "ak": cat_t(cache_a_k[l], prs["akf"]), "av": catv(cache_a_v[l], prs["av"]),
                "bk": cat_t(cache_b_k[l], prs["bkf"]), "bv": catv(cache_b_v[l], prs["bv"]),
                "dk": cat_t(cache_d_k[l], prs["dkf"]), "dv": catv(cache_d_v[l], prs["dv"])}
        logf_s = prs["misc"][:, :, MISC_BF:MISC_BF + HEADS]
        logf_all = jnp.concatenate([cache_b_logf[l].astype(F32), logf_s], axis=1)
        oa, ob, oc, od = _mixers(
            prs, keys, q_off=past, l_valid=l_s, tq_a=t_len, tq_bd=t_len,
            band_args=(flat(cache_c_k[l]).astype(BF16), prs["ck"], flat(cache_c_v[l]).astype(BF16), prs["cv"],
                       bias_s),
            band_tq=t_len, prev_from_cur=False,
            lam_vecs=lam_vecs, g_sub=g_sub[l], lam_init=lam_init, logf_all=logf_all)
        xs = _mix_cross(xs, oa, ob, oc, od, w_out_b, g_cross[l], wq_b,
                        flat(cache_mem_k[l]).astype(BF16), flat(cache_mem_v[l]).astype(BF16), wo_b)
        xs, conv_s = _ffn(xs, g_ffn[l], state_ffn_conv[l], w_up_b, conv_w[l], conv_b[l], w_dn_b)
        s_states.append((heads(prs["akf"]), heads(prs["avf"]), prs["misc"][:, :, :IDX_DIM],
                         heads(prs["bkf"]), heads(prs["bvf"]), logf_s,
                         heads(prs["ckf"]), heads(prs["cvf"]), heads(prs["dkf"]), heads(prs["dvf"]), conv_s))

    y_prompt = _rmsnorm(xp, g_final)
    y_sample = _rmsnorm(xs, g_final)
    p_out = [jnp.stack(z, axis=0) for z in zip(*p_states)]
    s_out = [jnp.stack(z, axis=0) for z in zip(*s_states)]
    return (y_prompt, y_sample, *p_out, *s_out)
```

```python
import functools
import math

import numpy as np
import jax
import jax.numpy as jnp
from jax import lax
from jax.experimental import pallas as pl
from jax.experimental.pallas import tpu as pltpu

F32 = jnp.float32
BF16 = jnp.bfloat16
I32 = jnp.int32

CHUNK = 64
HEAD_DIM = 64
HEADS = 4
GROUP_W = HEADS * HEAD_DIM
ROT_FRACTION = 4
ROPE_THETA = 500000.0
IDX_HEADS = 8
IDX_DIM = 64
TOPK_MAX = 256
BAND_CHUNKS = 8
BAND_KEYS = (BAND_CHUNKS + 1) * CHUNK
BAND_PREV = BAND_CHUNKS * CHUNK
REL_CLIP = 128
DIFF_DIM = HEAD_DIM // 2
X_HEADS = 4
X_HEAD_DIM = 128
CONV_W = 3
EPS = 1e-6

LANES = 128
VMEM_LIMIT_BYTES = 56 * 1024 * 1024

NEG = -1e30
INT_MIN = -2147483648
INT_MAX = 2147483647

MISC_W = LANES
MISC_AW = IDX_DIM
MISC_BF = IDX_DIM + IDX_HEADS
PROJ_COLS = 3 * GROUP_W + IDX_HEADS * IDX_DIM + MISC_W + 9 * GROUP_W


def _cparams(n_axes):
    return pltpu.CompilerParams(dimension_semantics=("arbitrary",) * n_axes,
                                vmem_limit_bytes=VMEM_LIMIT_BYTES)


def _rms(x, g):
    return (x * lax.rsqrt(jnp.mean(x * x, axis=-1, keepdims=True) + EPS)) * g


def _dot(a, b):
    return jnp.dot(a, b, preferred_element_type=F32)


def _dot_nt(a, b):
    return lax.dot_general(a, b, (((1,), (1,)), ((), ())), preferred_element_type=F32)


def _rope(z, tab, half):
    w = z.shape[1]
    reps = w // LANES
    rep = lambda t: t if reps == 1 else jnp.concatenate([t] * reps, axis=1)
    c = rep(tab[:, 0:LANES])
    s1 = rep(tab[:, LANES:2 * LANES])
    s2 = rep(tab[:, 2 * LANES:3 * LANES])
    return z * c + pltpu.roll(z, w - half, 1) * s1 + pltpu.roll(z, half, 1) * s2


def _slab_masked(z, widths):
    lane = lax.broadcasted_iota(I32, (z.shape[0], LANES), 1)
    outs = []
    for j in range(GROUP_W // widths):
        lo = (j * widths) % LANES
        zs = z[:, (j * widths) // LANES * LANES:((j * widths) // LANES + 1) * LANES]
        outs.append(jnp.where(lane < lo, 0.0, jnp.where(lane < lo + widths, zs, 0.0)))
    return jnp.concatenate(outs, axis=1)


def _value_slabs(z):
    one = jnp.where(lax.broadcasted_iota(I32, (z.shape[0], HEAD_DIM), 1) == 0, 1.0, 0.0)
    return jnp.concatenate([t for h in range(HEADS) for t in (z[:, h * HEAD_DIM:(h + 1) * HEAD_DIM], one)], axis=1)


def _proj_kernel(x_ref, g_ref, w_ref, bf_ref, t64_ref, t32_ref,
                 aq_o, ak_o, av_o, aqi_o, aki_o, bq_o, bk_o, bv_o, cq_o, ck_o, cv_o, dq_o, dk_o, dv_o,
                 akf_o, avf_o, misc_o, bkf_o, bvf_o, ckf_o, cvf_o, dkf_o, dvf_o):
    hb = _rms(x_ref[...], g_ref[...]).astype(BF16)
    t64 = t64_ref[...]
    t32 = t32_ref[...]
    h64 = HEAD_DIM // ROT_FRACTION // 2
    h32 = DIFF_DIM // ROT_FRACTION // 2
    qscale = HEAD_DIM ** -0.5
    col = [0]

    def mm(width):
        c0 = col[0]
        col[0] = c0 + width
        return _dot(hb, w_ref[:, c0:c0 + width])

    z = _rope(mm(GROUP_W), t64, h64)
    aq_o[...] = _slab_masked(z * qscale, HEAD_DIM).astype(BF16)
    z = _rope(mm(GROUP_W), t64, h64)
    akf_o[...] = z
    ak_o[...] = z.T.astype(BF16)
    z = mm(GROUP_W)
    avf_o[...] = z
    av_o[...] = _value_slabs(z).astype(BF16)
    z = _rope(mm(IDX_HEADS * IDX_DIM), t64, h64)
    aqi_o[...] = (z * (IDX_DIM ** -0.5)).astype(BF16)
    z = mm(MISC_W)
    r = _rope(z, t64, h64)
    lane = lax.broadcasted_iota(I32, z.shape, 1)
    zf = z + bf_ref[...]
    logsig = jnp.minimum(zf, 0.0) - jnp.log1p(jnp.exp(-jnp.abs(zf)))
    misc_o[...] = jnp.where(lane < MISC_AW, r, jnp.where(lane < MISC_BF, z * (IDX_HEADS ** -0.5), logsig))
    aki_o[...] = jnp.where(lane < MISC_AW, r, 0.0).T.astype(BF16)
    z = mm(GROUP_W)
    bq_o[...] = _slab_masked(z * qscale, HEAD_DIM).astype(BF16)
    z = mm(GROUP_W)
    bkf_o[...] = z
    bk_o[...] = z.T.astype(BF16)
    z = mm(GROUP_W)
    bvf_o[...] = z
    bv_o[...] = _value_slabs(z).astype(BF16)
    z = mm(GROUP_W)
    cq_o[...] = (z * qscale).astype(BF16)
    z = mm(GROUP_W)
    ckf_o[...] = z
    ck_o[...] = z.astype(BF16)
    z = mm(GROUP_W)
    cvf_o[...] = z
    cv_o[...] = z.astype(BF16)
    z = _rope(mm(GROUP_W), t32, h32)
    dq_o[...] = _slab_masked(z * (DIFF_DIM ** -0.5), DIFF_DIM).astype(BF16)
    z = _rope(mm(GROUP_W), t32, h32)
    dkf_o[...] = z
    dk_o[...] = z.T.astype(BF16)
    z = mm(GROUP_W)
    dvf_o[...] = z
    dv_o[...] = _value_slabs(z).astype(BF16)


def _rope_table(pos, half, period):
    inv = ROPE_THETA ** (-jnp.arange(half, dtype=F32) / half)
    ang = pos.astype(F32)[:, None] * inv[None, :]
    cos, sin = jnp.cos(ang), jnp.sin(ang)
    t = pos.shape[0]
    rest = period - 2 * half
    zh = jnp.zeros((t, half), F32)
    zr = jnp.zeros((t, rest), F32)
    c = jnp.concatenate([cos, cos, jnp.ones((t, rest), F32)], axis=1)
    s1 = jnp.concatenate([-sin, zh, zr], axis=1)
    s2 = jnp.concatenate([zh, sin, zr], axis=1)
    rep = lambda a: jnp.tile(a, (1, LANES // period))
    return jnp.concatenate([rep(c), rep(s1), rep(s2)], axis=1)


def _pack_w_in(w_in_l, b_forget_l):
    sizes = (GROUP_W, GROUP_W, GROUP_W, IDX_HEADS * IDX_DIM, IDX_DIM, IDX_HEADS,
             GROUP_W, GROUP_W, GROUP_W, HEADS, GROUP_W, GROUP_W, GROUP_W, GROUP_W, GROUP_W, GROUP_W)
    pts = [int(p) for p in np.cumsum(sizes)[:-1]]
    (aq, ak, av, aqi, aki, aw, bq, bk, bv, bf, cq, ck, cv, dq, dk, dv) = jnp.split(w_in_l, pts, axis=1)
    d = w_in_l.shape[0]
    pad = jnp.zeros((d, MISC_W - IDX_DIM - IDX_HEADS - HEADS), w_in_l.dtype)
    misc = jnp.concatenate([aki, aw, bf, pad], axis=1)
    w = jnp.concatenate([aq, ak, av, aqi, misc, bq, bk, bv, cq, ck, cv, dq, dk, dv], axis=1).astype(BF16)
    bfp = jnp.zeros((1, MISC_W), F32).at[0, MISC_BF:MISC_BF + HEADS].set(b_forget_l.astype(F32))
    return w, bfp


def _proj(x, g, w, bfp, t64, t32, c_keep):
    b, t, d = x.shape
    tm = min(512, t)
    nt = t // tm
    nkeep = c_keep // tm
    tok = lambda width: pl.BlockSpec((None, tm, width), lambda bi, i: (bi, i, 0))
    ctok = pl.BlockSpec((None, tm, GROUP_W), lambda bi, i: (bi, jnp.maximum(i - (nt - nkeep), 0), 0))
    const = lambda shape: pl.BlockSpec(shape, lambda bi, i: (0, 0))
    tab = pl.BlockSpec((tm, 3 * LANES), lambda bi, i: (i, 0))
    sds = lambda width, dt, rows=t: jax.ShapeDtypeStruct((b, rows, width), dt)
    bf_names = ["aq", "ak", "av", "aqi", "aki", "bq", "bk", "bv", "cq", "ck", "cv", "dq", "dk", "dv"]
    qx_w, vx_w = HEADS * LANES, HEADS * LANES
    bf_w = [qx_w, GROUP_W, vx_w, IDX_HEADS * IDX_DIM, MISC_W, qx_w, GROUP_W, vx_w, GROUP_W, GROUP_W, GROUP_W,
            2 * qx_w, GROUP_W, vx_w]
    f_names = ["akf", "avf", "misc", "bkf", "bvf", "ckf", "cvf", "dkf", "dvf"]
    f_w = [GROUP_W, GROUP_W, MISC_W] + [GROUP_W] * 6
    out_shape = [sds(wd, BF16) for wd in bf_w]
    out_specs = [tok(wd) for wd in bf_w]
    for n, wd in zip(bf_names, bf_w):
        if n in ("ak", "aki", "bk", "dk"):
            out_shape[bf_names.index(n)] = jax.ShapeDtypeStruct((b, wd, t), BF16)
            out_specs[bf_names.index(n)] = pl.BlockSpec((None, wd, tm), lambda bi, i: (bi, 0, i))
    for n, wd in zip(f_names, f_w):
        if n in ("ckf", "cvf"):
            out_shape.append(sds(wd, F32, c_keep))
            out_specs.append(ctok)
        else:
            out_shape.append(sds(wd, F32))
            out_specs.append(tok(wd))
    outs = pl.pallas_call(
        _proj_kernel,
        grid=(b, nt),
        in_specs=[tok(d), const((1, d)), const((d, PROJ_COLS)), const((1, MISC_W)), tab, tab],
        out_specs=out_specs,
        out_shape=out_shape,
        compiler_params=_cparams(2),
        name="proj",
    )(x, g.reshape(1, d), w, bfp, t64, t32)
    return dict(zip(bf_names + f_names, outs))


def _cumsum_kernel(x_ref, o_ref, *, nb):
    x = x_ref[...]
    lane = lax.broadcasted_iota(I32, x.shape, 1)
    d = 1
    while d < LANES:
        x = x + jnp.where(lane >= d, pltpu.roll(x, d, 1), 0.0)
        d *= 2
    row = lax.broadcasted_iota(I32, x.shape, 0) % nb
    tot = jnp.broadcast_to(x[:, LANES - 1:LANES], x.shape)
    exc = jnp.where(row >= 1, pltpu.roll(tot, 1, 0), 0.0)
    d = 1
    while d < nb:
        exc = exc + jnp.where(row >= d, pltpu.roll(exc, d, 0), 0.0)
        d *= 2
    o_ref[...] = x + exc


def _cumsum_t(logf_t):
    b, h, l = logf_t.shape
    nb = l // LANES
    x = logf_t.reshape(b, h * nb, LANES)
    spec = pl.BlockSpec((None, h * nb, LANES), lambda bi: (bi, 0, 0))
    out = pl.pallas_call(
        functools.partial(_cumsum_kernel, nb=nb),
        grid=(b,),
        in_specs=[spec],
        out_specs=spec,
        out_shape=jax.ShapeDtypeStruct(x.shape, F32),
        compiler_params=_cparams(1),
        name="cumsum",
    )(x)
    return out.reshape(b, h, l)


def _lane_tile(a, width):
    return a if width == LANES else jnp.concatenate([a] * (width // LANES), axis=1)


def _online_update(js, ss, vx, m_scr, acc_scr, shift=None):
    tq = ss[0].shape[0]
    ps, alphas = [], []
    for j, s in zip(js, ss):
        m_prev = m_scr[j]
        rmax = jnp.max(s, axis=1, keepdims=True)
        if shift is not None:
            rmax = rmax + shift
        m_new = jnp.maximum(m_prev, rmax)
        sub = m_new if shift is None else m_new - shift
        ps.append(jnp.exp(s - _lane_tile(sub, s.shape[1])).astype(BF16))
        alphas.append(jnp.exp(m_prev - m_new))
        m_scr[j] = m_new
    pv = _dot(ps[0] if len(ps) == 1 else jnp.concatenate(ps, axis=0), vx)
    for i, j in enumerate(js):
        acc_scr[j] = alphas[i] * acc_scr[j] + pv[i * tq:(i + 1) * tq]


def _stack_maps(q_ref, per_pair):
    n_pairs = q_ref.shape[1] // LANES // per_pair
    return [jnp.concatenate([q_ref[:, m * LANES:(m + 1) * LANES] for m in range(p * per_pair, (p + 1) * per_pair)],
                            axis=0) for p in range(n_pairs)]


def _map_scores(q_stacks, kt_ref, start, tk, tq):
    out = []
    for p, qs in enumerate(q_stacks):
        sp = _dot(qs, kt_ref[p * LANES:(p + 1) * LANES, pl.ds(start, tk)])
        out += [sp[i * tq:(i + 1) * tq] for i in range(qs.shape[0] // tq)]
    return out


def _init_online(m_scr, acc_scr):
    m_scr[...] = jnp.full(m_scr.shape, NEG, F32)
    acc_scr[...] = jnp.zeros(acc_scr.shape, F32)


def _normalized(acc):
    return acc[:, 0:HEAD_DIM] / acc[:, HEAD_DIM:HEAD_DIM + 1]


def _attn_scratch(n_state, tq):
    return [pltpu.VMEM((n_state, tq, LANES), F32), pltpu.VMEM((n_state, tq, LANES), F32)]


def _dsa_kernel(qi_ref, w_ref, q_ref, ki_ref, k_ref, v_ref, o_ref, key_scr, gmax_scr, m_scr, acc_scr,
                *, tq, tk, q_off, l_valid, ksel, pos_bits):
    i = pl.program_id(1)
    q0 = q_off + i * tq
    qchunk = (q0 + lax.broadcasted_iota(I32, (tq, 1), 0)) // CHUNK
    n_end = jnp.minimum(((q0 + tq - 1) // CHUNK + 1) * CHUNK, l_valid)
    nblk = (n_end + tk - 1) // tk
    kcol = lax.broadcasted_iota(I32, (1, tk), 1)

    qi = qi_ref[...]
    w = w_ref[...]
    qi_stack = jnp.concatenate([qi[:, g * IDX_DIM:(g + 1) * IDX_DIM] for g in range(IDX_HEADS)], axis=0)
    ws = [jnp.broadcast_to(w[:, MISC_AW + g:MISC_AW + g + 1], (tq, LANES)) for g in range(IDX_HEADS)]

    def idx_body(j, carry, masked):
        start = pl.multiple_of(j * tk, tk)
        s_all = _dot(qi_stack, ki_ref[0:IDX_DIM, pl.ds(start, tk)])
        idx = jnp.zeros((tq, tk), F32)
        for g in range(IDX_HEADS):
            idx = idx + jnp.maximum(s_all[g * tq:(g + 1) * tq], 0.0) * _lane_tile(ws[g], tk)
        bits = lax.bitcast_convert_type(idx, I32)
        key = jnp.where(bits < 0, bits ^ INT_MAX, bits)
        key = jnp.where(idx == 0.0, 0, key)
        if masked:
            kpos = start + kcol
            kchunk = jnp.where(kpos < l_valid, kpos // CHUNK, INT_MAX)
            key = jnp.where(kchunk <= qchunk, key, INT_MIN)
        key_scr[:, pl.ds(start, tk)] = key
        gmax_scr[...] = jnp.maximum(gmax_scr[...], key)
        return carry

    gmax_scr[...] = jnp.full((tq, tk), INT_MIN, I32)
    nfree = jnp.minimum((q0 // CHUNK + 1) * CHUNK, l_valid) // tk
    lax.fori_loop(0, nfree, functools.partial(idx_body, masked=False), 0)
    lax.fori_loop(nfree, nblk, functools.partial(idx_body, masked=True), 0)

    @pl.when(nblk % 2 == 1)
    def _():
        key_scr[:, pl.ds(pl.multiple_of(nblk * tk, tk), tk)] = jnp.full((tq, tk), INT_MIN, I32)

    npair = (nblk + 1) // 2

    rc_coarse = min(tq, 128)
    rc_fine = min(tq, 16)

    lane_col = lax.broadcasted_iota(I32, (1, LANES), 1)

    def count(indicator, *row_args, src=key_scr, nsteps=npair, width=2 * tk, rc=rc_coarse, active=None):
        outs = []
        for ci, r0 in enumerate(range(0, tq, rc)):
            args = [a[r0:r0 + rc] for a in row_args]

            def body(j, acc, r0=r0, args=args):
                for c in range(width // LANES):
                    start = pl.multiple_of(j * width + c * LANES, LANES)
                    acc = acc + indicator(src[r0:r0 + rc, pl.ds(start, LANES)], start + lane_col, *args)
                return acc

            def sweep(body=body):
                acc = lax.fori_loop(0, nsteps, body, jnp.zeros((rc, LANES), F32))
                return jnp.broadcast_to(jnp.sum(acc, axis=1, keepdims=True), (rc, LANES))

            if active is None:
                outs.append(sweep())
            else:
                outs.append(lax.cond(active[ci], sweep, lambda: jnp.zeros((rc, LANES), F32)))
        return outs[0] if len(outs) == 1 else jnp.concatenate(outs, axis=0)

    ge = lambda kt, kp, c: jnp.where(kt >= c, 1.0, 0.0)
    n_adm = jnp.broadcast_to(jnp.minimum((qchunk + 1) * CHUNK, l_valid).astype(F32), (tq, LANES))

    def lb_body(t, ans):
        cand_u = ans | lax.shift_left(jnp.int32(1), 31 - t)
        cnt = count(ge, cand_u ^ INT_MIN, src=gmax_scr, nsteps=1, width=tk)
        return jnp.where(cnt >= ksel, cand_u, ans)

    lb_bits = 16
    lb = lax.fori_loop(0, lb_bits, lb_body, jnp.zeros((tq, LANES), I32)) ^ INT_MIN
    gm = gmax_scr[...]
    gbits = jnp.where(gm < 0, gm ^ INT_MAX, gm)
    gval = jnp.where(gm == INT_MIN, -jnp.inf, lax.bitcast_convert_type(gbits, F32))
    vmax = jnp.broadcast_to(jnp.max(gval, axis=1, keepdims=True), (tq, LANES))
    mbits = lax.bitcast_convert_type(vmax, I32)
    kmax = jnp.where(mbits < 0, mbits ^ INT_MAX, mbits)
    lo0 = jnp.maximum(lb, INT_MIN + 1)
    hi0 = jnp.minimum(kmax, INT_MAX - 1) + 1

    def open_rows(lo, hi, c_lo):
        return jnp.where(c_lo == ksel, 0.0, jnp.where(lo + 1 >= hi, 0.0, jnp.where(n_adm <= ksel, 0.0, 1.0)))

    def chunk_flags(rows01, rc):
        return tuple(jnp.max(rows01[r0:r0 + rc]) > 0.0 for r0 in range(0, tq, rc))

    def n_open(rows01):
        return jnp.sum(rows01[:, 0:1])

    def halve_twice(lo, hi, c_lo, rc, active):
        for _ in range(2):
            mid = (lo >> 1) + (hi >> 1) + (lo & hi & 1)
            cnt = count(ge, mid, rc=rc, active=active)
            take = cnt >= ksel
            lo = jnp.where(take, mid, lo)
            c_lo = jnp.where(take, cnt, c_lo)
            hi = jnp.where(take, hi, mid)
        return lo, hi, c_lo

    few_rows = 16.0

    def coarse_cond(st):
        return jnp.logical_and(st[0] < 34, st[4] > few_rows)

    def coarse_body(st):
        t, lo, hi, c_lo = st[:4]
        lo, hi, c_lo = halve_twice(lo, hi, c_lo, rc_coarse, st[5:])
        rows01 = open_rows(lo, hi, c_lo)
        return (t + 2, lo, hi, c_lo, n_open(rows01)) + chunk_flags(rows01, rc_coarse)

    def fine_cond(st):
        return jnp.logical_and(st[0] < 34, st[4] > 0.0)

    def fine_body(st):
        t, lo, hi, c_lo = st[:4]
        lo, hi, c_lo = halve_twice(lo, hi, c_lo, rc_fine, st[5:])
        rows01 = open_rows(lo, hi, c_lo)
        return (t + 2, lo, hi, c_lo, n_open(rows01)) + chunk_flags(rows01, rc_fine)

    c_lo0 = count(ge, lo0)
    rows01 = open_rows(lo0, hi0, c_lo0)
    st = lax.while_loop(coarse_cond, coarse_body,
                        (jnp.int32(0), lo0, hi0, c_lo0, n_open(rows01)) + chunk_flags(rows01, rc_coarse))
    rows01 = open_rows(*st[1:4])
    st = lax.while_loop(fine_cond, fine_body, st[:4] + (n_open(rows01),) + chunk_flags(rows01, rc_fine))
    tau, c_ge = st[1], st[3]

    tie_rows = jnp.where(c_ge > ksel, 1.0, 0.0)
    tie_chunks = chunk_flags(tie_rows, rc_fine)
    any_tie = jnp.max(tie_rows) > 0.0
    need = ksel - count(lambda kt, kp, c: jnp.where(kt > c, 1.0, 0.0), tau, rc=rc_fine, active=tie_chunks)

    def pos_body(t, p):
        cand = p | lax.shift_left(jnp.int32(1), pos_bits - 1 - t)
        cnt = count(lambda kt, kp, tv, cv: jnp.where(kt == tv, jnp.where(kp < cv, 1.0, 0.0), 0.0), tau, cand,
                    rc=rc_fine, active=tie_chunks)
        return jnp.where(cnt < need, cand, p)

    p_lim = lax.fori_loop(0, jnp.where(any_tie, pos_bits, 0), pos_body, jnp.zeros((tq, LANES), I32))
    p_lim = jnp.where(tie_rows > 0.0, p_lim, INT_MAX)

    _init_online(m_scr, acc_scr)
    q_stacks = _stack_maps(q_ref, 2)
    ta = 2 * tk
    tau_t = _lane_tile(tau, ta)
    p_lim_t = _lane_tile(p_lim, ta)
    acol = lax.broadcasted_iota(I32, (1, ta), 1)

    def att_body(j, carry):
        start = pl.multiple_of(j * ta, ta)
        kt = key_scr[:, pl.ds(start, ta)]
        bias = jnp.where(kt > tau_t, 0.0, jnp.where(kt == tau_t, jnp.where(start + acol <= p_lim_t, 0.0, NEG), NEG))
        ss = _map_scores(q_stacks, k_ref, start, ta, tq)
        for h in range(HEADS):
            _online_update([h], [ss[h] + bias], v_ref[pl.ds(start, ta), h * LANES:(h + 1) * LANES], m_scr, acc_scr)
        return carry

    lax.fori_loop(0, npair, att_body, 0)
    o_ref[...] = jnp.concatenate([_normalized(acc_scr[h]) for h in range(HEADS)], axis=1).astype(BF16)


def _dsa(qi, misc, q, ki, k, v, *, q_off, l_valid, tq):
    b, t, _ = q.shape
    lp = v.shape[1]
    tk = 512
    assert lp % (2 * tk) == 0 and t % tq == 0
    ksel = min(TOPK_MAX, l_valid // 4)
    qtok = lambda width: pl.BlockSpec((None, tq, width), lambda bi, i: (bi, i, 0))
    res = lambda width: pl.BlockSpec((None, lp, width), lambda bi, i: (bi, 0, 0), pipeline_mode=pl.Buffered(1))
    res_t = lambda rows: pl.BlockSpec((None, rows, lp), lambda bi, i: (bi, 0, 0), pipeline_mode=pl.Buffered(1))
    kern = functools.partial(_dsa_kernel, tq=tq, tk=tk, q_off=q_off, l_valid=l_valid, ksel=float(ksel),
                             pos_bits=int(lp - 1).bit_length())
    return pl.pallas_call(
        kern,
        grid=(b, t // tq),
        in_specs=[qtok(IDX_HEADS * IDX_DIM), qtok(MISC_W), qtok(HEADS * LANES), res_t(MISC_W), res_t(GROUP_W),
                  res(HEADS * LANES)],
        out_specs=qtok(GROUP_W),
        out_shape=jax.ShapeDtypeStruct((b, t, GROUP_W), BF16),
        scratch_shapes=[pltpu.VMEM((tq, lp), I32), pltpu.VMEM((tq, tk), I32)] + _attn_scratch(HEADS, tq),
        compiler_params=_cparams(2),
        name="dsa",
    )(qi, misc, q, ki, k, v)


def _fox_kernel(q_ref, cq_ref, k_ref, v_ref, ck_ref, o_ref, m_scr, acc_scr, *, tq, tk, q_off, l_valid):
    i = pl.program_id(1)
    q0 = q_off + i * tq
    qpos = q0 + lax.broadcasted_iota(I32, (tq, 1), 0)
    nblk = (jnp.minimum(q0 + tq, l_valid) + tk - 1) // tk
    nfull = jnp.minimum((q0 + 1) // tk, l_valid // tk)
    kcol = lax.broadcasted_iota(I32, (1, tk), 1)
    cq = cq_ref[...]
    cqs = [jnp.broadcast_to(cq[:, h:h + 1], (tq, LANES)) for h in range(HEADS)]
    _init_online(m_scr, acc_scr)
    q_stacks = _stack_maps(q_ref, 2)

    def body(j, carry, masked):
        start = pl.multiple_of(j * tk, tk)
        if masked:
            kpos = start + kcol
            valid = kpos <= qpos
        ss = _map_scores(q_stacks, k_ref, start, tk, tq)
        for h in range(HEADS):
            s = ss[h] - ck_ref[h:h + 1, pl.ds(start, tk)]
            if masked:
                s = jnp.where(valid, s, NEG)
            _online_update([h], [s], v_ref[pl.ds(start, tk), h * LANES:(h + 1) * LANES], m_scr, acc_scr,
                           shift=cqs[h])
        return carry

    lax.fori_loop(0, nfull, functools.partial(body, masked=False), 0)
    lax.fori_loop(nfull, nblk, functools.partial(body, masked=True), 0)
    o_ref[...] = jnp.concatenate([_normalized(acc_scr[h]) for h in range(HEADS)], axis=1).astype(BF16)


def _fox(q, cq, k, v, ck_t, *, q_off, l_valid, tq):
    b, t, _ = q.shape
    lp = v.shape[1]
    tk = 1024
    assert lp % tk == 0 and t % tq == 0
    qtok = lambda width: pl.BlockSpec((None, tq, width), lambda bi, i: (bi, i, 0))
    res = lambda rows, width: pl.BlockSpec((None, rows, width), lambda bi, i: (bi, 0, 0),
                                           pipeline_mode=pl.Buffered(1))
    kern = functools.partial(_fox_kernel, tq=tq, tk=tk, q_off=q_off, l_valid=l_valid)
    return pl.pallas_call(
        kern,
        grid=(b, t // tq),
        in_specs=[qtok(HEADS * LANES), qtok(HEADS), res(GROUP_W, lp), res(lp, HEADS * LANES), res(HEADS, lp)],
        out_specs=qtok(GROUP_W),
        out_shape=jax.ShapeDtypeStruct((b, t, GROUP_W), BF16),
        scratch_shapes=_attn_scratch(HEADS, tq),
        compiler_params=_cparams(2),
        name="fox",
    )(q, cq, k, v, ck_t)


def _diff_kernel(q_ref, k_ref, v_ref, lam_ref, gsub_ref, o_ref, m_scr, acc_scr,
                 *, tq, tk, q_off, l_valid, out_scale):
    i = pl.program_id(1)
    q0 = q_off + i * tq
    qchunk = (q0 + lax.broadcasted_iota(I32, (tq, 1), 0)) // CHUNK
    n_end = jnp.minimum(((q0 + tq - 1) // CHUNK + 1) * CHUNK, l_valid)
    nblk = (n_end + tk - 1) // tk
    nfull = jnp.minimum(((q0 // CHUNK + 1) * CHUNK) // tk, l_valid // tk)
    kcol = lax.broadcasted_iota(I32, (1, tk), 1)
    _init_online(m_scr, acc_scr)
    q_stacks = _stack_maps(q_ref, 4)

    def body(j, carry, masked):
        start = pl.multiple_of(j * tk, tk)
        if masked:
            kpos = start + kcol
            valid = jnp.where(kpos < l_valid, kpos // CHUNK, INT_MAX) <= qchunk
        ss = _map_scores(q_stacks, k_ref, start, tk, tq)
        if masked:
            ss = [jnp.where(valid, s, NEG) for s in ss]
        for h in range(HEADS):
            _online_update([2 * h, 2 * h + 1], ss[2 * h:2 * h + 2], v_ref[pl.ds(start, tk), h * LANES:(h + 1) * LANES],
                           m_scr, acc_scr)
        return carry

    lax.fori_loop(0, nfull, functools.partial(body, masked=False), 0)
    lax.fori_loop(nfull, nblk, functools.partial(body, masked=True), 0)

    lv = lam_ref[...]
    lam = (jnp.exp(jnp.sum(lv[0:1] * lv[1:2], axis=1, keepdims=True))
           - jnp.exp(jnp.sum(lv[2:3] * lv[3:4], axis=1, keepdims=True)) + (1.0 - out_scale))
    outs = []
    for h in range(HEADS):
        o = _normalized(acc_scr[2 * h]) - lam * _normalized(acc_scr[2 * h + 1])
        outs.append(_rms(o, gsub_ref[...]) * out_scale)
    o_ref[...] = jnp.concatenate(outs, axis=1).astype(BF16)


def _diff(q, k, v, lam_vecs, g_sub, *, q_off, l_valid, tq, lam_init):
    b, t, _ = q.shape
    lp = v.shape[1]
    tk = 512
    assert lp % tk == 0 and t % tq == 0
    qtok = lambda width: pl.BlockSpec((None, tq, width), lambda bi, i: (bi, i, 0))
    res = lambda rows, width: pl.BlockSpec((None, rows, width), lambda bi, i: (bi, 0, 0),
                                           pipeline_mode=pl.Buffered(1))
    const = lambda shape: pl.BlockSpec(shape, lambda bi, i: (0, 0))
    kern = functools.partial(_diff_kernel, tq=tq, tk=tk, q_off=q_off, l_valid=l_valid, out_scale=1.0 - lam_init)
    return pl.pallas_call(
        kern,
        grid=(b, t // tq),
        in_specs=[qtok(2 * HEADS * LANES), res(GROUP_W, lp), res(lp, HEADS * LANES), const((4, DIFF_DIM)),
                  const((1, HEAD_DIM))],
        out_specs=qtok(GROUP_W),
        out_shape=jax.ShapeDtypeStruct((b, t, GROUP_W), BF16),
        scratch_shapes=_attn_scratch(2 * HEADS, tq),
        compiler_params=_cparams(2),
        name="diff",
    )(q, k, v, lam_vecs, g_sub.reshape(1, HEAD_DIM))


def _relbias_kernel(rb_ref, o_ref, *, tq):
    w = BAND_PREV + tq
    pw = 2 * w
    j = lax.broadcasted_iota(I32, (8, pw), 1)
    off = jnp.where(j < w, j, j - pw)
    idx = jnp.clip(BAND_PREV - off, -REL_CLIP, REL_CLIP) + REL_CLIP

    def body(r, accs):
        return tuple(jnp.where(idx == r, rb_ref[h, r], accs[h]) for h in range(HEADS))

    profs = lax.fori_loop(0, 2 * REL_CLIP + 1, body, tuple(jnp.zeros((8, pw), F32) for _ in range(HEADS)))
    qc = BAND_CHUNKS + lax.broadcasted_iota(I32, (tq, w), 0) // CHUNK
    kc = lax.broadcasted_iota(I32, (tq, w), 1) // CHUNK
    for h in range(HEADS):
        full = jnp.broadcast_to(profs[h][0:1], (tq, pw))
        t = pltpu.roll(full, 0, 1, stride=1, stride_axis=0)[:, 0:w]
        t = jnp.where(kc <= qc, t, NEG)
        o_ref[h] = jnp.where(kc >= qc - BAND_CHUNKS, t, NEG)


def _relbias(rel_bias_l, tq):
    return pl.pallas_call(
        functools.partial(_relbias_kernel, tq=tq),
        in_specs=[pl.BlockSpec(memory_space=pltpu.SMEM)],
        out_specs=pl.BlockSpec(memory_space=pltpu.VMEM),
        out_shape=jax.ShapeDtypeStruct((HEADS, tq, BAND_PREV + tq), F32),
        compiler_params=pltpu.CompilerParams(vmem_limit_bytes=VMEM_LIMIT_BYTES),
        name="relbias",
    )(rel_bias_l.astype(F32))


def _band_kernel(q_ref, kp_ref, kc_ref, vp_ref, vc_ref, bias_ref, o_ref, k_scr, v_scr, *, tq, first_has_no_prev):
    i = pl.program_id(1)
    w = BAND_PREV + tq
    k_scr[0:BAND_PREV, :] = kp_ref[...]
    k_scr[BAND_PREV:w, :] = kc_ref[...]
    v_scr[0:BAND_PREV, :] = vp_ref[...]
    v_scr[BAND_PREV:w, :] = vc_ref[...]
    kcol = lax.broadcasted_iota(I32, (1, w), 1)
    first = jnp.where(i == 0, BAND_PREV, 0) if first_has_no_prev else 0
    outs = []
    for h in range(HEADS):
        hs = slice(h * HEAD_DIM, (h + 1) * HEAD_DIM)
        s = _dot_nt(q_ref[:, hs], k_scr[:, hs]) + bias_ref[h]
        if first_has_no_prev:
            s = jnp.where(kcol >= first, s, NEG)
        e = jnp.exp(s - jnp.max(s, axis=1, keepdims=True))
        o = _dot(e.astype(BF16), v_scr[:, hs])
        outs.append(o / jnp.sum(e, axis=1, keepdims=True))
    o_ref[...] = jnp.concatenate(outs, axis=1).astype(BF16)


def _band(q, k_prev, k_cur, v_prev, v_cur, bias, *, tq, prev_from_cur):
    b, t, _ = q.shape
    assert t % tq == 0 and (not prev_from_cur or tq == BAND_PREV)
    qtok = pl.BlockSpec((None, tq, GROUP_W), lambda bi, i: (bi, i, 0))
    if prev_from_cur:
        prev = pl.BlockSpec((None, BAND_PREV, GROUP_W), lambda bi, i: (bi, jnp.maximum(i - 1, 0), 0))
    else:
        prev = pl.BlockSpec((None, BAND_PREV, GROUP_W), lambda bi, i: (bi, 0, 0))
    bspec = pl.BlockSpec((HEADS, tq, BAND_PREV + tq), lambda bi, i: (0, 0, 0))
    kern = functools.partial(_band_kernel, tq=tq, first_has_no_prev=prev_from_cur)
    return pl.pallas_call(
        kern,
        grid=(b, t // tq),
        in_specs=[qtok, prev, qtok, prev, qtok, bspec],
        out_specs=qtok,
        out_shape=jax.ShapeDtypeStruct((b, t, GROUP_W), BF16),
        scratch_shapes=[pltpu.VMEM((BAND_PREV + tq, GROUP_W), BF16), pltpu.VMEM((BAND_PREV + tq, GROUP_W), BF16)],
        compiler_params=_cparams(2),
        name="band",
    )(q, k_prev, k_cur, v_prev, v_cur, bias)


def _memkv_kernel(m_ref, g_ref, wk_ref, wv_ref, kf_o, vf_o, kb_o, vb_o):
    hb = _rms(m_ref[...], g_ref[...]).astype(BF16)
    k = _dot(hb, wk_ref[...])
    v = _dot(hb, wv_ref[...])
    kf_o[...] = k
    vf_o[...] = v
    kb_o[...] = k.astype(BF16)
    vb_o[...] = v.astype(BF16)


def _memkv(mem, g, wk, wv):
    b, m, d = mem.shape
    xw = wk.shape[1]
    tok = lambda width: pl.BlockSpec((None, m, width), lambda bi: (bi, 0, 0))
    const = lambda shape: pl.BlockSpec(shape, lambda bi: (0, 0))
    return pl.pallas_call(
        _memkv_kernel,
        grid=(b,),
        in_specs=[tok(d), const((1, d)), const((d, xw)), const((d, xw))],
        out_specs=[tok(xw)] * 4,
        out_shape=[jax.ShapeDtypeStruct((b, m, xw), F32)] * 2 + [jax.ShapeDtypeStruct((b, m, xw), BF16)] * 2,
        compiler_params=_cparams(1),
        name="memkv",
    )(mem, g.reshape(1, d), wk, wv)


def _mix_cross_kernel(x_ref, oa_ref, ob_ref, oc_ref, od_ref, wout_ref, g_ref, wq_ref, mk_ref, mv_ref, wo_ref, o_ref):
    x = x_ref[...]
    for n, oref in enumerate((oa_ref, ob_ref, oc_ref, od_ref)):
        x = x + _dot(oref[...], wout_ref[n * GROUP_W:(n + 1) * GROUP_W, :])
    hb = _rms(x, g_ref[...]).astype(BF16)
    q = _dot(hb, wq_ref[...])
    outs = []
    for h in range(X_HEADS):
        hs = slice(h * X_HEAD_DIM, (h + 1) * X_HEAD_DIM)
        s = _dot_nt(q[:, hs].astype(BF16), mk_ref[:, hs]) * (X_HEAD_DIM ** -0.5)
        e = jnp.exp(s - jnp.max(s, axis=1, keepdims=True))
        p = e / jnp.sum(e, axis=1, keepdims=True)
        outs.append(_dot(p.astype(BF16), mv_ref[:, hs]))
    o = jnp.concatenate(outs, axis=1).astype(BF16)
    o_ref[...] = x + _dot(o, wo_ref[...])


def _mix_cross(x, oa, ob, oc, od, w_out, g, wq, mk, mv, wo):
    b, t, d = x.shape
    tm = min(512, t)
    m, xw = mk.shape[1], mk.shape[2]
    tok = lambda width: pl.BlockSpec((None, tm, width), lambda bi, i: (bi, i, 0))
    const = lambda shape: pl.BlockSpec(shape, lambda bi, i: (0, 0))
    mem = pl.BlockSpec((None, m, xw), lambda bi, i: (bi, 0, 0))
    return pl.pallas_call(
        _mix_cross_kernel,
        grid=(b, t // tm),
        in_specs=[tok(d)] + [tok(GROUP_W)] * 4 + [const(w_out.shape), const((1, d)), const(wq.shape), mem, mem,
                                                   const(wo.shape)],
        out_specs=tok(d),
        out_shape=jax.ShapeDtypeStruct((b, t, d), F32),
        compiler_params=_cparams(2),
        name="mix_cross",
    )(x, oa, ob, oc, od, w_out, g.reshape(1, d), wq, mk, mv, wo)


def _ffn_kernel(x_ref, g_ref, prev_ref, wup_ref, cw_ref, cb_ref, wdn_ref, gout_ref, o_ref, conv_ref, halo_scr,
                acc_scr, *, tm, d_ff, fc, norm_out):
    i = pl.program_id(1)

    @pl.when(i == 0)
    def _():
        halo_scr[...] = prev_ref[...]

    x = x_ref[...]
    hb = _rms(x, g_ref[...]).astype(BF16)
    row = lax.broadcasted_iota(I32, (tm, fc), 0)
    for c in range(d_ff // fc):
        cs = slice(c * fc, (c + 1) * fc)
        g = _dot(hb, wup_ref[:, cs])
        u = _dot(hb, wup_ref[:, d_ff + c * fc:d_ff + (c + 1) * fc])
        h0 = halo_scr[0:1, cs]
        h1 = halo_scr[1:2, cs]
        gm1 = jnp.where(row == 0, h1, pltpu.roll(g, 1, 0))
        gm2 = jnp.where(row == 0, h0, jnp.where(row == 1, h1, pltpu.roll(g, 2, 0)))
        a = cb_ref[:, cs] + gm2 * cw_ref[0:1, cs]
        a = a + gm1 * cw_ref[1:2, cs]
        a = a + g * cw_ref[2:3, cs]
        y = (a * (1.0 / (1.0 + jnp.exp(-a)))) * u
        f = _dot(y.astype(BF16), wdn_ref[cs, :])
        if c == 0:
            acc_scr[...] = f
        else:
            acc_scr[...] += f
        tail = g[tm - (CONV_W - 1):tm, :]
        halo_scr[:, cs] = tail
        conv_ref[:, cs] = tail
    y = x + acc_scr[...]
    o_ref[...] = _rms(y, gout_ref[...]) if norm_out else y


def _ffn(x, g, prev, w_up, conv_w, conv_b, w_down, g_out, norm_out):
    b, t, d = x.shape
    d_ff = w_down.shape[0]
    tm = min(1024, t)
    fc = 256
    assert d_ff % fc == 0 and t % tm == 0 and tm >= CONV_W - 1
    tok = pl.BlockSpec((None, tm, d), lambda bi, i: (bi, i, 0))
    const = lambda shape: pl.BlockSpec(shape, lambda bi, i: (0, 0))
    st = pl.BlockSpec((None, CONV_W - 1, d_ff), lambda bi, i: (bi, 0, 0))
    kern = functools.partial(_ffn_kernel, tm=tm, d_ff=d_ff, fc=fc, norm_out=norm_out)
    return pl.pallas_call(
        kern,
        grid=(b, t // tm),
        in_specs=[tok, const((1, d)), st, const(w_up.shape), const(conv_w.shape), const((1, d_ff)),
                  const(w_down.shape), const((1, d))],
        out_specs=[tok, st],
        out_shape=[jax.ShapeDtypeStruct((b, t, d), F32), jax.ShapeDtypeStruct((b, CONV_W - 1, d_ff), F32)],
        scratch_shapes=[pltpu.VMEM((CONV_W - 1, d_ff), F32), pltpu.VMEM((tm, d), F32)],
        compiler_params=_cparams(2),
        name="ffn",
    )(x, g.reshape(1, d), prev, w_up, conv_w, conv_b.reshape(1, d_ff), w_down, g_out.reshape(1, d))


def _pad_rows(a, rows):
    return a if a.shape[1] == rows else jnp.pad(a, ((0, 0), (0, rows - a.shape[1]), (0, 0)))


def _round_up(n, m):
    return (n + m - 1) // m * m


def _value_slabs_host(v):
    b, l = v.shape[:2]
    one = jnp.zeros((b, l, HEADS, HEAD_DIM), BF16).at[..., 0].set(1.0)
    return jnp.concatenate([v.astype(BF16), one], axis=-1).reshape(b, l, HEADS * LANES)


def _mixers(pr, keys, *, q_off, l_valid, tq_a, tq_bd, band_args, band_tq, prev_from_cur, lam_vecs, g_sub,
            lam_init, logf_all):
    lp = keys["av"].shape[1]
    oa = _dsa(pr["aqi"], pr["misc"], pr["aq"], keys["aki"], keys["ak"], keys["av"],
              q_off=q_off, l_valid=l_valid, tq=tq_a)
    lc = _round_up(logf_all.shape[1], 1024)
    c_t = _cumsum_t(jnp.swapaxes(_pad_rows(logf_all, lc), 1, 2))
    t = pr["bq"].shape[1]
    cq = jnp.swapaxes(c_t[:, :, q_off:q_off + t], 1, 2)
    ck_t = c_t[:, :, :lp] if lc >= lp else jnp.pad(c_t, ((0, 0), (0, 0), (0, lp - lc)))
    ob = _fox(pr["bq"], cq, keys["bk"], keys["bv"], ck_t, q_off=q_off, l_valid=l_valid, tq=tq_bd)
    oc = _band(pr["cq"], *band_args, tq=band_tq, prev_from_cur=prev_from_cur)
    od = _diff(pr["dq"], keys["dk"], keys["dv"], lam_vecs, g_sub, q_off=q_off, l_valid=l_valid, tq=tq_bd,
               lam_init=lam_init)
    return oa, ob, oc, od


def kernel(x_prompt, x_sample, mem_prompt, cache_a_k, cache_a_v, cache_a_kidx, cache_b_k, cache_b_v,
           cache_b_logf, cache_c_k, cache_c_v, cache_d_k, cache_d_v, cache_mem_k, cache_mem_v,
           state_ffn_conv, g_mix, w_in, b_forget, rel_bias, lam_q1, lam_k1, lam_q2, lam_k2, g_sub, w_out,
           g_cross, g_mem, w_xq, w_xk, w_xv, w_xo, g_ffn, w_up, conv_w, conv_b, w_down, g_final):
    depth = w_in.shape[0]
    bp, s_len, d_model = x_prompt.shape
    bs, t_len, _ = x_sample.shape
    past = cache_a_k.shape[2]
    d_ff = w_down.shape[1]
    c_keep = min(BAND_PREV, s_len)
    l_s = past + t_len
    lp_s = _round_up(l_s, 1024)

    pos_p = jnp.arange(s_len, dtype=jnp.int32)
    pos_s = jnp.tile(past + jnp.arange(t_len, dtype=jnp.int32), bs)
    h64 = HEAD_DIM // ROT_FRACTION // 2
    h32 = DIFF_DIM // ROT_FRACTION // 2
    t64_p, t32_p = _rope_table(pos_p, h64, HEAD_DIM), _rope_table(pos_p, h32, DIFF_DIM)
    t64_s, t32_s = _rope_table(pos_s, h64, HEAD_DIM), _rope_table(pos_s, h32, DIFF_DIM)

    flat = lambda a: a.reshape(a.shape[0], a.shape[1], -1)
    heads = lambda a, hd=HEAD_DIM: a.reshape(a.shape[0], a.shape[1], a.shape[2] // hd, hd)

    xp, xs = x_prompt, x_sample
    p_states, s_states = [], []
    for l in range(depth):
        lam_init = 0.8 - 0.6 * math.exp(-0.3 * l)
        w_pack, bfp = _pack_w_in(w_in[l], b_forget[l])
        lam_vecs = jnp.stack([lam_q1[l], lam_k1[l], lam_q2[l], lam_k2[l]]).astype(F32)
        bias = _relbias(rel_bias[l], BAND_PREV)
        bias_s = bias[:, :t_len, :BAND_PREV + t_len]
        w_out_b, wq_b, wo_b = w_out[l].astype(BF16), w_xq[l].astype(BF16), w_xo[l].astype(BF16)
        w_up_b, w_dn_b = w_up[l].astype(BF16), w_down[l].astype(BF16)

        pr = _proj(xp, g_mix[l], w_pack, bfp, t64_p, t32_p, c_keep)
        logf = pr["misc"][:, :, MISC_BF:MISC_BF + HEADS]
        oa, ob, oc, od = _mixers(
            pr, pr, q_off=0, l_valid=s_len, tq_a=256, tq_bd=512,
            band_args=(pr["ck"], pr["ck"], pr["cv"], pr["cv"], bias), band_tq=BAND_PREV, prev_from_cur=True,
            lam_vecs=lam_vecs, g_sub=g_sub[l], lam_init=lam_init, logf_all=logf)
        mkf, mvf, mkb, mvb = _memkv(mem_prompt, g_mem[l], w_xk[l].astype(BF16), w_xv[l].astype(BF16))
        xp = _mix_cross(xp, oa, ob, oc, od, w_out_b, g_cross[l], wq_b, mkb, mvb, wo_b)
        last = l == depth - 1
        xp, conv_p = _ffn(xp, g_ffn[l], jnp.zeros((bp, CONV_W - 1, d_ff), F32), w_up_b, conv_w[l], conv_b[l], w_dn_b,
                          g_final, last)
        p_states.append((heads(pr["akf"]), heads(pr["avf"]), pr["misc"][:, :, :IDX_DIM],
                         heads(pr["bkf"]), heads(pr["bvf"]), logf,
                         heads(pr["ckf"]), heads(pr["cvf"]), heads(pr["dkf"]), heads(pr["dvf"]),
                         conv_p, heads(mkf, X_HEAD_DIM), heads(mvf, X_HEAD_DIM)))

        prs = _proj(xs.reshape(1, bs * t_len, d_model), g_mix[l], w_pack, bfp, t64_s, t32_s, bs * t_len)
        prs = {n: a.reshape(bs, t_len, a.shape[-1]) for n, a in prs.items() if n not in ("ak", "aki", "bk", "dk")}
        cat_t = lambda cache, new: jnp.swapaxes(
            _pad_rows(jnp.concatenate([flat(cache), new], axis=1), lp_s).astype(BF16), 1, 2)
        kidx = jnp.concatenate([cache_a_kidx[l], prs["misc"][:, :, :IDX_DIM]], axis=1)
        kidx = jnp.pad(kidx, ((0, 0), (0, lp_s - l_s), (0, MISC_W - IDX_DIM)))
        catv = lambda cache, new: _pad_rows(jnp.concatenate([_value_slabs_host(cache), new], axis=1), lp_s)
        keys = {"aki": jnp.swapaxes(kidx.astype(BF16), 1, 2),
                "ak": cat_t(cache_a_k[l], prs["akf"]), "av": catv(cache_a_v[l], prs["av"]),
                "bk": cat_t(cache_b_k[l], prs["bkf"]), "bv": catv(cache_b_v[l], prs["bv"]),
                "dk": cat_t(cache_d_k[l], prs["dkf"]), "dv": catv(cache_d_v[l], prs["dv"])}
        logf_s = prs["misc"][:, :, MISC_BF:MISC_BF + HEADS]
        logf_all = jnp.concatenate([cache_b_logf[l].astype(F32), logf_s], axis=1)
        oa, ob, oc, od = _mixers(
            prs, keys, q_off=past, l_valid=l_s, tq_a=t_len, tq_bd=t_len,
            band_args=(flat(cache_c_k[l]).astype(BF16), prs["ck"], flat(cache_c_v[l]).astype(BF16), prs["cv"],
                       bias_s),
            band_tq=t_len, prev_from_cur=False,
            lam_vecs=lam_vecs, g_sub=g_sub[l], lam_init=lam_init, logf_all=logf_all)
        xs = _mix_cross(xs, oa, ob, oc, od, w_out_b, g_cross[l], wq_b,
                        flat(cache_mem_k[l]).astype(BF16), flat(cache_mem_v[l]).astype(BF16), wo_b)
        xs, conv_s = _ffn(xs, g_ffn[l], state_ffn_conv[l], w_up_b, conv_w[l], conv_b[l], w_dn_b, g_final, last)
        s_states.append((heads(prs["akf"]), heads(prs["avf"]), prs["misc"][:, :, :IDX_DIM],
                         heads(prs["bkf"]), heads(prs["bvf"]), logf_s,
                         heads(prs["ckf"]), heads(prs["cvf"]), heads(prs["dkf"]), heads(prs["dvf"]), conv_s))

    y_prompt, y_sample = xp, xs
    p_out = [jnp.stack(z, axis=0) for z in zip(*p_states)]
    s_out = [jnp.stack(z, axis=0) for z in zip(*s_states)]
    return (y_prompt, y_sample, *p_out, *s_out)
```
